```python
import math
import jax
import jax.numpy as jnp
from jax import lax
import numpy as np

D_MODEL = 2048
BATCH = 4
SEQ = 2048
DEPTH = 2
DEC_BATCH = 128
DEC_SEQ = 1
PAST_LEN = 2048
PAGE_SIZE = 128

D_RNN = D_MODEL
N_RNN_BLOCKS = 16
RNN_BLOCK = D_RNN // N_RNN_BLOCKS
CONV_W = 4
LRU_C = 8.0
N_HEADS = 16
HEAD_DIM = D_MODEL // N_HEADS
N_KV_HEADS = 8
KV_GROUP = N_HEADS // N_KV_HEADS
N_IDX_HEADS = 16
IDX_DIM = 64
TOPK_MAX = 256
Q_BLOCK = 128
N_BUCKETS = 32
MAX_DISTANCE = 128
D_FF = 5632
N_EXPERTS = 8
TOP_K_EXPERTS = 2
N_DENSE = (DEPTH + 1) // 2
N_MOE = DEPTH // 2
EPS = 1e-6
SPLITS = (D_RNN, D_RNN, N_HEADS * HEAD_DIM, N_KV_HEADS * HEAD_DIM, N_KV_HEADS * HEAD_DIM,
          N_IDX_HEADS * IDX_DIM, IDX_DIM, N_IDX_HEADS, D_MODEL, D_MODEL)
D_IN = sum(SPLITS)

kernel_name = 'hawk_dsa_hybrid_decode_step'


def rms_norm(x, g):
    xf = x.astype(jnp.float32)
    y = xf * lax.rsqrt(jnp.mean(xf * xf, axis=-1, keepdims=True) + EPS)
    return (y * g.astype(jnp.float32)).astype(x.dtype)


def split_cols(z):
    offs = np.cumsum(SPLITS)[:-1].tolist()
    return jnp.split(z, offs, axis=-1)


def modulation(c, w_ada_l, b_ada_l):
    m = jax.nn.silu(c) @ w_ada_l + b_ada_l
    return [t[:, None, :] for t in jnp.split(m, 6, axis=-1)]


def swiglu(h, w1, w3, w2):
    return (jax.nn.silu(h @ w1) * (h @ w3)) @ w2


def moe_swiglu(h, router, w1, w3, w2):
    B, T, D = h.shape
    hf = h.reshape(B * T, D)
    logits = (hf @ router).astype(jnp.float32)
    top_v, top_i = lax.top_k(logits, TOP_K_EXPERTS)
    top_w = jax.nn.softmax(top_v, axis=-1)
    gates = jnp.sum(jax.nn.one_hot(top_i, N_EXPERTS, dtype=jnp.float32) * top_w[..., None], axis=1)
    y = jnp.zeros_like(hf)
    for e in range(N_EXPERTS):
        y = y + gates[:, e:e + 1].astype(h.dtype) * swiglu(hf, w1[e], w3[e], w2[e])
    return y.reshape(B, T, D)


def rel_bucket(dist):
    n = jnp.maximum(dist, 0)
    max_exact = N_BUCKETS // 2
    nf = jnp.maximum(n, 1).astype(jnp.float32)
    large = max_exact + (jnp.log(nf / max_exact) / math.log(MAX_DISTANCE / max_exact)
                         * (N_BUCKETS - max_exact)).astype(jnp.int32)
    large = jnp.minimum(large, N_BUCKETS - 1)
    return jnp.where(n < max_exact, n, large)


def index_scores(q_idx, w_idx, k_idx):
    s = jax.nn.relu(jnp.einsum('bthd,bsd->bths', q_idx, k_idx).astype(jnp.float32))
    return jnp.einsum('bths,bth->bts', s, w_idx.astype(jnp.float32)) * (N_IDX_HEADS * IDX_DIM) ** -0.5


def sparse_attend(q, k_sel, v_sel, sel_idx, valid, q_pos, rel_bias):
    B, T = q.shape[:2]
    qg = q.reshape(B, T, N_KV_HEADS, KV_GROUP, HEAD_DIM)
    logits = jnp.einsum('btngd,btknd->btngk', qg, k_sel).astype(jnp.float32) * HEAD_DIM ** -0.5
    bucket = rel_bucket(q_pos[None, :, None] - sel_idx)
    bias = rel_bias.astype(jnp.float32)[bucket]
    bias = bias.reshape(B, T, -1, N_KV_HEADS, KV_GROUP).transpose(0, 1, 3, 4, 2)
    logits = jnp.where(valid[:, :, None, None, :], logits + bias, -jnp.inf)
    p = jax.nn.softmax(logits, axis=-1).astype(v_sel.dtype)
    o = jnp.einsum('btngk,btknd->btngd', p, v_sel)
    return o.reshape(B, T, N_HEADS * HEAD_DIM)


def gather_rows(src, idx):
    return jax.vmap(lambda sb, ib: sb[ib])(src, idx)


def prompt_sparse_attention(q, k, v, q_idx, w_idx, k_idx, rel_bias):
    B, S = q.shape[:2]
    n_sel = min(TOPK_MAX, S // 4)
    kpos = jnp.arange(S, dtype=jnp.int32)

    def block(i):
        start = i * Q_BLOCK
        qb = lax.dynamic_slice_in_dim(q, start, Q_BLOCK, axis=1)
        qib = lax.dynamic_slice_in_dim(q_idx, start, Q_BLOCK, axis=1)
        wb = lax.dynamic_slice_in_dim(w_idx, start, Q_BLOCK, axis=1)
        qpos = start + jnp.arange(Q_BLOCK, dtype=jnp.int32)
        sc = index_scores(qib, wb, k_idx)
        sc = jnp.where((kpos[None, :] <= qpos[:, None])[None], sc, -jnp.inf)
        _, sel = lax.top_k(sc, n_sel)
        valid = sel <= qpos[None, :, None]
        return sparse_attend(qb, gather_rows(k, sel), gather_rows(v, sel), sel, valid, qpos, rel_bias)

    out = lax.map(block, jnp.arange(S // Q_BLOCK))
    return out.transpose(1, 0, 2, 3).reshape(B, S, N_HEADS * HEAD_DIM)


def sample_sparse_attention(q, k_new, v_new, q_idx, w_idx, k_idx_new,
                            cache_k, cache_v, cache_idx_k, layer, page_table, rel_bias):
    B, T = q.shape[:2]
    past = PAST_LEN
    n_sel = min(TOPK_MAX, (past + T) // 4)
    idx_past = cache_idx_k[layer, page_table].reshape(B, past, IDX_DIM)
    k_idx_all = jnp.concatenate([idx_past.astype(k_idx_new.dtype), k_idx_new], axis=1)
    qpos = past + jnp.arange(T, dtype=jnp.int32)
    kpos = jnp.arange(past + T, dtype=jnp.int32)
    sc = index_scores(q_idx, w_idx, k_idx_all)
    sc = jnp.where((kpos[None, :] <= qpos[:, None])[None], sc, -jnp.inf)
    _, sel = lax.top_k(sc, n_sel)
    valid = sel <= qpos[None, :, None]
    in_past = (sel < past)[..., None, None]
    ps = jnp.minimum(sel, past - 1)
    phys = jax.vmap(lambda pt, ib: pt[ib])(page_table, ps // PAGE_SIZE)
    off = ps % PAGE_SIZE
    jn = jnp.clip(sel - past, 0, T - 1)
    k_sel = jnp.where(in_past, cache_k[layer, phys, off].astype(k_new.dtype), gather_rows(k_new, jn))
    v_sel = jnp.where(in_past, cache_v[layer, phys, off].astype(v_new.dtype), gather_rows(v_new, jn))
    return sparse_attend(q, k_sel, v_sel, sel, valid, qpos, rel_bias)


def rglru_branch(u, gate_in, conv_hist, h0, conv_w, conv_b, wa, ba, wx, bx, lam, reset_first):
    B, T, _ = u.shape
    ext = jnp.concatenate([conv_hist.astype(u.dtype), u], axis=1)
    xc = conv_b
    for j in range(CONV_W):
        xc = xc + ext[:, j:j + T] * conv_w[j]
    new_hist = ext[:, T:]
    xb = xc.reshape(B, T, N_RNN_BLOCKS, RNN_BLOCK)
    r = jax.nn.sigmoid((jnp.einsum('btnc,ncd->btnd', xb, wa).reshape(B, T, D_RNN) + ba).astype(jnp.float32))
    ig = jax.nn.sigmoid((jnp.einsum('btnc,ncd->btnd', xb, wx).reshape(B, T, D_RNN) + bx).astype(jnp.float32))
    log_a = -LRU_C * r * jax.nn.softplus(-lam.astype(jnp.float32))
    a = jnp.exp(log_a)
    mult = jnp.sqrt(-jnp.expm1(2.0 * log_a))
    if reset_first:
        mult = mult.at[:, 0].set(1.0)
    xin = mult * ig * xc.astype(jnp.float32)

    def step(h, inp):
        a_t, x_t = inp
        h = a_t * h + x_t
        return h, h

    h_last, hs = lax.scan(step, h0.astype(jnp.float32), (a.transpose(1, 0, 2), xin.transpose(1, 0, 2)))
    y = hs.transpose(1, 0, 2).astype(u.dtype) * jax.nn.gelu(gate_in)
    return y, h_last, new_hist


def group_layer(l, x, c, conv_hist, h0, reset_first, attend, P):
    B, T = x.shape[:2]
    sh1, sc1, g1, sh2, sc2, g2 = modulation(c, P['w_ada'][l], P['b_ada'][l])
    h = rms_norm(x, P['norm_g'][l, 0]) * (1 + sc1) + sh1
    z = h @ P['w_in'][l]
    u, gate_in, q, k, v, qi, ki, wi, ga, gb = split_cols(z)
    y_rnn, h_last, new_hist = rglru_branch(u, gate_in, conv_hist, h0, P['conv_w'][l], P['conv_b'][l],
                                           P['lru_wa'][l], P['lru_ba'][l], P['lru_wx'][l], P['lru_bx'][l],
                                           P['lru_lambda'][l], reset_first)
    q = q.reshape(B, T, N_HEADS, HEAD_DIM)
    k = k.reshape(B, T, N_KV_HEADS, HEAD_DIM)
    v = v.reshape(B, T, N_KV_HEADS, HEAD_DIM)
    qi = qi.reshape(B, T, N_IDX_HEADS, IDX_DIM)
    y_att = attend(l, q, k, v, qi, wi, ki)
    merged = jax.nn.sigmoid(ga) * (y_rnn @ P['w_branch_a'][l]) + jax.nn.sigmoid(gb) * (y_att @ P['w_branch_b'][l])
    x = x + g1 * (merged @ P['w_out'][l])
    h2 = rms_norm(x, P['norm_g'][l, 1]) * (1 + sc2) + sh2
    if l % 2 == 0:
        i = l // 2
        f = swiglu(h2, P['ffn_w1'][i], P['ffn_w3'][i], P['ffn_w2'][i])
    else:
        i = l // 2
        f = moe_swiglu(h2, P['moe_router'][i], P['moe_w1'][i], P['moe_w3'][i], P['moe_w2'][i])
    x = x + g2 * f
    return x, h_last, new_hist, k, v, ki


def setup_inputs(seed: int = 0) -> dict:
    key = jax.random.key(seed)
    ks = jax.random.split(key, 40)
    nrm = jax.random.normal
    n_pages = PAST_LEN // PAGE_SIZE
    n_used = DEC_BATCH * n_pages
    n_phys = n_used + n_used // 4
    page_table = jax.random.permutation(ks[0], n_phys)[:n_used].reshape(DEC_BATCH, n_pages).astype(jnp.int32)
    u = jax.random.uniform(ks[1], (DEPTH, D_RNN), minval=0.9, maxval=0.999)
    a0 = u ** (1.0 / LRU_C)
    lru_lambda = jnp.log(a0) - jnp.log1p(-a0)
    return {
        'x_prompt': nrm(ks[2], (BATCH, SEQ, D_MODEL), jnp.float32),
        'x_sample': nrm(ks[3], (DEC_BATCH, DEC_SEQ, D_MODEL), jnp.float32),
        'c_prompt': nrm(ks[4], (BATCH, D_MODEL), jnp.float32),
        'c_sample': nrm(ks[5], (DEC_BATCH, D_MODEL), jnp.float32),
        'cache_k': nrm(ks[6], (DEPTH, n_phys, PAGE_SIZE, N_KV_HEADS, HEAD_DIM), jnp.float32),
        'cache_v': nrm(ks[7], (DEPTH, n_phys, PAGE_SIZE, N_KV_HEADS, HEAD_DIM), jnp.float32),
        'cache_idx_k': nrm(ks[8], (DEPTH, n_phys, PAGE_SIZE, IDX_DIM), jnp.float32),
        'state_h': 0.5 * nrm(ks[9], (DEPTH, DEC_BATCH, D_RNN), jnp.float32),
        'state_conv': nrm(ks[10], (DEPTH, DEC_BATCH, CONV_W - 1, D_RNN), jnp.float32),
        'page_table': page_table,
        'w_ada': 0.5 * D_MODEL ** -0.5 * nrm(ks[11], (DEPTH, D_MODEL, 6 * D_MODEL), jnp.float32),
        'b_ada': 0.02 * nrm(ks[12], (DEPTH, 6 * D_MODEL), jnp.float32),
        'norm_g': 1.0 + 0.02 * nrm(ks[13], (DEPTH, 2, D_MODEL), jnp.float32),
        'w_in': D_MODEL ** -0.5 * nrm(ks[14], (DEPTH, D_MODEL, D_IN), jnp.float32),
        'conv_w': 0.5 * nrm(ks[15], (DEPTH, CONV_W, D_RNN), jnp.float32),
        'conv_b': 0.02 * nrm(ks[16], (DEPTH, D_RNN), jnp.float32),
        'lru_wa': RNN_BLOCK ** -0.5 * nrm(ks[17], (DEPTH, N_RNN_BLOCKS, RNN_BLOCK, RNN_BLOCK), jnp.float32),
        'lru_ba': 0.02 * nrm(ks[18], (DEPTH, D_RNN), jnp.float32),
        'lru_wx': RNN_BLOCK ** -0.5 * nrm(ks[19], (DEPTH, N_RNN_BLOCKS, RNN_BLOCK, RNN_BLOCK), jnp.float32),
        'lru_bx': 0.02 * nrm(ks[20], (DEPTH, D_RNN), jnp.float32),
        'lru_lambda': lru_lambda,
        'w_branch_a': D_RNN ** -0.5 * nrm(ks[21], (DEPTH, D_RNN, D_MODEL), jnp.float32),
        'w_branch_b': (N_HEADS * HEAD_DIM) ** -0.5 * nrm(ks[22], (DEPTH, N_HEADS * HEAD_DIM, D_MODEL), jnp.float32),
        'w_out': D_MODEL ** -0.5 * nrm(ks[23], (DEPTH, D_MODEL, D_MODEL), jnp.float32),
        'rel_bias': 0.5 * nrm(ks[24], (N_BUCKETS, N_HEADS), jnp.float32),
        'ffn_w1': D_MODEL ** -0.5 * nrm(ks[25], (N_DENSE, D_MODEL, D_FF), jnp.float32),
        'ffn_w3': D_MODEL ** -0.5 * nrm(ks[26], (N_DENSE, D_MODEL, D_FF), jnp.float32),
        'ffn_w2': D_FF ** -0.5 * nrm(ks[27], (N_DENSE, D_FF, D_MODEL), jnp.float32),
        'moe_router': D_MODEL ** -0.5 * nrm(ks[28], (N_MOE, D_MODEL, N_EXPERTS), jnp.float32),
        'moe_w1': D_MODEL ** -0.5 * nrm(ks[29], (N_MOE, N_EXPERTS, D_MODEL, D_FF), jnp.float32),
        'moe_w3': D_MODEL ** -0.5 * nrm(ks[30], (N_MOE, N_EXPERTS, D_MODEL, D_FF), jnp.float32),
        'moe_w2': D_FF ** -0.5 * nrm(ks[31], (N_MOE, N_EXPERTS, D_FF, D_MODEL), jnp.float32),
        'final_g': 1.0 + 0.02 * nrm(ks[32], (D_MODEL,), jnp.float32),
    }


def reference(x_prompt, x_sample, c_prompt, c_sample, cache_k, cache_v, cache_idx_k, state_h, state_conv,
              page_table, w_ada, b_ada, norm_g, w_in, conv_w, conv_b, lru_wa, lru_ba, lru_wx, lru_bx,
              lru_lambda, w_branch_a, w_branch_b, w_out, rel_bias, ffn_w1, ffn_w3, ffn_w2,
              moe_router, moe_w1, moe_w3, moe_w2, final_g):
    P = dict(w_ada=w_ada, b_ada=b_ada, norm_g=norm_g, w_in=w_in, conv_w=conv_w, conv_b=conv_b,
             lru_wa=lru_wa, lru_ba=lru_ba, lru_wx=lru_wx, lru_bx=lru_bx, lru_lambda=lru_lambda,
             w_branch_a=w_branch_a, w_branch_b=w_branch_b, w_out=w_out,
             ffn_w1=ffn_w1, ffn_w3=ffn_w3, ffn_w2=ffn_w2,
             moe_router=moe_router, moe_w1=moe_w1, moe_w3=moe_w3, moe_w2=moe_w2)

    def attend_prompt(l, q, k, v, qi, wi, ki):
        return prompt_sparse_attention(q, k, v, qi, wi, ki, rel_bias)

    def attend_sample(l, q, k, v, qi, wi, ki):
        return sample_sparse_attention(q, k, v, qi, wi, ki, cache_k, cache_v, cache_idx_k, l, page_table, rel_bias)

    xp, xs = x_prompt, x_sample
    bp = x_prompt.shape[0]
    kp_l, vp_l, ip_l, hp_l, cp_l = [], [], [], [], []
    ks_l, vs_l, is_l, hs_l, cs_l = [], [], [], [], []
    for l in range(DEPTH):
        hist0 = jnp.zeros((bp, CONV_W - 1, D_RNN), x_prompt.dtype)
        h00 = jnp.zeros((bp, D_RNN), jnp.float32)
        xp, hp, cp, kp, vp, ip = group_layer(l, xp, c_prompt, hist0, h00, True, attend_prompt, P)
        xs, hs, cs, ksm, vsm, ism = group_layer(l, xs, c_sample, state_conv[l], state_h[l], False, attend_sample, P)
        kp_l.append(kp); vp_l.append(vp); ip_l.append(ip); hp_l.append(hp); cp_l.append(cp)
        ks_l.append(ksm); vs_l.append(vsm); is_l.append(ism); hs_l.append(hs); cs_l.append(cs)
    y_prompt = rms_norm(xp, final_g)
    y_sample = rms_norm(xs, final_g)
    return (y_prompt, y_sample,
            jnp.stack(kp_l), jnp.stack(vp_l), jnp.stack(ip_l), jnp.stack(hp_l), jnp.stack(cp_l),
            jnp.stack(ks_l), jnp.stack(vs_l), jnp.stack(is_l), jnp.stack(hs_l), jnp.stack(cs_l))
```

```python
import functools
import math

import jax
import jax.numpy as jnp
from jax import lax
from jax.experimental import pallas as pl
from jax.experimental.pallas import tpu as pltpu

F32 = jnp.float32
BF16 = jnp.bfloat16
I32 = jnp.int32

VMEM_LIMIT_BYTES = 56 * 1024 * 1024
LANES = 128

EPS = 1e-6
LRU_C = 8.0
RNN_BLOCK = 128
CONV_W = 4
HEAD_DIM = 128
KV_GROUP = 2
IDX_DIM = 64
TOPK_MAX = 256
Q_BLOCK = 128
N_BUCKETS = 32
MAX_DISTANCE = 128
PAGE = 128
NEG_INF = float("-inf")
INT_MIN = -(2 ** 31)


def _params(*sem):
    return pltpu.CompilerParams(dimension_semantics=sem, vmem_limit_bytes=VMEM_LIMIT_BYTES)


def _dot(a, b):
    return jnp.dot(a, b, preferred_element_type=F32)


def _dot_nt(a, b):
    return lax.dot_general(a, b, (((1,), (1,)), ((), ())), preferred_element_type=F32)


def _dot_tn(a, b):
    return lax.dot_general(a, b, (((0,), (0,)), ((), ())), preferred_element_type=F32)


def _mm_body(x_ref, w_ref, *refs, silu_in, has_bias):
    x = x_ref[...]
    if silu_in:
        x = x.astype(F32)
        x = x * jax.nn.sigmoid(x)
    acc = _dot(x.astype(BF16), w_ref[...].astype(BF16))
    outs = refs
    if has_bias:
        acc = acc + refs[0][...]
        outs = refs[1:]
    for o in outs:
        o[...] = acc.astype(o.dtype)


def _mm(x, w, layer, col0, ncols, out_dtypes, *, tm, tn, bias=None, silu_in=False):
    M, K = x.shape
    assert M % tm == 0 and ncols % tn == 0 and col0 % tn == 0
    cb0 = col0 // tn
    in_specs = [pl.BlockSpec((tm, K), lambda j, i: (i, 0)),
                pl.BlockSpec((None, K, tn), lambda j, i: (layer, 0, cb0 + j))]
    args = [x, w]
    if bias is not None:
        in_specs.append(pl.BlockSpec((None, 1, tn), lambda j, i: (layer, 0, cb0 + j)))
        args.append(bias)
    outs = pl.pallas_call(
        functools.partial(_mm_body, silu_in=silu_in, has_bias=bias is not None),
        grid=(ncols // tn, M // tm),
        in_specs=in_specs,
        out_specs=[pl.BlockSpec((tm, tn), lambda j, i: (i, j)) for _ in out_dtypes],
        out_shape=[jax.ShapeDtypeStruct((M, ncols), dt) for dt in out_dtypes],
        compiler_params=_params("arbitrary", "arbitrary"),
    )(*args)
    return outs


def _rms(x, g):
    return x * lax.rsqrt(jnp.mean(x * x, axis=-1, keepdims=True) + EPS) * g


def _norm_mod_body(x_ref, g_ref, sc_ref, sh_ref, o_ref):
    y = _rms(x_ref[0], g_ref[...])
    o_ref[0] = (y * (1.0 + sc_ref[0]) + sh_ref[0]).astype(o_ref.dtype)


def _norm_mod(x, g, mod, sc_chunk, sh_chunk, *, tt):
    B, T, D = x.shape
    per_tok = mod.shape[1] == T
    tg = tt if per_tok else 1
    mod_spec = lambda chunk: pl.BlockSpec((1, tg, D), lambda b, t: (b, t if per_tok else 0, chunk))
    return pl.pallas_call(
        _norm_mod_body,
        grid=(B, T // tt),
        in_specs=[pl.BlockSpec((1, tt, D), lambda b, t: (b, t, 0)),
                  pl.BlockSpec((1, D), lambda b, t: (0, 0)),
                  mod_spec(sc_chunk), mod_spec(sh_chunk)],
        out_specs=pl.BlockSpec((1, tt, D), lambda b, t: (b, t, 0)),
        out_shape=jax.ShapeDtypeStruct((B, T, D), BF16),
        compiler_params=_params("arbitrary", "arbitrary"),
    )(x, g, mod, mod)


def _final_norm_body(x_ref, g_ref, o_ref):
    o_ref[0] = _rms(x_ref[0], g_ref[...])


def _final_norm(x, g, *, tt):
    B, T, D = x.shape
    return pl.pallas_call(
        _final_norm_body,
        grid=(B, T // tt),
        in_specs=[pl.BlockSpec((1, tt, D), lambda b, t: (b, t, 0)),
                  pl.BlockSpec((1, D), lambda b, t: (0, 0))],
        out_specs=pl.BlockSpec((1, tt, D), lambda b, t: (b, t, 0)),
        out_shape=jax.ShapeDtypeStruct((B, T, D), F32),
        compiler_params=_params("arbitrary", "arbitrary"),
    )(x, g)


def _softplus(x):
    return jnp.maximum(x, 0.0) + jnp.log1p(jnp.exp(-jnp.abs(x)))


def _expm1(x):
    u = jnp.exp(x)
    return jnp.where(u == 1.0, x, jnp.where(u == 0.0, -1.0, (u - 1.0) * x / jnp.log(u)))


def _block_diag_dot(xc, w_ref):
    nb = w_ref.shape[0]
    outs = []
    for n in range(nb):
        xb = xc[:, n * RNN_BLOCK:(n + 1) * RNN_BLOCK].astype(BF16)
        outs.append(_dot(xb, w_ref[n].astype(BF16)))
    return jnp.concatenate(outs, axis=-1)


def _lru_gates(xc, wa_ref, ba, wx_ref, bx, lam):
    r = jax.nn.sigmoid(_block_diag_dot(xc, wa_ref) + ba)
    ig = jax.nn.sigmoid(_block_diag_dot(xc, wx_ref) + bx)
    log_a = (-LRU_C * r) * _softplus(-lam)
    a = jnp.exp(log_a)
    mult = jnp.sqrt(-_expm1(2.0 * log_a))
    return a, mult, ig


HIST = CONV_W - 1
EXT_PAD = 8


def _rglru_prompt_body(u_ref, gt_ref, cw_ref, cb_ref, wa_ref, ba_ref, wx_ref, bx_ref, lam_ref,
                       y_ref, hl_ref, nh_ref, ext_s, a_s, x_s, h_s):
    c = pl.program_id(1)
    tc = u_ref.shape[1]

    @pl.when(c == 0)
    def _():
        ext_s[0:EXT_PAD, :] = jnp.zeros((EXT_PAD, ext_s.shape[1]), F32)
        h_s[...] = jnp.zeros(h_s.shape, F32)

    @pl.when(c > 0)
    def _():
        ext_s[0:EXT_PAD, :] = ext_s[tc:tc + EXT_PAD, :]

    ext_s[EXT_PAD:EXT_PAD + tc, :] = u_ref[0]
    xc = cb_ref[...]
    for j in range(CONV_W):
        off = EXT_PAD - HIST + j
        xc = xc + ext_s[off:off + tc, :] * cw_ref[j:j + 1, :]
    a, mult, ig = _lru_gates(xc, wa_ref, ba_ref[...], wx_ref, bx_ref[...], lam_ref[...])
    row = c * tc + lax.broadcasted_iota(I32, (tc, 1), 0)
    mult = jnp.where(row == 0, 1.0, mult)
    a_s[...] = a
    x_s[...] = mult * ig * xc

    def step(t, h):
        h = a_s[pl.ds(t, 1), :] * h + x_s[pl.ds(t, 1), :]
        x_s[pl.ds(t, 1), :] = h
        return h

    h = lax.fori_loop(0, tc, step, h_s[0:1, :], unroll=8)
    h_s[0:1, :] = h
    y_ref[0] = (x_s[...] * jax.nn.gelu(gt_ref[0])).astype(y_ref.dtype)

    @pl.when(c == pl.num_programs(1) - 1)
    def _():
        hl_ref[0] = h
        nh_ref[0] = ext_s[EXT_PAD + tc - HIST:EXT_PAD + tc, :]


def _rglru_prompt(zug, P, l, *, tc):
    B, T, C2 = zug.shape
    C = C2 // 2
    nb = C // RNN_BLOCK
    vec = lambda: pl.BlockSpec((None, 1, C), lambda b, c: (l, 0, 0))
    blk = lambda: pl.BlockSpec((None, nb, RNN_BLOCK, RNN_BLOCK), lambda b, c: (l, 0, 0, 0))
    return pl.pallas_call(
        _rglru_prompt_body,
        grid=(B, T // tc),
        in_specs=[pl.BlockSpec((1, tc, C), lambda b, c: (b, c, 0)),
                  pl.BlockSpec((1, tc, C), lambda b, c: (b, c, 1)),
                  pl.BlockSpec((None, CONV_W, C), lambda b, c: (l, 0, 0)),
                  vec(), blk(), vec(), blk(), vec(), vec()],
        out_specs=[pl.BlockSpec((1, tc, C), lambda b, c: (b, c, 0)),
                   pl.BlockSpec((1, 1, C), lambda b, c: (b, 0, 0)),
                   pl.BlockSpec((1, HIST, C), lambda b, c: (b, 0, 0))],
        out_shape=[jax.ShapeDtypeStruct((B, T, C), BF16),
                   jax.ShapeDtypeStruct((B, 1, C), F32),
                   jax.ShapeDtypeStruct((B, HIST, C), F32)],
        scratch_shapes=[pltpu.VMEM((tc + EXT_PAD, C), F32), pltpu.VMEM((tc, C), F32),
                        pltpu.VMEM((tc, C), F32), pltpu.VMEM((8, C), F32)],
        compiler_params=_params("arbitrary", "arbitrary"),
    )(zug, zug, P["conv_w"], P["conv_b3"], P["lru_wa"], P["lru_ba3"], P["lru_wx"], P["lru_bx3"],
      P["lru_lambda3"])


def _rglru_sample_body(u_ref, gt_ref, hist_ref, h0_ref, cw_ref, cb_ref, wa_ref, ba_ref, wx_ref, bx_ref,
                       lam_ref, y_ref, h_ref):
    u = u_ref[...]
    xc = cb_ref[...]
    for j in range(HIST):
        xc = xc + hist_ref[j] * cw_ref[j:j + 1, :]
    xc = xc + u * cw_ref[HIST:HIST + 1, :]
    a, mult, ig = _lru_gates(xc, wa_ref, ba_ref[...], wx_ref, bx_ref[...], lam_ref[...])
    h = a * h0_ref[...] + mult * ig * xc
    h_ref[...] = h
    y_ref[...] = (h * jax.nn.gelu(gt_ref[...])).astype(y_ref.dtype)


def _rglru_sample(zug, hist_t, h0, P, l):
    Bs, C2 = zug.shape
    C = C2 // 2
    nb = C // RNN_BLOCK
    vec = lambda: pl.BlockSpec((None, 1, C), lambda i: (l, 0, 0))
    blk = lambda: pl.BlockSpec((None, nb, RNN_BLOCK, RNN_BLOCK), lambda i: (l, 0, 0, 0))
    return pl.pallas_call(
        _rglru_sample_body,
        grid=(1,),
        in_specs=[pl.BlockSpec((Bs, C), lambda i: (0, 0)),
                  pl.BlockSpec((Bs, C), lambda i: (0, 1)),
                  pl.BlockSpec((HIST, Bs, C), lambda i: (0, 0, 0)),
                  pl.BlockSpec((Bs, C), lambda i: (0, 0)),
                  pl.BlockSpec((None, CONV_W, C), lambda i: (l, 0, 0)),
                  vec(), blk(), vec(), blk(), vec(), vec()],
        out_specs=[pl.BlockSpec((Bs, C), lambda i: (0, 0)),
                   pl.BlockSpec((Bs, C), lambda i: (0, 0))],
        out_shape=[jax.ShapeDtypeStruct((Bs, C), BF16), jax.ShapeDtypeStruct((Bs, C), F32)],
        compiler_params=_params("arbitrary"),
    )(zug, zug, hist_t, h0, P["conv_w"], P["conv_b3"], P["lru_wa"], P["lru_ba3"], P["lru_wx"],
      P["lru_bx3"], P["lru_lambda3"])


def _rel_bucket(dist):
    n = jnp.maximum(dist, 0)
    max_exact = N_BUCKETS // 2
    nf = jnp.maximum(n, 1).astype(F32)
    large = max_exact + (jnp.log(nf / max_exact) / math.log(MAX_DISTANCE / max_exact)
                         * (N_BUCKETS - max_exact)).astype(I32)
    large = jnp.minimum(large, N_BUCKETS - 1)
    return jnp.where(n < max_exact, n, large)


def _sort_key(x):
    bits = pltpu.bitcast(x, I32)
    return jnp.where(bits < 0, bits ^ jnp.int32(0x7FFFFFFF), bits)


def _kth_largest_key(count_ge, shape, k):
    c = count_ge(jnp.zeros(shape, I32))
    t = jnp.where(c >= k, jnp.int32(0), jnp.int32(INT_MIN))
    for bit in range(30, -1, -1):
        cand = t + jnp.int32(1 << bit)
        c = count_ge(cand)
        t = jnp.where(c >= k, cand, t)
    return t


def _attn_prompt_body(rb_ref, q_ref, k_ref, v_ref, qi_ref, ki_ref, wit_ref, o_ref,
                      key_s, msk_s, lg_s, bias_s, *, n_sel, n_heads, n_idx_heads):
    b = pl.program_id(0)
    i = pl.program_id(1)
    QB = Q_BLOCK
    nblk = i + 1
    s_io = lax.broadcasted_iota(I32, (QB, QB), 0)
    t_io = lax.broadcasted_iota(I32, (QB, QB), 1)

    @pl.when((b == 0) & (i == 0))
    def _():
        for d in range(2):
            bucket = _rel_bucket(t_io - s_io + d * QB)
            for h in range(n_heads):
                acc = jnp.zeros((QB, QB), F32)
                for bb in range(N_BUCKETS):
                    acc = jnp.where(bucket == bb, rb_ref[bb, h], acc)
                bias_s[d, h] = acc
        for h in range(n_heads):
            bias_s[2, h] = jnp.full((QB, QB), rb_ref[N_BUCKETS - 1, h], F32)

    qi = qi_ref[0]
    qi_heads = [qi[:, h * IDX_DIM:(h + 1) * IDX_DIM] for h in range(n_idx_heads)]
    wt = wit_ref[0] * (n_idx_heads * IDX_DIM) ** -0.5

    def score_block(j, carry):
        ks = pl.multiple_of(j * QB, QB)
        kib = ki_ref[0, pl.ds(ks, QB), :]
        st = jnp.zeros((QB, QB), F32)
        for h in range(n_idx_heads):
            st = st + jnp.maximum(_dot_nt(kib, qi_heads[h]), 0.0) * wt[h:h + 1, :]
        st = jnp.where(s_io > t_io + jnp.where(j == i, 0, QB), NEG_INF, st)
        key_s[pl.ds(ks, QB), :] = _sort_key(st)
        return carry

    lax.fori_loop(0, nblk, score_block, 0)

    def count_ge(cand):
        def body(j, acc):
            blk = key_s[pl.ds(pl.multiple_of(j * QB, QB), QB), :]
            return acc + jnp.sum(jnp.where(blk >= cand, 1.0, 0.0), axis=0, keepdims=True)
        return lax.fori_loop(0, nblk, body, jnp.zeros((1, QB), F32))

    thr = _kth_largest_key(count_ge, (1, QB), n_sel)

    def count_gt(j, acc):
        blk = key_s[pl.ds(pl.multiple_of(j * QB, QB), QB), :]
        return acc + jnp.sum(jnp.where(blk > thr, 1.0, 0.0), axis=0, keepdims=True)

    n_gt = lax.fori_loop(0, nblk, count_gt, jnp.zeros((1, QB), F32))
    need = n_sel - n_gt
    ltri = jnp.where(t_io <= s_io, 1.0, 0.0).astype(BF16)

    def mask_block(j, carry):
        ks = pl.multiple_of(j * QB, QB)
        blk = key_s[pl.ds(ks, QB), :]
        eq = blk == thr
        rank = carry + _dot(ltri, jnp.where(eq, 1.0, 0.0).astype(BF16))
        keep = jnp.where(blk > thr, 1.0, jnp.where(eq & (rank <= need), 1.0, 0.0))
        keep = jnp.where(s_io > t_io + jnp.where(j == i, 0, QB), 0.0, keep)
        msk_s[pl.ds(ks, QB), :] = jnp.where(keep > 0.0, 0.0, NEG_INF)
        return rank[QB - 1:QB, :]

    lax.fori_loop(0, nblk, mask_block, jnp.zeros((1, QB), F32))

    scale = HEAD_DIM ** -0.5
    for g in range(n_heads // KV_GROUP):
        heads = [g * KV_GROUP + r for r in range(KV_GROUP)]
        qp = jnp.concatenate([q_ref[0, :, h * HEAD_DIM:(h + 1) * HEAD_DIM] for h in heads], axis=0)

        def logits_block(j, m):
            ks = pl.multiple_of(j * QB, QB)
            kb = k_ref[0, pl.ds(ks, QB), g * HEAD_DIM:(g + 1) * HEAD_DIM]
            lt = _dot_nt(kb, qp) * scale
            dsel = jnp.minimum(i - j, 2)
            bias = jnp.concatenate([bias_s[dsel, h] for h in heads], axis=1)
            mk = msk_s[pl.ds(ks, QB), :]
            lt = lt + bias + jnp.concatenate([mk] * KV_GROUP, axis=1)
            lg_s[pl.ds(ks, QB), :] = lt
            return jnp.maximum(m, jnp.max(lt, axis=0, keepdims=True))

        m = lax.fori_loop(0, nblk, logits_block, jnp.full((1, KV_GROUP * QB), NEG_INF, F32))

        def pv_block(j, carry):
            l, acc = carry
            ks = pl.multiple_of(j * QB, QB)
            p = jnp.exp(lg_s[pl.ds(ks, QB), :] - m)
            vb = v_ref[0, pl.ds(ks, QB), g * HEAD_DIM:(g + 1) * HEAD_DIM]
            acc = acc + _dot_tn(vb, p.astype(BF16))
            return l + jnp.sum(p, axis=0, keepdims=True), acc

        l, acc = lax.fori_loop(0, nblk, pv_block, (jnp.zeros((1, KV_GROUP * QB), F32),
                                                   jnp.zeros((HEAD_DIM, KV_GROUP * QB), F32)))
        ot = acc / l
        for r, h in enumerate(heads):
            o_ref[0, :, h * HEAD_DIM:(h + 1) * HEAD_DIM] = ot[:, r * QB:(r + 1) * QB].T.astype(o_ref.dtype)


def _attn_prompt(q, kv, qi, ki, wit, rel_bias):
    B, S, HD = q.shape
    n_heads = HD // HEAD_DIM
    KD = kv.shape[2] // 2
    n_idx_heads = wit.shape[1]
    n_sel = min(TOPK_MAX, S // 4)
    QB = Q_BLOCK
    return pl.pallas_call(
        functools.partial(_attn_prompt_body, n_sel=n_sel, n_heads=n_heads, n_idx_heads=n_idx_heads),
        grid=(B, S // QB),
        in_specs=[pl.BlockSpec(memory_space=pltpu.SMEM),
                  pl.BlockSpec((1, QB, HD), lambda b, i: (b, i, 0)),
                  pl.BlockSpec((1, S, KD), lambda b, i: (b, 0, 0)),
                  pl.BlockSpec((1, S, KD), lambda b, i: (b, 0, 1)),
                  pl.BlockSpec((1, QB, qi.shape[2]), lambda b, i: (b, i, 0)),
                  pl.BlockSpec((1, S, IDX_DIM), lambda b, i: (b, 0, 0)),
                  pl.BlockSpec((1, n_idx_heads, QB), lambda b, i: (b, 0, i))],
        out_specs=pl.BlockSpec((1, QB, HD), lambda b, i: (b, i, 0)),
        out_shape=jax.ShapeDtypeStruct((B, S, HD), BF16),
        scratch_shapes=[pltpu.VMEM((S, QB), I32), pltpu.VMEM((S, QB), F32),
                        pltpu.VMEM((S, KV_GROUP * QB), F32),
                        pltpu.VMEM((3, n_heads, QB, QB), F32)],
        compiler_params=_params("arbitrary", "arbitrary"),
    )(rel_bias, q, kv, kv, qi, ki, wit)


SCORE_ROWS = 24


def _smp_scores_body(pt_ref, qi_ref, wcol_ref, kin_ref, *refs, n_pages, n_idx_heads):
    pages = refs[:n_pages]
    out_ref = refs[n_pages]
    q16 = qi_ref[0].astype(BF16)
    wcol = wcol_ref[0] * (n_idx_heads * IDX_DIM) ** -0.5
    rows = []
    for p in range(n_pages):
        d = _dot_nt(q16, pages[p][...].astype(BF16))
        rows.append(jnp.sum(jnp.maximum(d, 0.0) * wcol, axis=0, keepdims=True))
    kin = kin_ref[0].astype(BF16).astype(F32)
    dn = jnp.sum(q16.astype(F32) * kin, axis=1, keepdims=True)
    snew = jnp.sum(jnp.maximum(dn, 0.0) * wcol, axis=0, keepdims=True)
    lane = lax.broadcasted_iota(I32, (1, PAGE), 1)
    rows.append(jnp.where(lane == 0, snew, NEG_INF))
    rows.append(jnp.full((SCORE_ROWS - n_pages - 1, PAGE), NEG_INF, F32))
    out_ref[0] = jnp.concatenate(rows, axis=0)


def _smp_scores(page_table, qi3, wcol, kin3, cache_idx_k, layer):
    Bs, n_pages = page_table.shape
    n_idx_heads = qi3.shape[1]
    page_spec = lambda p: pl.BlockSpec((None, None, PAGE, IDX_DIM), lambda b, pt: (layer, pt[b, p], 0, 0))
    grid_spec = pltpu.PrefetchScalarGridSpec(
        num_scalar_prefetch=1,
        grid=(Bs,),
        in_specs=[pl.BlockSpec((1, n_idx_heads, IDX_DIM), lambda b, pt: (b, 0, 0)),
                  pl.BlockSpec((1, n_idx_heads, 1), lambda b, pt: (b, 0, 0)),
                  pl.BlockSpec((1, 1, IDX_DIM), lambda b, pt: (b, 0, 0))]
        + [page_spec(p) for p in range(n_pages)],
        out_specs=pl.BlockSpec((1, SCORE_ROWS, PAGE), lambda b, pt: (b, 0, 0)),
    )
    return pl.pallas_call(
        functools.partial(_smp_scores_body, n_pages=n_pages, n_idx_heads=n_idx_heads),
        grid_spec=grid_spec,
        out_shape=jax.ShapeDtypeStruct((Bs, SCORE_ROWS, PAGE), F32),
        compiler_params=_params("arbitrary"),
    )(page_table, qi3, wcol, kin3, *([cache_idx_k] * n_pages))


def _smp_select_body(sc_ref, o_ref, key_s, *, n_sel):
    Bs, W = sc_ref.shape
    key_s[...] = _sort_key(sc_ref[...])

    def count_ge(cand):
        return jnp.sum(jnp.where(key_s[...] >= cand, 1.0, 0.0), axis=1, keepdims=True)

    thr = _kth_largest_key(count_ge, (Bs, 1), n_sel)
    n_gt = jnp.sum(jnp.where(key_s[...] > thr, 1.0, 0.0), axis=1, keepdims=True)
    need = n_sel - n_gt
    r_io = lax.broadcasted_iota(I32, (LANES, LANES), 0)
    c_io = lax.broadcasted_iota(I32, (LANES, LANES), 1)
    utri = jnp.where(r_io <= c_io, 1.0, 0.0).astype(BF16)
    carry = jnp.zeros((Bs, 1), F32)
    for j in range(W // LANES):
        blk = key_s[:, j * LANES:(j + 1) * LANES]
        eq = blk == thr
        rank = carry + _dot(jnp.where(eq, 1.0, 0.0).astype(BF16), utri)
        keep = jnp.where(blk > thr, 1.0, jnp.where(eq & (rank <= need), 1.0, 0.0))
        o_ref[:, j * LANES:(j + 1) * LANES] = jnp.where(keep > 0.0, 0.0, NEG_INF)
        carry = rank[:, LANES - 1:LANES]


def _smp_select(scores, n_sel):
    Bs, W = scores.shape
    return pl.pallas_call(
        functools.partial(_smp_select_body, n_sel=n_sel),
        grid=(1,),
        in_specs=[pl.BlockSpec((Bs, W), lambda i: (0, 0))],
        out_specs=pl.BlockSpec((Bs, W), lambda i: (0, 0)),
        out_shape=jax.ShapeDtypeStruct((Bs, W), F32),
        scratch_shapes=[pltpu.VMEM((Bs, W), I32)],
        compiler_params=_params("arbitrary"),
    )(scores)


def _smp_attn_body(pt_ref, rb_ref, q_ref, kn_ref, vn_ref, mt_ref, *refs, n_pages, n_heads, past):
    kpages = refs[:n_pages]
    vpages = refs[n_pages:2 * n_pages]
    o_ref = refs[2 * n_pages]
    lg_s, bias_s = refs[2 * n_pages + 1:]
    n_groups = n_heads // KV_GROUP
    lane = lax.broadcasted_iota(I32, (1, LANES), 1)
    head_ok = lane < n_heads

    @pl.when(pl.program_id(0) == 0)
    def _():
        s_io = lax.broadcasted_iota(I32, (PAGE, LANES), 0)
        bucket = _rel_bucket(PAGE - s_io)
        acc = jnp.zeros((PAGE, LANES), F32)
        for bb in range(N_BUCKETS):
            acc = jnp.where(bucket == bb, rb_ref[bb:bb + 1, :], acc)
        bias_s[...] = acc

    qpad = jnp.concatenate([q_ref[0].astype(F32), jnp.zeros((LANES - n_heads, HEAD_DIM), F32)], axis=0)
    qt = qpad.T
    qg = [jnp.where((lane // KV_GROUP) == g, qt, 0.0).astype(BF16) for g in range(n_groups)]
    scale = HEAD_DIM ** -0.5
    far_bias = rb_ref[N_BUCKETS - 1:N_BUCKETS, :]
    mt = mt_ref[0]

    m = jnp.full((1, LANES), NEG_INF, F32)
    for p in range(n_pages):
        lt = jnp.zeros((PAGE, LANES), F32)
        for g in range(n_groups):
            lt = lt + _dot(kpages[p][:, g, :].astype(BF16), qg[g])
        bias = bias_s[...] if p == n_pages - 1 else far_bias
        lt = lt * scale + bias + mt[:, p:p + 1]
        lg_s[p] = lt
        m = jnp.maximum(m, jnp.max(lt, axis=0, keepdims=True))
    kn = kn_ref[0].astype(BF16)
    row8 = lax.broadcasted_iota(I32, (n_groups, 1), 0)
    ln = jnp.zeros((n_groups, LANES), F32)
    for g in range(n_groups):
        ln = ln + _dot(jnp.where(row8 == g, kn, jnp.zeros_like(kn)), qg[g])
    ln = jnp.sum(ln, axis=0, keepdims=True) * scale + rb_ref[0:1, :] + mt[0:1, n_pages:n_pages + 1]
    m = jnp.maximum(m, ln)

    pn = jnp.exp(ln - m)
    l = pn
    acc = [jnp.zeros((HEAD_DIM, LANES), F32) for _ in range(n_groups)]
    for p in range(n_pages):
        pr = jnp.exp(lg_s[p] - m)
        l = l + jnp.sum(pr, axis=0, keepdims=True)
        prb = pr.astype(BF16)
        for g in range(n_groups):
            acc[g] = acc[g] + _dot_tn(vpages[p][:, g, :].astype(BF16), prb)
    vn = vn_ref[0].astype(BF16)
    pn8 = jnp.broadcast_to(pn, (n_groups, LANES)).astype(BF16)
    ot = jnp.zeros((HEAD_DIM, LANES), F32)
    for g in range(n_groups):
        og = acc[g] + _dot_tn(jnp.where(row8 == g, vn, jnp.zeros_like(vn)), pn8)
        ot = ot + jnp.where((lane // KV_GROUP) == g, og, 0.0)
    ot = ot / jnp.where(head_ok, l, 1.0)
    o_ref[0] = ot.T[:n_heads, :].astype(o_ref.dtype)


def _smp_attn(page_table, rb_pad, q3, kn3, vn3, mask_t, cache_k, cache_v, layer):
    Bs, n_pages = page_table.shape
    n_heads = q3.shape[1]
    n_kv = kn3.shape[1]
    page_spec = lambda p: pl.BlockSpec((None, None, PAGE, n_kv, HEAD_DIM),
                                       lambda b, pt: (layer, pt[b, p], 0, 0, 0))
    grid_spec = pltpu.PrefetchScalarGridSpec(
        num_scalar_prefetch=1,
        grid=(Bs,),
        in_specs=[pl.BlockSpec((N_BUCKETS, LANES), lambda b, pt: (0, 0)),
                  pl.BlockSpec((1, n_heads, HEAD_DIM), lambda b, pt: (b, 0, 0)),
                  pl.BlockSpec((1, n_kv, HEAD_DIM), lambda b, pt: (b, 0, 0)),
                  pl.BlockSpec((1, n_kv, HEAD_DIM), lambda b, pt: (b, 0, 0)),
                  pl.BlockSpec((1, PAGE, SCORE_ROWS), lambda b, pt: (b, 0, 0))]
        + [page_spec(p) for p in range(n_pages)] * 2,
        out_specs=pl.BlockSpec((1, n_heads, HEAD_DIM), lambda b, pt: (b, 0, 0)),
        scratch_shapes=[pltpu.VMEM((n_pages, PAGE, LANES), F32), pltpu.VMEM((PAGE, LANES), F32)],
    )
    return pl.pallas_call(
        functools.partial(_smp_attn_body, n_pages=n_pages, n_heads=n_heads, past=n_pages * PAGE),
        grid_spec=grid_spec,
        out_shape=jax.ShapeDtypeStruct((Bs, n_heads, HEAD_DIM), BF16),
        compiler_params=_params("arbitrary"),
    )(page_table, rb_pad, q3, kn3, vn3, mask_t, *([cache_k] * n_pages), *([cache_v] * n_pages))


def _merge_body(yr_ref, ya_ref, wa_ref, wb_ref, ga_ref, gb_ref, o_ref):
    ba = _dot(yr_ref[...], wa_ref[...].astype(BF16))
    bb = _dot(ya_ref[...], wb_ref[...].astype(BF16))
    o_ref[...] = (jax.nn.sigmoid(ga_ref[...]) * ba + jax.nn.sigmoid(gb_ref[...]) * bb).astype(o_ref.dtype)


def _merge(yr, ya, wa, wb, gab, l, *, tm, tn):
    M, K = yr.shape
    D = wa.shape[2]
    nj = D // tn
    return pl.pallas_call(
        _merge_body,
        grid=(nj, M // tm),
        in_specs=[pl.BlockSpec((tm, K), lambda j, i: (i, 0)),
                  pl.BlockSpec((tm, K), lambda j, i: (i, 0)),
                  pl.BlockSpec((None, K, tn), lambda j, i: (l, 0, j)),
                  pl.BlockSpec((None, K, tn), lambda j, i: (l, 0, j)),
                  pl.BlockSpec((tm, tn), lambda j, i: (i, j)),
                  pl.BlockSpec((tm, tn), lambda j, i: (i, nj + j))],
        out_specs=pl.BlockSpec((tm, tn), lambda j, i: (i, j)),
        out_shape=jax.ShapeDtypeStruct((M, D), BF16),
        compiler_params=_params("arbitrary", "arbitrary"),
    )(yr, ya, wa, wb, gab, gab)


def _proj_res_body(a_ref, w_ref, x_ref, g_ref, o_ref):
    o_ref[0] = x_ref[0] + g_ref[0] * _dot(a_ref[...], w_ref[...].astype(BF16))


def _proj_res(a, w, l, x, mod, g_chunk, *, tm, tn):
    B, T, D = x.shape
    K = a.shape[1]
    tpb = T // tm
    nj = D // tn
    per_tok = mod.shape[1] == T
    tg = tm if per_tok else 1
    gmap = (lambda j, i: (i // tpb, i % tpb, g_chunk * nj + j)) if per_tok else \
        (lambda j, i: (i // tpb, 0, g_chunk * nj + j))
    return pl.pallas_call(
        _proj_res_body,
        grid=(nj, B * tpb),
        in_specs=[pl.BlockSpec((tm, K), lambda j, i: (i, 0)),
                  pl.BlockSpec((None, K, tn), lambda j, i: (l, 0, j)),
                  pl.BlockSpec((1, tm, tn), lambda j, i: (i // tpb, i % tpb, j)),
                  pl.BlockSpec((1, tg, tn), gmap)],
        out_specs=pl.BlockSpec((1, tm, tn), lambda j, i: (i // tpb, i % tpb, j)),
        out_shape=jax.ShapeDtypeStruct((B, T, D), F32),
        compiler_params=_params("arbitrary", "arbitrary"),
    )(a, w, x, mod)


def _router_body(h_ref, r_ref, o_ref, *, n_experts):
    lg = _dot(h_ref[...], r_ref[...].astype(BF16))
    lane = lax.broadcasted_iota(I32, lg.shape, 1)
    lg = jnp.where(lane < n_experts, lg, NEG_INF)
    m1 = jnp.max(lg, axis=1, keepdims=True)
    i1 = jnp.min(jnp.where(lg == m1, lane, LANES), axis=1, keepdims=True)
    rest = jnp.where(lane == i1, NEG_INF, lg)
    m2 = jnp.max(rest, axis=1, keepdims=True)
    i2 = jnp.min(jnp.where(rest == m2, lane, LANES), axis=1, keepdims=True)
    e2 = jnp.exp(m2 - m1)
    den = 1.0 + e2
    o_ref[...] = jnp.where(lane == i1, 1.0 / den, 0.0) + jnp.where(lane == i2, e2 / den, 0.0)


def _router(h, router_pad, *, tm, n_experts):
    M, D = h.shape
    return pl.pallas_call(
        functools.partial(_router_body, n_experts=n_experts),
        grid=(M // tm,),
        in_specs=[pl.BlockSpec((tm, D), lambda i: (i, 0)),
                  pl.BlockSpec((D, LANES), lambda i: (0, 0))],
        out_specs=pl.BlockSpec((tm, LANES), lambda i: (i, 0)),
        out_shape=jax.ShapeDtypeStruct((M, LANES), F32),
        compiler_params=_params("arbitrary"),
    )(h, router_pad)


def _ffn_body(h_ref, w1_ref, w3_ref, w2_ref, x_ref, g2_ref, *refs, gated):
    if gated:
        gt_ref, o_ref = refs
    else:
        (o_ref,) = refs
    e = pl.program_id(1)
    f = pl.program_id(2)

    @pl.when((e == 0) & (f == 0))
    def _():
        o_ref[...] = jnp.zeros(o_ref.shape, F32)

    h = h_ref[...]
    a = _dot(h, w1_ref[...].astype(BF16))
    b = _dot(h, w3_ref[...].astype(BF16))
    act = a * jax.nn.sigmoid(a) * b
    if gated:
        gt = gt_ref[...]
        lane = lax.broadcasted_iota(I32, gt.shape, 1)
        act = act * jnp.sum(jnp.where(lane == e, gt, 0.0), axis=1, keepdims=True)
    o_ref[0] += _dot(act.astype(BF16), w2_ref[...].astype(BF16))

    @pl.when((e == pl.num_programs(1) - 1) & (f == pl.num_programs(2) - 1))
    def _():
        o_ref[0] = x_ref[0] + g2_ref[0] * o_ref[0]


def _ffn(h, w1, w3, w2, layer_idx, x, mod, g_chunk, gates, *, tm, tf):
    B, T, D = x.shape
    tpb = T // tm
    gated = gates is not None
    if gated:
        nE = w1.shape[1]
        F = w1.shape[3]
        w13_spec = lambda: pl.BlockSpec((None, None, D, tf), lambda i, e, f: (layer_idx, e, 0, f))
        w2_spec = pl.BlockSpec((None, None, tf, D), lambda i, e, f: (layer_idx, e, f, 0))
    else:
        nE = 1
        F = w1.shape[2]
        w13_spec = lambda: pl.BlockSpec((None, D, tf), lambda i, e, f: (layer_idx, 0, f))
        w2_spec = pl.BlockSpec((None, tf, D), lambda i, e, f: (layer_idx, f, 0))
    per_tok = mod.shape[1] == T
    tg = tm if per_tok else 1
    gmap = (lambda i, e, f: (i // tpb, i % tpb, g_chunk)) if per_tok else (lambda i, e, f: (i // tpb, 0, g_chunk))
    in_specs = [pl.BlockSpec((tm, D), lambda i, e, f: (i, 0)), w13_spec(), w13_spec(), w2_spec,
                pl.BlockSpec((1, tm, D), lambda i, e, f: (i // tpb, i % tpb, 0)),
                pl.BlockSpec((1, tg, D), gmap)]
    args = [h, w1, w3, w2, x, mod]
    if gated:
        in_specs.append(pl.BlockSpec((tm, LANES), lambda i, e, f: (i, 0)))
        args.append(gates)
    return pl.pallas_call(
        functools.partial(_ffn_body, gated=gated),
        grid=(B * tpb, nE, F // tf),
        in_specs=in_specs,
        out_specs=pl.BlockSpec((1, tm, D), lambda i, e, f: (i // tpb, i % tpb, 0)),
        out_shape=jax.ShapeDtypeStruct((B, T, D), F32),
        compiler_params=_params("arbitrary", "arbitrary", "arbitrary"),
    )(*args)


def _tile(n, pref):
    return pref if n % pref == 0 else n


def _group_layer(l, x, mod, P, attend, rglru, *, tt, tm):
    B, T, D = x.shape
    M = B * T
    h = _norm_mod(x, P["norm_g"][l, 0:1], mod, 1, 0, tt=tt).reshape(M, D)
    w_in = P["w_in"]
    (zug,) = _mm(h, w_in, l, 0, 2 * D, [F32], tm=tm, tn=1024)
    (q,) = _mm(h, w_in, l, 2 * D, D, [BF16], tm=tm, tn=1024)
    kv32, kv16 = _mm(h, w_in, l, P["col_kv"], P["n_kv_cols"], [F32, BF16], tm=tm, tn=1024)
    (qi,) = _mm(h, w_in, l, P["col_qi"], P["n_qi_cols"], [BF16], tm=tm, tn=1024)
    (kiwi,) = _mm(h, P["w_kiwi"], l, 0, LANES, [F32], tm=tm, tn=LANES)
    (gab,) = _mm(h, P["w_gab"], l, 0, 2 * D, [F32], tm=tm, tn=1024)
    y_rnn, h_last, new_hist = rglru(l, zug)
    y_att = attend(l, q, kv32, kv16, qi, kiwi)
    merged = _merge(y_rnn.reshape(M, D), y_att.reshape(M, D), P["w_branch_a"], P["w_branch_b"], gab, l,
                    tm=tm, tn=512)
    x = _proj_res(merged, P["w_out"], l, x, mod, 2, tm=tm, tn=1024)
    h2 = _norm_mod(x, P["norm_g"][l, 1:2], mod, 4, 3, tt=tt).reshape(M, D)
    tmf = _tile(T, 512)
    if l % 2 == 0:
        x = _ffn(h2, P["ffn_w1"], P["ffn_w3"], P["ffn_w2"], l // 2, x, mod, 5, None, tm=tmf, tf=256)
    else:
        gates = _router(h2, P["router_pad"][l // 2], tm=tmf, n_experts=P["moe_w1"].shape[1])
        x = _ffn(h2, P["moe_w1"], P["moe_w3"], P["moe_w2"], l // 2, x, mod, 5, gates, tm=tmf, tf=256)
    return x, h_last, new_hist, kv32, kiwi


def kernel(x_prompt, x_sample, c_prompt, c_sample, cache_k, cache_v, cache_idx_k, state_h, state_conv,
           page_table, w_ada, b_ada, norm_g, w_in, conv_w, conv_b, lru_wa, lru_ba, lru_wx, lru_bx,
           lru_lambda, w_branch_a, w_branch_b, w_out, rel_bias, ffn_w1, ffn_w3, ffn_w2,
           moe_router, moe_w1, moe_w3, moe_w2, final_g):
    Bp, S, D = x_prompt.shape
    Bs = x_sample.shape[0]
    depth = w_in.shape[0]
    n_heads = rel_bias.shape[1]
    n_kv = cache_k.shape[3]
    KD = n_kv * HEAD_DIM
    n_idx_heads = (w_in.shape[2] - (5 * D + 2 * KD + IDX_DIM)) // (IDX_DIM + 1)
    n_pages = page_table.shape[1]
    past = n_pages * PAGE
    col_qi = 3 * D + 2 * KD
    col_ki = col_qi + n_idx_heads * IDX_DIM
    n_kiwi = IDX_DIM + n_idx_heads
    n_experts = moe_router.shape[2]

    P = dict(
        col_kv=3 * D, n_kv_cols=2 * KD, col_qi=col_qi, n_qi_cols=n_idx_heads * IDX_DIM,
        norm_g=norm_g, w_in=w_in, conv_w=conv_w, lru_wa=lru_wa, lru_wx=lru_wx,
        conv_b3=conv_b[:, None, :], lru_ba3=lru_ba[:, None, :], lru_bx3=lru_bx[:, None, :],
        lru_lambda3=lru_lambda[:, None, :],
        w_kiwi=jnp.pad(w_in[:, :, col_ki:col_ki + n_kiwi], ((0, 0), (0, 0), (0, LANES - n_kiwi))),
        w_gab=w_in[:, :, col_ki + n_kiwi:],
        w_branch_a=w_branch_a, w_branch_b=w_branch_b, w_out=w_out,
        ffn_w1=ffn_w1, ffn_w3=ffn_w3, ffn_w2=ffn_w2, moe_w1=moe_w1, moe_w3=moe_w3, moe_w2=moe_w2,
        router_pad=jnp.pad(moe_router, ((0, 0), (0, 0), (0, LANES - n_experts))),
    )
    rb_pad = jnp.pad(rel_bias, ((0, 0), (0, LANES - n_heads)))

    n_c = Bp + Bs
    n_c_pad = -(-n_c // 8) * 8
    c_all = jnp.concatenate([c_prompt, c_sample, jnp.zeros((n_c_pad - n_c, D), F32)], axis=0)
    b_ada3 = b_ada[:, None, :]

    def attend_prompt(l, q, kv32, kv16, qi, kiwi):
        ki = kiwi[:, :IDX_DIM].astype(BF16).reshape(Bp, S, IDX_DIM)
        wit = kiwi[:, IDX_DIM:n_kiwi].reshape(Bp, S, n_idx_heads).transpose(0, 2, 1)
        return _attn_prompt(q.reshape(Bp, S, D), kv16.reshape(Bp, S, 2 * KD), qi.reshape(Bp, S, -1), ki, wit,
                            rel_bias)

    def attend_sample(l, q, kv32, kv16, qi, kiwi):
        qi3 = qi.reshape(Bs, n_idx_heads, IDX_DIM)
        wcol = kiwi[:, IDX_DIM:n_kiwi].reshape(Bs, n_idx_heads, 1)
        kin3 = kiwi[:, :IDX_DIM].reshape(Bs, 1, IDX_DIM)
        scores = _smp_scores(page_table, qi3, wcol, kin3, cache_idx_k, l)
        n_sel = min(TOPK_MAX, (past + 1) // 4)
        mask = _smp_select(scores.reshape(Bs, SCORE_ROWS * PAGE), n_sel)
        mask_t = mask.reshape(Bs, SCORE_ROWS, PAGE).transpose(0, 2, 1)
        q3 = q.reshape(Bs, n_heads, HEAD_DIM)
        kn3 = kv32[:, :KD].reshape(Bs, n_kv, HEAD_DIM)
        vn3 = kv32[:, KD:].reshape(Bs, n_kv, HEAD_DIM)
        return _smp_attn(page_table, rb_pad, q3, kn3, vn3, mask_t, cache_k, cache_v, l)

    xp, xs = x_prompt, x_sample.reshape(1, Bs, D)
    outs_p, outs_s = [], []
    for l in range(depth):
        (mod,) = _mm(c_all, w_ada, l, 0, 6 * D, [F32], tm=n_c_pad, tn=1024, bias=b_ada3, silu_in=True)
        mod_p = mod[:Bp].reshape(Bp, 1, 6 * D)
        mod_s = mod[Bp:n_c].reshape(1, Bs, 6 * D)
        hist_t = state_conv[l].transpose(1, 0, 2)

        xp, hp, cp, kvp, kiwip = _group_layer(
            l, xp, mod_p, P, attend_prompt, lambda l_, zug: _rglru_prompt(zug.reshape(Bp, S, 2 * D), P, l_, tc=256),
            tt=512, tm=1024)

        def rglru_s(l_, zug, hist_t=hist_t):
            y, h = _rglru_sample(zug, hist_t, state_h[l_], P, l_)
            new_hist = jnp.concatenate([state_conv[l_][:, 1:], zug[:, None, :D]], axis=1)
            return y, h, new_hist

        xs, hs, cs, kvs, kiwis = _group_layer(l, xs, mod_s, P, attend_sample, rglru_s, tt=Bs, tm=Bs)
        outs_p.append((kvp[:, :KD].reshape(Bp, S, n_kv, HEAD_DIM), kvp[:, KD:].reshape(Bp, S, n_kv, HEAD_DIM),
                       kiwip[:, :IDX_DIM].reshape(Bp, S, IDX_DIM), hp.reshape(Bp, D), cp))
        outs_s.append((kvs[:, :KD].reshape(Bs, 1, n_kv, HEAD_DIM), kvs[:, KD:].reshape(Bs, 1, n_kv, HEAD_DIM),
                       kiwis[:, :IDX_DIM].reshape(Bs, 1, IDX_DIM), hs, cs))
    y_prompt = _final_norm(xp, final_g[None, :], tt=512)
    y_sample = _final_norm(xs, final_g[None, :], tt=Bs).reshape(Bs, 1, D)
    stack = lambda outs, i: jnp.stack([o[i] for o in outs])
    return (y_prompt, y_sample,
            stack(outs_p, 0), stack(outs_p, 1), stack(outs_p, 2), stack(outs_p, 3), stack(outs_p, 4),
            stack(outs_s, 0), stack(outs_s, 1), stack(outs_s, 2), stack(outs_s, 3), stack(outs_s, 4))
```

```python
import functools
import math

import jax
import jax.numpy as jnp
from jax import lax
from jax.experimental import pallas as pl
from jax.experimental.pallas import tpu as pltpu

F32 = jnp.float32
BF16 = jnp.bfloat16
I32 = jnp.int32

VMEM_LIMIT_BYTES = 56 * 1024 * 1024
LANES = 128

EPS = 1e-6
LRU_C = 8.0
RNN_BLOCK = 128
CONV_W = 4
HEAD_DIM = 128
KV_GROUP = 2
IDX_DIM = 64
TOPK_MAX = 256
Q_BLOCK = 128
N_BUCKETS = 32
MAX_DISTANCE = 128
PAGE = 128
NEG_INF = float("-inf")
INT_MIN = -(2 ** 31)


def _params(*sem):
    return pltpu.CompilerParams(dimension_semantics=sem, vmem_limit_bytes=VMEM_LIMIT_BYTES)


def _dot(a, b):
    return jnp.dot(a, b, preferred_element_type=F32)


def _dot_nt(a, b):
    return lax.dot_general(a, b, (((1,), (1,)), ((), ())), preferred_element_type=F32)


def _dot_tn(a, b):
    return lax.dot_general(a, b, (((0,), (0,)), ((), ())), preferred_element_type=F32)


def _mm_body(x_ref, w_ref, *refs, silu_in, has_bias):
    x = x_ref[...]
    if silu_in:
        x = x.astype(F32)
        x = x * jax.nn.sigmoid(x)
    acc = _dot(x.astype(BF16), w_ref[...].astype(BF16))
    outs = refs
    if has_bias:
        acc = acc + refs[0][...]
        outs = refs[1:]
    for o in outs:
        o[...] = acc.astype(o.dtype)


def _mm(x, w, layer, col0, ncols, out_dtypes, *, tm, tn, bias=None, silu_in=False):
    M, K = x.shape
    assert M % tm == 0 and ncols % tn == 0 and col0 % tn == 0
    cb0 = col0 // tn
    in_specs = [pl.BlockSpec((tm, K), lambda j, i: (i, 0)),
                pl.BlockSpec((None, K, tn), lambda j, i: (layer, 0, cb0 + j))]
    args = [x, w]
    if bias is not None:
        in_specs.append(pl.BlockSpec((None, 1, tn), lambda j, i: (layer, 0, cb0 + j)))
        args.append(bias)
    outs = pl.pallas_call(
        functools.partial(_mm_body, silu_in=silu_in, has_bias=bias is not None),
        grid=(ncols // tn, M // tm),
        in_specs=in_specs,
        out_specs=[pl.BlockSpec((tm, tn), lambda j, i: (i, j)) for _ in out_dtypes],
        out_shape=[jax.ShapeDtypeStruct((M, ncols), dt) for dt in out_dtypes],
        compiler_params=_params("arbitrary", "arbitrary"),
        name="mm",
    )(*args)
    return outs


def _rms(x, g):
    return x * lax.rsqrt(jnp.mean(x * x, axis=-1, keepdims=True) + EPS) * g


def _norm_mod_body(x_ref, g_ref, sc_ref, sh_ref, o_ref):
    y = _rms(x_ref[0], g_ref[...])
    o_ref[0] = (y * (1.0 + sc_ref[0]) + sh_ref[0]).astype(o_ref.dtype)


def _norm_mod(x, g, mod, sc_chunk, sh_chunk, out_dtype, *, tt):
    B, T, D = x.shape
    per_tok = mod.shape[1] == T
    tg = tt if per_tok else 1
    mod_spec = lambda chunk: pl.BlockSpec((1, tg, D), lambda b, t: (b, t if per_tok else 0, chunk))
    return pl.pallas_call(
        _norm_mod_body,
        grid=(B, T // tt),
        in_specs=[pl.BlockSpec((1, tt, D), lambda b, t: (b, t, 0)),
                  pl.BlockSpec((1, D), lambda b, t: (0, 0)),
                  mod_spec(sc_chunk), mod_spec(sh_chunk)],
        out_specs=pl.BlockSpec((1, tt, D), lambda b, t: (b, t, 0)),
        out_shape=jax.ShapeDtypeStruct((B, T, D), out_dtype),
        compiler_params=_params("arbitrary", "arbitrary"),
        name="norm_mod",
    )(x, g, mod, mod)


def _final_norm_body(x_ref, g_ref, o_ref):
    o_ref[0] = _rms(x_ref[0], g_ref[...])


def _final_norm(x, g, *, tt):
    B, T, D = x.shape
    return pl.pallas_call(
        _final_norm_body,
        grid=(B, T // tt),
        in_specs=[pl.BlockSpec((1, tt, D), lambda b, t: (b, t, 0)),
                  pl.BlockSpec((1, D), lambda b, t: (0, 0))],
        out_specs=pl.BlockSpec((1, tt, D), lambda b, t: (b, t, 0)),
        out_shape=jax.ShapeDtypeStruct((B, T, D), F32),
        compiler_params=_params("arbitrary", "arbitrary"),
        name="final_norm",
    )(x, g)


def _softplus(x):
    return jnp.maximum(x, 0.0) + jnp.log1p(jnp.exp(-jnp.abs(x)))


def _expm1(x):
    u = jnp.exp(x)
    return jnp.where(u == 1.0, x, jnp.where(u == 0.0, -1.0, (u - 1.0) * x / jnp.log(u)))


def _block_diag_dot(xc, w_ref):
    nb = w_ref.shape[0]
    outs = []
    for n in range(nb):
        xb = xc[:, n * RNN_BLOCK:(n + 1) * RNN_BLOCK].astype(BF16)
        outs.append(_dot(xb, w_ref[n].astype(BF16)))
    return jnp.concatenate(outs, axis=-1)


def _lru_gates(xc, wa_ref, ba, wx_ref, bx, lam):
    r = jax.nn.sigmoid(_block_diag_dot(xc, wa_ref) + ba)
    ig = jax.nn.sigmoid(_block_diag_dot(xc, wx_ref) + bx)
    log_a = (-LRU_C * r) * _softplus(-lam)
    a = jnp.exp(log_a)
    mult = jnp.sqrt(-_expm1(2.0 * log_a))
    return a, mult, ig


HIST = CONV_W - 1
EXT_PAD = 8


def _rglru_prompt_body(u_ref, gt_ref, cw_ref, cb_ref, wa_ref, ba_ref, wx_ref, bx_ref, lam_ref,
                       y_ref, hl_ref, nh_ref, ext_s, a_s, x_s, h_s):
    c = pl.program_id(1)
    tc = u_ref.shape[1]

    @pl.when(c == 0)
    def _():
        ext_s[0:EXT_PAD, :] = jnp.zeros((EXT_PAD, ext_s.shape[1]), F32)
        h_s[...] = jnp.zeros(h_s.shape, F32)

    @pl.when(c > 0)
    def _():
        ext_s[0:EXT_PAD, :] = ext_s[tc:tc + EXT_PAD, :]

    ext_s[EXT_PAD:EXT_PAD + tc, :] = u_ref[0]
    xc = cb_ref[...]
    for j in range(CONV_W):
        off = EXT_PAD - HIST + j
        xc = xc + ext_s[off:off + tc, :] * cw_ref[j:j + 1, :]
    a, mult, ig = _lru_gates(xc, wa_ref, ba_ref[...], wx_ref, bx_ref[...], lam_ref[...])
    row = c * tc + lax.broadcasted_iota(I32, (tc, 1), 0)
    mult = jnp.where(row == 0, 1.0, mult)
    a_s[...] = a
    x_s[...] = mult * ig * xc

    def step(t, h):
        h = a_s[pl.ds(t, 1), :] * h + x_s[pl.ds(t, 1), :]
        x_s[pl.ds(t, 1), :] = h
        return h

    h = lax.fori_loop(0, tc, step, h_s[0:1, :], unroll=8)
    h_s[0:1, :] = h
    y_ref[0] = (x_s[...] * jax.nn.gelu(gt_ref[0])).astype(y_ref.dtype)

    @pl.when(c == pl.num_programs(1) - 1)
    def _():
        hl_ref[0] = h
        nh_ref[0] = ext_s[EXT_PAD + tc - HIST:EXT_PAD + tc, :]


def _rglru_prompt(zug, P, l, *, tc):
    B, T, C2 = zug.shape
    C = C2 // 2
    nb = C // RNN_BLOCK
    vec = lambda: pl.BlockSpec((None, 1, C), lambda b, c: (l, 0, 0))
    blk = lambda: pl.BlockSpec((None, nb, RNN_BLOCK, RNN_BLOCK), lambda b, c: (l, 0, 0, 0))
    return pl.pallas_call(
        _rglru_prompt_body,
        grid=(B, T // tc),
        in_specs=[pl.BlockSpec((1, tc, C), lambda b, c: (b, c, 0)),
                  pl.BlockSpec((1, tc, C), lambda b, c: (b, c, 1)),
                  pl.BlockSpec((None, CONV_W, C), lambda b, c: (l, 0, 0)),
                  vec(), blk(), vec(), blk(), vec(), vec()],
        out_specs=[pl.BlockSpec((1, tc, C), lambda b, c: (b, c, 0)),
                   pl.BlockSpec((1, 1, C), lambda b, c: (b, 0, 0)),
                   pl.BlockSpec((1, HIST, C), lambda b, c: (b, 0, 0))],
        out_shape=[jax.ShapeDtypeStruct((B, T, C), BF16),
                   jax.ShapeDtypeStruct((B, 1, C), F32),
                   jax.ShapeDtypeStruct((B, HIST, C), F32)],
        scratch_shapes=[pltpu.VMEM((tc + EXT_PAD, C), F32), pltpu.VMEM((tc, C), F32),
                        pltpu.VMEM((tc, C), F32), pltpu.VMEM((8, C), F32)],
        compiler_params=_params("arbitrary", "arbitrary"),
        name="rglru_prompt",
    )(zug, zug, P["conv_w"], P["conv_b3"], P["lru_wa"], P["lru_ba3"], P["lru_wx"], P["lru_bx3"],
      P["lru_lambda3"])


def _rglru_sample_body(u_ref, gt_ref, hist_ref, h0_ref, cw_ref, cb_ref, wa_ref, ba_ref, wx_ref, bx_ref,
                       lam_ref, y_ref, h_ref):
    u = u_ref[...]
    xc = cb_ref[...]
    for j in range(HIST):
        xc = xc + hist_ref[j] * cw_ref[j:j + 1, :]
    xc = xc + u * cw_ref[HIST:HIST + 1, :]
    a, mult, ig = _lru_gates(xc, wa_ref, ba_ref[...], wx_ref, bx_ref[...], lam_ref[...])
    h = a * h0_ref[...] + mult * ig * xc
    h_ref[...] = h
    y_ref[...] = (h * jax.nn.gelu(gt_ref[...])).astype(y_ref.dtype)


def _rglru_sample(zug, hist_t, h0, P, l):
    Bs, C2 = zug.shape
    C = C2 // 2
    nb = C // RNN_BLOCK
    vec = lambda: pl.BlockSpec((None, 1, C), lambda i: (l, 0, 0))
    blk = lambda: pl.BlockSpec((None, nb, RNN_BLOCK, RNN_BLOCK), lambda i: (l, 0, 0, 0))
    return pl.pallas_call(
        _rglru_sample_body,
        grid=(1,),
        in_specs=[pl.BlockSpec((Bs, C), lambda i: (0, 0)),
                  pl.BlockSpec((Bs, C), lambda i: (0, 1)),
                  pl.BlockSpec((HIST, Bs, C), lambda i: (0, 0, 0)),
                  pl.BlockSpec((Bs, C), lambda i: (0, 0)),
                  pl.BlockSpec((None, CONV_W, C), lambda i: (l, 0, 0)),
                  vec(), blk(), vec(), blk(), vec(), vec()],
        out_specs=[pl.BlockSpec((Bs, C), lambda i: (0, 0)),
                   pl.BlockSpec((Bs, C), lambda i: (0, 0))],
        out_shape=[jax.ShapeDtypeStruct((Bs, C), BF16), jax.ShapeDtypeStruct((Bs, C), F32)],
        compiler_params=_params("arbitrary"),
        name="rglru_sample",
    )(zug, zug, hist_t, h0, P["conv_w"], P["conv_b3"], P["lru_wa"], P["lru_ba3"], P["lru_wx"],
      P["lru_bx3"], P["lru_lambda3"])


def _rel_bucket(dist):
    n = jnp.maximum(dist, 0)
    max_exact = N_BUCKETS // 2
    nf = jnp.maximum(n, 1).astype(F32)
    large = max_exact + (jnp.log(nf / max_exact) / math.log(MAX_DISTANCE / max_exact)
                         * (N_BUCKETS - max_exact)).astype(I32)
    large = jnp.minimum(large, N_BUCKETS - 1)
    return jnp.where(n < max_exact, n, large)


def _sort_key(x):
    bits = pltpu.bitcast(x, I32)
    return jnp.where(bits < 0, bits ^ jnp.int32(0x7FFFFFFF), bits)


def _kth_largest_key(count_ge, shape, k):
    c = count_ge(jnp.zeros(shape, I32))
    t = jnp.where(c >= k, jnp.int32(0), jnp.int32(INT_MIN))
    for bit in range(30, -1, -1):
        cand = t + jnp.int32(1 << bit)
        c = count_ge(cand)
        t = jnp.where(c >= k, cand, t)
    return t


KB = 2 * Q_BLOCK


def _attn_prompt_body(rb_ref, q_ref, k_ref, v_ref, qi_ref, ki_ref, wit_ref, o_ref,
                      key_s, msk_s, bias_s, qis_s, m_s, l_s, acc_s, *, n_sel, n_heads, n_idx_heads):
    b = pl.program_id(0)
    i = pl.program_id(1)
    QB = Q_BLOCK
    n_groups = n_heads // KV_GROUP
    GQ = KV_GROUP * QB
    npair = (i + 2) // 2
    s_io = lax.broadcasted_iota(I32, (QB, QB), 0)
    t_io = lax.broadcasted_iota(I32, (QB, QB), 1)

    @pl.when((b == 0) & (i == 0))
    def _():
        for d in range(2):
            bucket = _rel_bucket(t_io - s_io + d * QB)
            for h in range(n_heads):
                acc = jnp.zeros((QB, QB), F32)
                for bb in range(N_BUCKETS):
                    acc = jnp.where(bucket == bb, rb_ref[bb, h], acc)
                bias_s[d, h] = acc
        for h in range(n_heads):
            bias_s[2, h] = jnp.full((QB, QB), rb_ref[N_BUCKETS - 1, h], F32)

    def causal_masked(j):
        off = jnp.where(j < i, QB, jnp.where(j == i, 0, -QB))
        return s_io > t_io + off

    for h in range(n_idx_heads):
        qis_s[h * QB:(h + 1) * QB, :] = qi_ref[0, :, h * IDX_DIM:(h + 1) * IDX_DIM]
    wt = wit_ref[0] * (n_idx_heads * IDX_DIM) ** -0.5

    def score_pair(jj, carry):
        ks = pl.multiple_of(jj * KB, KB)
        d = _dot_nt(ki_ref[0, pl.ds(ks, KB), :], qis_s[...])
        st = jnp.zeros((KB, QB), F32)
        for h in range(n_idx_heads):
            st = st + jnp.maximum(d[:, h * QB:(h + 1) * QB], 0.0) * wt[h:h + 1, :]
        for r in range(2):
            blk = jnp.where(causal_masked(2 * jj + r), NEG_INF, st[r * QB:(r + 1) * QB])
            key_s[pl.ds(ks + r * QB, QB), :] = _sort_key(blk)
        return carry

    lax.fori_loop(0, npair, score_pair, 0)

    def count(pred):
        def body(jj, acc):
            blk = key_s[pl.ds(pl.multiple_of(jj * KB, KB), KB), :]
            return acc + jnp.sum(jnp.where(pred(blk), 1.0, 0.0).reshape(KB // 8, 8, QB), axis=0)
        return jnp.sum(lax.fori_loop(0, npair, body, jnp.zeros((8, QB), F32)), axis=0, keepdims=True)

    thr = _kth_largest_key(lambda cand: count(lambda blk: blk >= cand), (1, QB), n_sel)
    need = n_sel - count(lambda blk: blk > thr)
    r_io = lax.broadcasted_iota(I32, (KB, KB), 0)
    c_io = lax.broadcasted_iota(I32, (KB, KB), 1)
    ltri = jnp.where(c_io <= r_io, 1.0, 0.0).astype(BF16)

    def mask_pair(jj, carry):
        ks = pl.multiple_of(jj * KB, KB)
        blk = key_s[pl.ds(ks, KB), :]
        eq = blk == thr
        rank = carry + _dot(ltri, jnp.where(eq, 1.0, 0.0).astype(BF16))
        keep = jnp.where(blk > thr, 1.0, jnp.where(eq & (rank <= need), 1.0, 0.0))
        for r in range(2):
            kr = jnp.where(causal_masked(2 * jj + r), 0.0, keep[r * QB:(r + 1) * QB])
            msk_s[pl.ds(ks + r * QB, QB), :] = jnp.where(kr > 0.0, 0.0, NEG_INF)
        return rank[KB - 1:KB, :]

    lax.fori_loop(0, npair, mask_pair, jnp.zeros((1, QB), F32))

    scale = HEAD_DIM ** -0.5
    m_s[...] = jnp.full(m_s.shape, NEG_INF, F32)
    l_s[...] = jnp.zeros(l_s.shape, F32)
    acc_s[...] = jnp.zeros(acc_s.shape, F32)

    def kv_pair(jj, carry):
        ks = pl.multiple_of(jj * KB, KB)
        mk = msk_s[pl.ds(ks, KB), :]
        mk = jnp.concatenate([mk] * KV_GROUP, axis=1)
        dsel = [jnp.clip(i - 2 * jj - r, 0, 2) for r in range(2)]
        for g in range(n_groups):
            heads = [g * KV_GROUP + r for r in range(KV_GROUP)]
            qp = jnp.concatenate([q_ref[0, :, h * HEAD_DIM:(h + 1) * HEAD_DIM] for h in heads], axis=0)
            kb = k_ref[0, pl.ds(ks, KB), g * HEAD_DIM:(g + 1) * HEAD_DIM]
            bias = jnp.concatenate(
                [jnp.concatenate([bias_s[dsel[r], h] for h in heads], axis=1) for r in range(2)], axis=0)
            lt = _dot_nt(kb, qp) * scale + bias + mk
            m_old = m_s[g]
            m_new = jnp.maximum(m_old, jnp.max(lt, axis=0, keepdims=True))
            m_fin = jnp.where(m_new == NEG_INF, 0.0, m_new)
            alpha = jnp.exp(m_old - m_fin)
            p = jnp.exp(lt - m_fin)
            l_s[g] = alpha * l_s[g] + jnp.sum(p, axis=0, keepdims=True)
            vb = v_ref[0, pl.ds(ks, KB), g * HEAD_DIM:(g + 1) * HEAD_DIM]
            acc_s[g] = alpha * acc_s[g] + _dot_tn(vb, p.astype(BF16))
            m_s[g] = m_new
        return carry

    lax.fori_loop(0, npair, kv_pair, 0)
    for g in range(n_groups):
        ot = acc_s[g] / l_s[g]
        for r in range(KV_GROUP):
            h = g * KV_GROUP + r
            o_ref[0, :, h * HEAD_DIM:(h + 1) * HEAD_DIM] = ot[:, r * QB:(r + 1) * QB].T.astype(o_ref.dtype)


def _attn_prompt(q, kv, qi, ki, wit, rel_bias):
    B, S, HD = q.shape
    n_heads = HD // HEAD_DIM
    n_groups = n_heads // KV_GROUP
    KD = kv.shape[2] // 2
    n_idx_heads = wit.shape[1]
    n_sel = min(TOPK_MAX, S // 4)
    QB = Q_BLOCK
    assert S % KB == 0
    return pl.pallas_call(
        functools.partial(_attn_prompt_body, n_sel=n_sel, n_heads=n_heads, n_idx_heads=n_idx_heads),
        grid=(B, S // QB),
        in_specs=[pl.BlockSpec(memory_space=pltpu.SMEM),
                  pl.BlockSpec((1, QB, HD), lambda b, i: (b, i, 0)),
                  pl.BlockSpec((1, S, KD), lambda b, i: (b, 0, 0)),
                  pl.BlockSpec((1, S, KD), lambda b, i: (b, 0, 1)),
                  pl.BlockSpec((1, QB, qi.shape[2]), lambda b, i: (b, i, 0)),
                  pl.BlockSpec((1, S, IDX_DIM), lambda b, i: (b, 0, 0)),
                  pl.BlockSpec((1, n_idx_heads, QB), lambda b, i: (b, 0, i))],
        out_specs=pl.BlockSpec((1, QB, HD), lambda b, i: (b, i, 0)),
        out_shape=jax.ShapeDtypeStruct((B, S, HD), BF16),
        scratch_shapes=[pltpu.VMEM((S, QB), I32), pltpu.VMEM((S, QB), F32),
                        pltpu.VMEM((3, n_heads, QB, QB), F32),
                        pltpu.VMEM((n_idx_heads * QB, IDX_DIM), BF16),
                        pltpu.VMEM((n_groups, 1, KV_GROUP * QB), F32),
                        pltpu.VMEM((n_groups, 1, KV_GROUP * QB), F32),
                        pltpu.VMEM((n_groups, HEAD_DIM, KV_GROUP * QB), F32)],
        compiler_params=_params("arbitrary", "arbitrary"),
        name="attn_prompt",
    )(rel_bias, q, kv, kv, qi, ki, wit)


SCORE_ROWS = 24


def _smp_scores_body(pt_ref, qi_ref, wcol_ref, kin_ref, *refs, n_pages, n_idx_heads):
    pages = refs[:n_pages]
    out_ref = refs[n_pages]
    q16 = qi_ref[0].astype(BF16)
    wcol = wcol_ref[0] * (n_idx_heads * IDX_DIM) ** -0.5
    rows = []
    for p in range(n_pages):
        d = _dot_nt(q16, pages[p][...].astype(BF16))
        rows.append(jnp.sum(jnp.maximum(d, 0.0) * wcol, axis=0, keepdims=True))
    kin = kin_ref[0].astype(BF16).astype(F32)
    dn = jnp.sum(q16.astype(F32) * kin, axis=1, keepdims=True)
    snew = jnp.sum(jnp.maximum(dn, 0.0) * wcol, axis=0, keepdims=True)
    lane = lax.broadcasted_iota(I32, (1, PAGE), 1)
    rows.append(jnp.where(lane == 0, snew, NEG_INF))
    rows.append(jnp.full((SCORE_ROWS - n_pages - 1, PAGE), NEG_INF, F32))
    out_ref[0] = jnp.concatenate(rows, axis=0)


def _smp_scores(page_table, qi3, wcol, kin3, cache_idx_k, layer):
    Bs, n_pages = page_table.shape
    n_idx_heads = qi3.shape[1]
    page_spec = lambda p: pl.BlockSpec((None, None, PAGE, IDX_DIM), lambda b, pt: (layer, pt[b, p], 0, 0))
    grid_spec = pltpu.PrefetchScalarGridSpec(
        num_scalar_prefetch=1,
        grid=(Bs,),
        in_specs=[pl.BlockSpec((1, n_idx_heads, IDX_DIM), lambda b, pt: (b, 0, 0)),
                  pl.BlockSpec((1, n_idx_heads, 1), lambda b, pt: (b, 0, 0)),
                  pl.BlockSpec((1, 1, IDX_DIM), lambda b, pt: (b, 0, 0))]
        + [page_spec(p) for p in range(n_pages)],
        out_specs=pl.BlockSpec((1, SCORE_ROWS, PAGE), lambda b, pt: (b, 0, 0)),
    )
    return pl.pallas_call(
        functools.partial(_smp_scores_body, n_pages=n_pages, n_idx_heads=n_idx_heads),
        grid_spec=grid_spec,
        out_shape=jax.ShapeDtypeStruct((Bs, SCORE_ROWS, PAGE), F32),
        compiler_params=_params("arbitrary"),
        name="smp_scores",
    )(page_table, qi3, wcol, kin3, *([cache_idx_k] * n_pages))


def _smp_select_body(sc_ref, o_ref, key_s, *, n_sel):
    Bs, W = sc_ref.shape
    key_s[...] = _sort_key(sc_ref[...])

    def count_ge(cand):
        return jnp.sum(jnp.where(key_s[...] >= cand, 1.0, 0.0), axis=1, keepdims=True)

    thr = _kth_largest_key(count_ge, (Bs, 1), n_sel)
    n_gt = jnp.sum(jnp.where(key_s[...] > thr, 1.0, 0.0), axis=1, keepdims=True)
    need = n_sel - n_gt
    r_io = lax.broadcasted_iota(I32, (LANES, LANES), 0)
    c_io = lax.broadcasted_iota(I32, (LANES, LANES), 1)
    utri = jnp.where(r_io <= c_io, 1.0, 0.0).astype(BF16)
    carry = jnp.zeros((Bs, 1), F32)
    for j in range(W // LANES):
        blk = key_s[:, j * LANES:(j + 1) * LANES]
        eq = blk == thr
        rank = carry + _dot(jnp.where(eq, 1.0, 0.0).astype(BF16), utri)
        keep = jnp.where(blk > thr, 1.0, jnp.where(eq & (rank <= need), 1.0, 0.0))
        o_ref[:, j * LANES:(j + 1) * LANES] = keep
        carry = rank[:, LANES - 1:LANES]


def _smp_select(scores, n_sel):
    Bs, W = scores.shape
    return pl.pallas_call(
        functools.partial(_smp_select_body, n_sel=n_sel),
        grid=(1,),
        in_specs=[pl.BlockSpec((Bs, W), lambda i: (0, 0))],
        out_specs=pl.BlockSpec((Bs, W), lambda i: (0, 0)),
        out_shape=jax.ShapeDtypeStruct((Bs, W), F32),
        scratch_shapes=[pltpu.VMEM((Bs, W), I32)],
        compiler_params=_params("arbitrary"),
        name="smp_select",
    )(scores)


def _smp_attn_body(pt_ref, rbt_ref, q_ref, kn_ref, vn_ref, sel_ref, *refs, n_pages, n_heads, n_kv):
    kpages = refs[:n_pages]
    vpages = refs[n_pages:2 * n_pages]
    o_ref = refs[2 * n_pages]
    lg_s, bias_s, expand_s = refs[2 * n_pages + 1:]
    R = PAGE * n_kv
    row = lax.broadcasted_iota(I32, (1, R), 1)
    head = lax.broadcasted_iota(I32, (n_heads, 1), 0)
    own_group = (row % n_kv) == (head // KV_GROUP)

    @pl.when(pl.program_id(0) == 0)
    def _():
        k_io = lax.broadcasted_iota(I32, (PAGE, R), 0)
        r_io = lax.broadcasted_iota(I32, (PAGE, R), 1)
        expand_s[...] = jnp.where(r_io // n_kv == k_io, 1.0, 0.0).astype(BF16)
        bucket = _rel_bucket(PAGE - row // n_kv)
        acc = jnp.zeros((n_heads, R), F32)
        for bb in range(N_BUCKETS):
            acc = jnp.where(bucket == bb, rbt_ref[:, bb:bb + 1], acc)
        bias_s[...] = acc

    q16 = q_ref[0]
    scale = HEAD_DIM ** -0.5
    far_bias = rbt_ref[:, N_BUCKETS - 1:N_BUCKETS]
    sel = sel_ref[0]
    selx = _dot(sel.astype(BF16), expand_s[...])

    m = jnp.full((n_heads, LANES), NEG_INF, F32)
    for p in range(n_pages):
        lt = _dot_nt(q16, kpages[p][...].astype(BF16)) * scale
        lt = lt + (bias_s[...] if p == n_pages - 1 else far_bias)
        lt = jnp.where(own_group, jnp.where(selx[p:p + 1, :] > 0.5, lt, NEG_INF), NEG_INF)
        lg_s[p] = lt
        for c in range(R // LANES):
            m = jnp.maximum(m, lt[:, c * LANES:(c + 1) * LANES])
    m = jnp.max(m, axis=1, keepdims=True)
    qf = q16.astype(F32)
    ln = jnp.sum(qf * kn_ref[0].astype(BF16).astype(F32), axis=1, keepdims=True) * scale + rbt_ref[:, 0:1]
    ln = jnp.where(sel[n_pages:n_pages + 1, 0:1] > 0.5, ln, NEG_INF)
    m = jnp.maximum(m, ln)

    pn = jnp.exp(ln - m)
    l = pn
    acc = pn.astype(BF16).astype(F32) * vn_ref[0].astype(BF16).astype(F32)
    for p in range(n_pages):
        pr = jnp.exp(lg_s[p] - m)
        l = l + jnp.sum(pr, axis=1, keepdims=True)
        acc = acc + _dot(pr.astype(BF16), vpages[p][...].astype(BF16))
    o_ref[0] = (acc / l).astype(o_ref.dtype)


def _smp_attn(page_table, rbt_pad, q3, kn_rep, vn_rep, sel3, cache_k4, cache_v4, layer, n_kv):
    Bs, n_pages = page_table.shape
    n_heads = q3.shape[1]
    R = PAGE * n_kv
    page_spec = lambda p: pl.BlockSpec((None, None, R, HEAD_DIM), lambda b, pt: (layer, pt[b, p], 0, 0))
    head_spec = lambda: pl.BlockSpec((1, n_heads, HEAD_DIM), lambda b, pt: (b, 0, 0))
    grid_spec = pltpu.PrefetchScalarGridSpec(
        num_scalar_prefetch=1,
        grid=(Bs,),
        in_specs=[pl.BlockSpec((n_heads, LANES), lambda b, pt: (0, 0)),
                  head_spec(), head_spec(), head_spec(),
                  pl.BlockSpec((1, SCORE_ROWS, PAGE), lambda b, pt: (b, 0, 0))]
        + [page_spec(p) for p in range(n_pages)] * 2,
        out_specs=head_spec(),
        scratch_shapes=[pltpu.VMEM((n_pages, n_heads, R), F32), pltpu.VMEM((n_heads, R), F32),
                        pltpu.VMEM((PAGE, R), BF16)],
    )
    return pl.pallas_call(
        functools.partial(_smp_attn_body, n_pages=n_pages, n_heads=n_heads, n_kv=n_kv),
        grid_spec=grid_spec,
        out_shape=jax.ShapeDtypeStruct((Bs, n_heads, HEAD_DIM), BF16),
        compiler_params=_params("arbitrary"),
        name="smp_attn",
    )(page_table, rbt_pad, q3, kn_rep, vn_rep, sel3, *([cache_k4] * n_pages), *([cache_v4] * n_pages))


def _merge_body(yr_ref, ya_ref, wa_ref, wb_ref, ga_ref, gb_ref, o_ref):
    ba = _dot(yr_ref[...], wa_ref[...].astype(BF16))
    bb = _dot(ya_ref[...], wb_ref[...].astype(BF16))
    o_ref[...] = (jax.nn.sigmoid(ga_ref[...]) * ba + jax.nn.sigmoid(gb_ref[...]) * bb).astype(o_ref.dtype)


def _merge(yr, ya, wa, wb, gab, l, *, tm, tn):
    M, K = yr.shape
    D = wa.shape[2]
    nj = D // tn
    return pl.pallas_call(
        _merge_body,
        grid=(nj, M // tm),
        in_specs=[pl.BlockSpec((tm, K), lambda j, i: (i, 0)),
                  pl.BlockSpec((tm, K), lambda j, i: (i, 0)),
                  pl.BlockSpec((None, K, tn), lambda j, i: (l, 0, j)),
                  pl.BlockSpec((None, K, tn), lambda j, i: (l, 0, j)),
                  pl.BlockSpec((tm, tn), lambda j, i: (i, j)),
                  pl.BlockSpec((tm, tn), lambda j, i: (i, nj + j))],
        out_specs=pl.BlockSpec((tm, tn), lambda j, i: (i, j)),
        out_shape=jax.ShapeDtypeStruct((M, D), BF16),
        compiler_params=_params("arbitrary", "arbitrary"),
        name="merge",
    )(yr, ya, wa, wb, gab, gab)


def _proj_res_body(a_ref, w_ref, x_ref, g_ref, o_ref):
    o_ref[0] = x_ref[0] + g_ref[0] * _dot(a_ref[...], w_ref[...].astype(BF16))


def _proj_res(a, w, l, x, mod, g_chunk, *, tm, tn):
    B, T, D = x.shape
    K = a.shape[1]
    tpb = T // tm
    nj = D // tn
    per_tok = mod.shape[1] == T
    tg = tm if per_tok else 1
    gmap = (lambda j, i: (i // tpb, i % tpb, g_chunk * nj + j)) if per_tok else \
        (lambda j, i: (i // tpb, 0, g_chunk * nj + j))
    return pl.pallas_call(
        _proj_res_body,
        grid=(nj, B * tpb),
        in_specs=[pl.BlockSpec((tm, K), lambda j, i: (i, 0)),
                  pl.BlockSpec((None, K, tn), lambda j, i: (l, 0, j)),
                  pl.BlockSpec((1, tm, tn), lambda j, i: (i // tpb, i % tpb, j)),
                  pl.BlockSpec((1, tg, tn), gmap)],
        out_specs=pl.BlockSpec((1, tm, tn), lambda j, i: (i // tpb, i % tpb, j)),
        out_shape=jax.ShapeDtypeStruct((B, T, D), F32),
        compiler_params=_params("arbitrary", "arbitrary"),
        name="proj_res",
    )(a, w, x, mod)


def _router_body(h_ref, r_ref, o_ref, *, n_experts):
    lg = _dot(h_ref[...].astype(BF16), r_ref[...].astype(BF16))
    lane = lax.broadcasted_iota(I32, lg.shape, 1)
    lg = jnp.where(lane < n_experts, lg, NEG_INF)
    m1 = jnp.max(lg, axis=1, keepdims=True)
    i1 = jnp.min(jnp.where(lg == m1, lane, LANES), axis=1, keepdims=True)
    rest = jnp.where(lane == i1, NEG_INF, lg)
    m2 = jnp.max(rest, axis=1, keepdims=True)
    i2 = jnp.min(jnp.where(rest == m2, lane, LANES), axis=1, keepdims=True)
    e2 = jnp.exp(m2 - m1)
    den = 1.0 + e2
    o_ref[...] = jnp.where(lane == 0, i1.astype(F32),
                           jnp.where(lane == 1, i2.astype(F32),
                                     jnp.where(lane == 2, 1.0 / den, jnp.where(lane == 3, e2 / den, 0.0))))


def _router(h, router_pad, *, tm, n_experts):
    M, D = h.shape
    return pl.pallas_call(
        functools.partial(_router_body, n_experts=n_experts),
        grid=(M // tm,),
        in_specs=[pl.BlockSpec((tm, D), lambda i: (i, 0)),
                  pl.BlockSpec((D, LANES), lambda i: (0, 0))],
        out_specs=pl.BlockSpec((tm, LANES), lambda i: (i, 0)),
        out_shape=jax.ShapeDtypeStruct((M, LANES), F32),
        compiler_params=_params("arbitrary"),
        name="router",
    )(h, router_pad)


def _ffn_body(h_ref, w1_ref, w3_ref, w2_ref, x_ref, g2_ref, o_ref):
    f = pl.program_id(1)

    @pl.when(f == 0)
    def _():
        o_ref[...] = jnp.zeros(o_ref.shape, F32)

    h = h_ref[...]
    a = _dot(h, w1_ref[...].astype(BF16))
    b = _dot(h, w3_ref[...].astype(BF16))
    act = a * jax.nn.sigmoid(a) * b
    o_ref[0] += _dot(act.astype(BF16), w2_ref[...].astype(BF16))

    @pl.when(f == pl.num_programs(1) - 1)
    def _():
        o_ref[0] = x_ref[0] + g2_ref[0] * o_ref[0]


def _ffn(h, w1, w3, w2, layer_idx, x, mod, g_chunk, *, tm, tf):
    B, T, D = x.shape
    tpb = T // tm
    F = w1.shape[2]
    per_tok = mod.shape[1] == T
    tg = tm if per_tok else 1
    gmap = (lambda i, f: (i // tpb, i % tpb, g_chunk)) if per_tok else (lambda i, f: (i // tpb, 0, g_chunk))
    return pl.pallas_call(
        _ffn_body,
        grid=(B * tpb, F // tf),
        in_specs=[pl.BlockSpec((tm, D), lambda i, f: (i, 0)),
                  pl.BlockSpec((None, D, tf), lambda i, f: (layer_idx, 0, f)),
                  pl.BlockSpec((None, D, tf), lambda i, f: (layer_idx, 0, f)),
                  pl.BlockSpec((None, tf, D), lambda i, f: (layer_idx, f, 0)),
                  pl.BlockSpec((1, tm, D), lambda i, f: (i // tpb, i % tpb, 0)),
                  pl.BlockSpec((1, tg, D), gmap)],
        out_specs=pl.BlockSpec((1, tm, D), lambda i, f: (i // tpb, i % tpb, 0)),
        out_shape=jax.ShapeDtypeStruct((B, T, D), F32),
        compiler_params=_params("arbitrary", "arbitrary"),
        name="ffn_dense",
    )(h, w1, w3, w2, x, mod)


def _moe_body(te_ref, dst_ref, nval_ref, nused_ref, h_hbm, w1_ref, w3_ref, w2_ref, rw_ref, out_hbm,
              xg_s, xb_s, acc_s, sem_in, sem_out, *, n_tok):
    i = pl.program_id(0)
    f = pl.program_id(1)
    tm = xg_s.shape[0]
    base = i * tm

    @pl.when((i == 0) & (f == 0))
    def _():
        xg_s[...] = jnp.zeros(xg_s.shape, F32)

    def row_in(r, tok):
        return pltpu.make_async_copy(h_hbm.at[pl.ds(tok, 1), :], xg_s.at[pl.ds(r, 1), :], sem_in)

    def row_out(r, dst):
        return pltpu.make_async_copy(acc_s.at[pl.ds(r, 1), :], out_hbm.at[pl.ds(dst, 1), :], sem_out)

    @pl.when(i < nused_ref[0])
    def _():
        nv = nval_ref[i]

        @pl.when(f == 0)
        def _():
            def start(r, c):
                dst = dst_ref[base + r]
                row_in(r, dst - jnp.where(dst >= n_tok, n_tok, 0)).start()
                return c

            lax.fori_loop(0, nv, start, 0)

            def wait(r, c):
                row_in(r, 0).wait()
                return c

            lax.fori_loop(0, nv, wait, 0)
            xb_s[...] = xg_s[...].astype(BF16)
            acc_s[...] = jnp.zeros(acc_s.shape, F32)

        x = xb_s[...]
        a = _dot(x, w1_ref[...].astype(BF16))
        b = _dot(x, w3_ref[...].astype(BF16))
        act = a * jax.nn.sigmoid(a) * b
        acc_s[...] += _dot(act.astype(BF16), w2_ref[...].astype(BF16))

        @pl.when(f == pl.num_programs(1) - 1)
        def _():
            acc_s[...] = acc_s[...] * rw_ref[...]

            def start(r, c):
                row_out(r, dst_ref[base + r]).start()
                return c

            lax.fori_loop(0, nv, start, 0)

            def wait(r, c):
                row_out(r, 0).wait()
                return c

            lax.fori_loop(0, nv, wait, 0)


def _moe_sparse(h_all, top_i, top_w, w1, w3, w2, layer_idx, *, tm, tf):
    M, D = h_all.shape
    nE, _, F = w1.shape[1:]
    n_asg = 2 * M
    n_tiles = -(-(n_asg + nE * (tm - 1)) // tm)
    m_pad = n_tiles * tm
    e_flat = top_i.T.reshape(n_asg)
    onehot = (e_flat[:, None] == jnp.arange(nE, dtype=I32)[None, :]).astype(I32)
    cum = jnp.cumsum(onehot, axis=0)
    counts = cum[-1]
    rank = jnp.take_along_axis(cum, e_flat[:, None], axis=1)[:, 0] - 1
    padded = -(-counts // tm) * tm
    ends = jnp.cumsum(padded)
    starts = ends - padded
    pos = starts[e_flat] + rank
    dst = jnp.full((m_pad,), -1, I32).at[pos].set(jnp.arange(n_asg, dtype=I32))
    roww = jnp.zeros((m_pad,), F32).at[pos].set(top_w.T.reshape(n_asg))
    tile_start = jnp.arange(n_tiles, dtype=I32) * tm
    n_used = (ends[-1] // tm).astype(I32)
    tile_e = jnp.minimum(jnp.searchsorted(ends, tile_start, side="right").astype(I32), nE - 1)
    tile_e = jnp.where(tile_start < ends[-1], tile_e, tile_e[jnp.maximum(n_used - 1, 0)])
    nval = jnp.clip(counts[tile_e] - (tile_start - starts[tile_e]), 0, tm).astype(I32)
    nval = jnp.where(tile_start < ends[-1], nval, 0)
    nf = F // tf

    def fidx(i, f, nused):
        return jnp.where(i < nused[0], f, nf - 1)

    grid_spec = pltpu.PrefetchScalarGridSpec(
        num_scalar_prefetch=4,
        grid=(n_tiles, nf),
        in_specs=[pl.BlockSpec(memory_space=pl.ANY),
                  pl.BlockSpec((None, None, D, tf), lambda i, f, te, ds_, nv, nu: (layer_idx, te[i], 0, fidx(i, f, nu))),
                  pl.BlockSpec((None, None, D, tf), lambda i, f, te, ds_, nv, nu: (layer_idx, te[i], 0, fidx(i, f, nu))),
                  pl.BlockSpec((None, None, tf, D), lambda i, f, te, ds_, nv, nu: (layer_idx, te[i], fidx(i, f, nu), 0)),
                  pl.BlockSpec((tm, 1), lambda i, f, te, ds_, nv, nu: (i, 0))],
        out_specs=pl.BlockSpec(memory_space=pl.ANY),
        scratch_shapes=[pltpu.VMEM((tm, D), F32), pltpu.VMEM((tm, D), BF16), pltpu.VMEM((tm, D), F32),
                        pltpu.SemaphoreType.DMA(()), pltpu.SemaphoreType.DMA(())],
    )
    return pl.pallas_call(
        functools.partial(_moe_body, n_tok=M),
        grid_spec=grid_spec,
        out_shape=jax.ShapeDtypeStruct((n_asg, D), F32),
        compiler_params=_params("arbitrary", "arbitrary"),
        name="moe_sparse",
    )(tile_e, dst, nval, n_used.reshape(1), h_all, w1, w3, w2, roww.reshape(m_pad, 1))


def _moe_combine_body(x_ref, g2_ref, a_ref, b_ref, o_ref):
    o_ref[0] = x_ref[0] + g2_ref[0] * (a_ref[...] + b_ref[...])


def _moe_combine(x, mod, g_chunk, y2, row0, n_tok, *, tt):
    B, T, D = x.shape
    per_tok = mod.shape[1] == T
    tg = tt if per_tok else 1
    tpb = T // tt
    assert row0 % tt == 0 and n_tok % tt == 0
    rb0, rb1 = row0 // tt, (n_tok + row0) // tt
    return pl.pallas_call(
        _moe_combine_body,
        grid=(B, tpb),
        in_specs=[pl.BlockSpec((1, tt, D), lambda b, t: (b, t, 0)),
                  pl.BlockSpec((1, tg, D), lambda b, t: (b, t if per_tok else 0, g_chunk)),
                  pl.BlockSpec((tt, D), lambda b, t: (rb0 + b * tpb + t, 0)),
                  pl.BlockSpec((tt, D), lambda b, t: (rb1 + b * tpb + t, 0))],
        out_specs=pl.BlockSpec((1, tt, D), lambda b, t: (b, t, 0)),
        out_shape=jax.ShapeDtypeStruct((B, T, D), F32),
        compiler_params=_params("arbitrary", "arbitrary"),
        name="moe_combine",
    )(x, mod, y2, y2)


def _tile(n, prefs):
    for p in prefs:
        if n % p == 0:
            return p
    return n


def _mixer_half(l, x, mod, P, attend, rglru, *, tt, tm):
    B, T, D = x.shape
    M = B * T
    h = _norm_mod(x, P["norm_g"][l, 0:1], mod, 1, 0, BF16, tt=tt).reshape(M, D)
    w_in = P["w_in"]
    (zug,) = _mm(h, w_in, l, 0, 2 * D, [F32], tm=tm, tn=1024)
    (q,) = _mm(h, w_in, l, 2 * D, D, [BF16], tm=tm, tn=1024)
    kv32, kv16 = _mm(h, w_in, l, P["col_kv"], P["n_kv_cols"], [F32, BF16], tm=tm, tn=1024)
    (qi,) = _mm(h, w_in, l, P["col_qi"], P["n_qi_cols"], [BF16], tm=tm, tn=1024)
    (kiwi,) = _mm(h, P["w_kiwi"], l, 0, LANES, [F32], tm=tm, tn=LANES)
    (gab,) = _mm(h, P["w_gab"], l, 0, 2 * D, [F32], tm=tm, tn=1024)
    y_rnn, h_last, new_hist = rglru(l, zug)
    y_att = attend(l, q, kv32, kv16, qi, kiwi)
    merged = _merge(y_rnn.reshape(M, D), y_att.reshape(M, D), P["w_branch_a"], P["w_branch_b"], gab, l,
                    tm=tm, tn=512)
    x = _proj_res(merged, P["w_out"], l, x, mod, 2, tm=tm, tn=1024)
    return x, h_last, new_hist, kv32, kiwi


def kernel(x_prompt, x_sample, c_prompt, c_sample, cache_k, cache_v, cache_idx_k, state_h, state_conv,
           page_table, w_ada, b_ada, norm_g, w_in, conv_w, conv_b, lru_wa, lru_ba, lru_wx, lru_bx,
           lru_lambda, w_branch_a, w_branch_b, w_out, rel_bias, ffn_w1, ffn_w3, ffn_w2,
           moe_router, moe_w1, moe_w3, moe_w2, final_g):
    Bp, S, D = x_prompt.shape
    Bs = x_sample.shape[0]
    Mp = Bp * S
    depth = w_in.shape[0]
    n_heads = rel_bias.shape[1]
    n_phys, _, n_kv = cache_k.shape[1:4]
    KD = n_kv * HEAD_DIM
    n_idx_heads = (w_in.shape[2] - (5 * D + 2 * KD + IDX_DIM)) // (IDX_DIM + 1)
    n_pages = page_table.shape[1]
    past = n_pages * PAGE
    col_qi = 3 * D + 2 * KD
    col_ki = col_qi + n_idx_heads * IDX_DIM
    n_kiwi = IDX_DIM + n_idx_heads
    n_experts = moe_router.shape[2]

    P = dict(
        col_kv=3 * D, n_kv_cols=2 * KD, col_qi=col_qi, n_qi_cols=n_idx_heads * IDX_DIM,
        norm_g=norm_g, w_in=w_in, conv_w=conv_w, lru_wa=lru_wa, lru_wx=lru_wx,
        conv_b3=conv_b[:, None, :], lru_ba3=lru_ba[:, None, :], lru_bx3=lru_bx[:, None, :],
        lru_lambda3=lru_lambda[:, None, :],
        w_kiwi=jnp.pad(w_in[:, :, col_ki:col_ki + n_kiwi], ((0, 0), (0, 0), (0, LANES - n_kiwi))),
        w_gab=w_in[:, :, col_ki + n_kiwi:],
        w_branch_a=w_branch_a, w_branch_b=w_branch_b, w_out=w_out,
    )
    router_pad = jnp.pad(moe_router, ((0, 0), (0, 0), (0, LANES - n_experts)))
    rbt_pad = jnp.pad(rel_bias.T, ((0, 0), (0, LANES - rel_bias.shape[0])))
    cache_k4 = cache_k.reshape(depth, n_phys, PAGE * n_kv, HEAD_DIM)
    cache_v4 = cache_v.reshape(depth, n_phys, PAGE * n_kv, HEAD_DIM)

    n_c = Bp + Bs
    n_c_pad = -(-n_c // 8) * 8
    c_all = jnp.concatenate([c_prompt, c_sample, jnp.zeros((n_c_pad - n_c, D), F32)], axis=0)
    b_ada3 = b_ada[:, None, :]

    def attend_prompt(l, q, kv32, kv16, qi, kiwi):
        ki = kiwi[:, :IDX_DIM].astype(BF16).reshape(Bp, S, IDX_DIM)
        wit = kiwi[:, IDX_DIM:n_kiwi].reshape(Bp, S, n_idx_heads).transpose(0, 2, 1)
        return _attn_prompt(q.reshape(Bp, S, D), kv16.reshape(Bp, S, 2 * KD), qi.reshape(Bp, S, -1), ki, wit,
                            rel_bias)

    def attend_sample(l, q, kv32, kv16, qi, kiwi):
        qi3 = qi.reshape(Bs, n_idx_heads, IDX_DIM)
        wcol = kiwi[:, IDX_DIM:n_kiwi].reshape(Bs, n_idx_heads, 1)
        kin3 = kiwi[:, :IDX_DIM].reshape(Bs, 1, IDX_DIM)
        scores = _smp_scores(page_table, qi3, wcol, kin3, cache_idx_k, l)
        n_sel = min(TOPK_MAX, (past + 1) // 4)
        sel3 = _smp_select(scores.reshape(Bs, SCORE_ROWS * PAGE), n_sel).reshape(Bs, SCORE_ROWS, PAGE)
        q3 = q.reshape(Bs, n_heads, HEAD_DIM)
        kn_rep = jnp.repeat(kv32[:, :KD].reshape(Bs, n_kv, HEAD_DIM), KV_GROUP, axis=1)
        vn_rep = jnp.repeat(kv32[:, KD:].reshape(Bs, n_kv, HEAD_DIM), KV_GROUP, axis=1)
        return _smp_attn(page_table, rbt_pad, q3, kn_rep, vn_rep, sel3, cache_k4, cache_v4, l, n_kv)

    xp, xs = x_prompt, x_sample.reshape(1, Bs, D)
    outs_p, outs_s = [], []
    for l in range(depth):
        (mod,) = _mm(c_all, w_ada, l, 0, 6 * D, [F32], tm=n_c_pad, tn=1024, bias=b_ada3, silu_in=True)
        mod_p = mod[:Bp].reshape(Bp, 1, 6 * D)
        mod_s = mod[Bp:n_c].reshape(1, Bs, 6 * D)
        hist_t = state_conv[l].transpose(1, 0, 2)

        xp, hp, cp, kvp, kiwip = _mixer_half(
            l, xp, mod_p, P, attend_prompt, lambda l_, zug: _rglru_prompt(zug.reshape(Bp, S, 2 * D), P, l_, tc=256),
            tt=512, tm=1024)

        def rglru_s(l_, zug, hist_t=hist_t):
            y, h = _rglru_sample(zug, hist_t, state_h[l_], P, l_)
            new_hist = jnp.concatenate([state_conv[l_][:, 1:], zug[:, None, :D]], axis=1)
            return y, h, new_hist

        xs, hs, cs, kvs, kiwis = _mixer_half(l, xs, mod_s, P, attend_sample, rglru_s, tt=Bs, tm=Bs)

        g2p, g2s = P["norm_g"][l, 1:2], P["norm_g"][l, 1:2]
        if l % 2 == 0:
            h2p = _norm_mod(xp, g2p, mod_p, 4, 3, BF16, tt=512).reshape(Mp, D)
            h2s = _norm_mod(xs, g2s, mod_s, 4, 3, BF16, tt=Bs).reshape(Bs, D)
            xp = _ffn(h2p, ffn_w1, ffn_w3, ffn_w2, l // 2, xp, mod_p, 5, tm=512, tf=256)
            xs = _ffn(h2s, ffn_w1, ffn_w3, ffn_w2, l // 2, xs, mod_s, 5, tm=Bs, tf=256)
        else:
            h2p = _norm_mod(xp, g2p, mod_p, 4, 3, F32, tt=512).reshape(Mp, D)
            h2s = _norm_mod(xs, g2s, mod_s, 4, 3, F32, tt=Bs).reshape(Bs, D)
            h_all = jnp.concatenate([h2p, h2s], axis=0)
            n_tok = Mp + Bs
            route = _router(h_all, router_pad[l // 2], tm=_tile(n_tok, (1024, 640, 512, 256, 128)),
                            n_experts=n_experts)
            y2 = _moe_sparse(h_all, route[:, 0:2].astype(I32), route[:, 2:4], moe_w1, moe_w3, moe_w2, l // 2,
                             tm=512, tf=256)
            xp = _moe_combine(xp, mod_p, 5, y2, 0, n_tok, tt=Bs)
            xs = _moe_combine(xs, mod_s, 5, y2, Mp, n_tok, tt=Bs)

        outs_p.append((kvp[:, :KD].reshape(Bp, S, n_kv, HEAD_DIM), kvp[:, KD:].reshape(Bp, S, n_kv, HEAD_DIM),
                       kiwip[:, :IDX_DIM].reshape(Bp, S, IDX_DIM), hp.reshape(Bp, D), cp))
        outs_s.append((kvs[:, :KD].reshape(Bs, 1, n_kv, HEAD_DIM), kvs[:, KD:].reshape(Bs, 1, n_kv, HEAD_DIM),
                       kiwis[:, :IDX_DIM].reshape(Bs, 1, IDX_DIM), hs, cs))
    y_prompt = _final_norm(xp, final_g[None, :], tt=512)
    y_sample = _final_norm(xs, final_g[None, :], tt=Bs).reshape(Bs, 1, D)
    stack = lambda outs, i: jnp.stack([o[i] for o in outs])
    return (y_prompt, y_sample,
            stack(outs_p, 0), stack(outs_p, 1), stack(outs_p, 2), stack(outs_p, 3), stack(outs_p, 4),
            stack(outs_s, 0), stack(outs_s, 1), stack(outs_s, 2), stack(outs_s, 3), stack(outs_s, 4))
```

```python
import functools
import math

import jax
import jax.numpy as jnp
from jax import lax
from jax.experimental import pallas as pl
from jax.experimental.pallas import tpu as pltpu

F32 = jnp.float32
BF16 = jnp.bfloat16
I32 = jnp.int32

VMEM_LIMIT_BYTES = 56 * 1024 * 1024
LANES = 128

EPS = 1e-6
LRU_C = 8.0
RNN_BLOCK = 128
CONV_W = 4
HEAD_DIM = 128
KV_GROUP = 2
IDX_DIM = 64
TOPK_MAX = 256
Q_BLOCK = 128
N_BUCKETS = 32
MAX_DISTANCE = 128
PAGE = 128
NEG_INF = float("-inf")
INT_MIN = -(2 ** 31)


def _params(*sem):
    return pltpu.CompilerParams(dimension_semantics=sem, vmem_limit_bytes=VMEM_LIMIT_BYTES)


def _dot(a, b):
    return jnp.dot(a, b, preferred_element_type=F32)


def _dot_nt(a, b):
    return lax.dot_general(a, b, (((1,), (1,)), ((), ())), preferred_element_type=F32)


def _dot_tn(a, b):
    return lax.dot_general(a, b, (((0,), (0,)), ((), ())), preferred_element_type=F32)


def _mm_body(x_ref, w_ref, *refs, silu_in, has_bias):
    x = x_ref[...]
    if silu_in:
        x = x.astype(F32)
        x = x * jax.nn.sigmoid(x)
    acc = _dot(x.astype(BF16), w_ref[...].astype(BF16))
    outs = refs
    if has_bias:
        acc = acc + refs[0][...]
        outs = refs[1:]
    for o in outs:
        o[...] = acc.astype(o.dtype)


def _mm(x, w, layer, col0, ncols, out_dtypes, *, tm, tn, bias=None, silu_in=False):
    M, K = x.shape
    assert M % tm == 0 and ncols % tn == 0 and col0 % tn == 0
    cb0 = col0 // tn
    in_specs = [pl.BlockSpec((tm, K), lambda j, i: (i, 0)),
                pl.BlockSpec((None, K, tn), lambda j, i: (layer, 0, cb0 + j))]
    args = [x, w]
    if bias is not None:
        in_specs.append(pl.BlockSpec((None, 1, tn), lambda j, i: (layer, 0, cb0 + j)))
        args.append(bias)
    outs = pl.pallas_call(
        functools.partial(_mm_body, silu_in=silu_in, has_bias=bias is not None),
        grid=(ncols // tn, M // tm),
        in_specs=in_specs,
        out_specs=[pl.BlockSpec((tm, tn), lambda j, i: (i, j)) for _ in out_dtypes],
        out_shape=[jax.ShapeDtypeStruct((M, ncols), dt) for dt in out_dtypes],
        compiler_params=_params("arbitrary", "arbitrary"),
        name="mm",
    )(*args)
    return outs


def _rms(x, g):
    return x * lax.rsqrt(jnp.mean(x * x, axis=-1, keepdims=True) + EPS) * g


def _norm_mod_body(x_ref, g_ref, sc_ref, sh_ref, o_ref):
    y = _rms(x_ref[0], g_ref[...])
    o_ref[0] = (y * (1.0 + sc_ref[0]) + sh_ref[0]).astype(o_ref.dtype)


def _norm_mod(x, g, mod, sc_chunk, sh_chunk, out_dtype, *, tt):
    B, T, D = x.shape
    per_tok = mod.shape[1] == T
    tg = tt if per_tok else 1
    mod_spec = lambda chunk: pl.BlockSpec((1, tg, D), lambda b, t: (b, t if per_tok else 0, chunk))
    return pl.pallas_call(
        _norm_mod_body,
        grid=(B, T // tt),
        in_specs=[pl.BlockSpec((1, tt, D), lambda b, t: (b, t, 0)),
                  pl.BlockSpec((1, D), lambda b, t: (0, 0)),
                  mod_spec(sc_chunk), mod_spec(sh_chunk)],
        out_specs=pl.BlockSpec((1, tt, D), lambda b, t: (b, t, 0)),
        out_shape=jax.ShapeDtypeStruct((B, T, D), out_dtype),
        compiler_params=_params("arbitrary", "arbitrary"),
        name="norm_mod",
    )(x, g, mod, mod)


def _final_norm_body(x_ref, g_ref, o_ref):
    o_ref[0] = _rms(x_ref[0], g_ref[...])


def _final_norm(x, g, *, tt):
    B, T, D = x.shape
    return pl.pallas_call(
        _final_norm_body,
        grid=(B, T // tt),
        in_specs=[pl.BlockSpec((1, tt, D), lambda b, t: (b, t, 0)),
                  pl.BlockSpec((1, D), lambda b, t: (0, 0))],
        out_specs=pl.BlockSpec((1, tt, D), lambda b, t: (b, t, 0)),
        out_shape=jax.ShapeDtypeStruct((B, T, D), F32),
        compiler_params=_params("arbitrary", "arbitrary"),
        name="final_norm",
    )(x, g)


def _softplus(x):
    return jnp.maximum(x, 0.0) + jnp.log1p(jnp.exp(-jnp.abs(x)))


def _expm1(x):
    u = jnp.exp(x)
    return jnp.where(u == 1.0, x, jnp.where(u == 0.0, -1.0, (u - 1.0) * x / jnp.log(u)))


def _block_diag_dot(xc, w_ref):
    nb = w_ref.shape[0]
    outs = []
    for n in range(nb):
        xb = xc[:, n * RNN_BLOCK:(n + 1) * RNN_BLOCK].astype(BF16)
        outs.append(_dot(xb, w_ref[n].astype(BF16)))
    return jnp.concatenate(outs, axis=-1)


def _lru_gates(xc, wa_ref, ba, wx_ref, bx, lam):
    r = jax.nn.sigmoid(_block_diag_dot(xc, wa_ref) + ba)
    ig = jax.nn.sigmoid(_block_diag_dot(xc, wx_ref) + bx)
    log_a = (-LRU_C * r) * _softplus(-lam)
    a = jnp.exp(log_a)
    mult = jnp.sqrt(-_expm1(2.0 * log_a))
    return a, mult, ig


HIST = CONV_W - 1
EXT_PAD = 8


def _rglru_prompt_body(u_ref, gt_ref, cw_ref, cb_ref, wa_ref, ba_ref, wx_ref, bx_ref, lam_ref,
                       y_ref, hl_ref, nh_ref, ext_s, a_s, x_s, h_s):
    c = pl.program_id(1)
    tc = u_ref.shape[1]

    @pl.when(c == 0)
    def _():
        ext_s[0:EXT_PAD, :] = jnp.zeros((EXT_PAD, ext_s.shape[1]), F32)
        h_s[...] = jnp.zeros(h_s.shape, F32)

    @pl.when(c > 0)
    def _():
        ext_s[0:EXT_PAD, :] = ext_s[tc:tc + EXT_PAD, :]

    ext_s[EXT_PAD:EXT_PAD + tc, :] = u_ref[0]
    xc = cb_ref[...]
    for j in range(CONV_W):
        off = EXT_PAD - HIST + j
        xc = xc + ext_s[off:off + tc, :] * cw_ref[j:j + 1, :]
    a, mult, ig = _lru_gates(xc, wa_ref, ba_ref[...], wx_ref, bx_ref[...], lam_ref[...])
    row = c * tc + lax.broadcasted_iota(I32, (tc, 1), 0)
    mult = jnp.where(row == 0, 1.0, mult)
    a_s[...] = a
    x_s[...] = mult * ig * xc

    def step(t, h):
        h = a_s[pl.ds(t, 1), :] * h + x_s[pl.ds(t, 1), :]
        x_s[pl.ds(t, 1), :] = h
        return h

    h = lax.fori_loop(0, tc, step, h_s[0:1, :], unroll=8)
    h_s[0:1, :] = h
    y_ref[0] = (x_s[...] * jax.nn.gelu(gt_ref[0])).astype(y_ref.dtype)

    @pl.when(c == pl.num_programs(1) - 1)
    def _():
        hl_ref[0] = h
        nh_ref[0] = ext_s[EXT_PAD + tc - HIST:EXT_PAD + tc, :]


def _rglru_prompt(zug, P, l, *, tc):
    B, T, C2 = zug.shape
    C = C2 // 2
    nb = C // RNN_BLOCK
    vec = lambda: pl.BlockSpec((None, 1, C), lambda b, c: (l, 0, 0))
    blk = lambda: pl.BlockSpec((None, nb, RNN_BLOCK, RNN_BLOCK), lambda b, c: (l, 0, 0, 0))
    return pl.pallas_call(
        _rglru_prompt_body,
        grid=(B, T // tc),
        in_specs=[pl.BlockSpec((1, tc, C), lambda b, c: (b, c, 0)),
                  pl.BlockSpec((1, tc, C), lambda b, c: (b, c, 1)),
                  pl.BlockSpec((None, CONV_W, C), lambda b, c: (l, 0, 0)),
                  vec(), blk(), vec(), blk(), vec(), vec()],
        out_specs=[pl.BlockSpec((1, tc, C), lambda b, c: (b, c, 0)),
                   pl.BlockSpec((1, 1, C), lambda b, c: (b, 0, 0)),
                   pl.BlockSpec((1, HIST, C), lambda b, c: (b, 0, 0))],
        out_shape=[jax.ShapeDtypeStruct((B, T, C), BF16),
                   jax.ShapeDtypeStruct((B, 1, C), F32),
                   jax.ShapeDtypeStruct((B, HIST, C), F32)],
        scratch_shapes=[pltpu.VMEM((tc + EXT_PAD, C), F32), pltpu.VMEM((tc, C), F32),
                        pltpu.VMEM((tc, C), F32), pltpu.VMEM((8, C), F32)],
        compiler_params=_params("arbitrary", "arbitrary"),
        name="rglru_prompt",
    )(zug, zug, P["conv_w"], P["conv_b3"], P["lru_wa"], P["lru_ba3"], P["lru_wx"], P["lru_bx3"],
      P["lru_lambda3"])


def _rglru_sample_body(u_ref, gt_ref, hist_ref, h0_ref, cw_ref, cb_ref, wa_ref, ba_ref, wx_ref, bx_ref,
                       lam_ref, y_ref, h_ref):
    u = u_ref[...]
    xc = cb_ref[...]
    for j in range(HIST):
        xc = xc + hist_ref[j] * cw_ref[j:j + 1, :]
    xc = xc + u * cw_ref[HIST:HIST + 1, :]
    a, mult, ig = _lru_gates(xc, wa_ref, ba_ref[...], wx_ref, bx_ref[...], lam_ref[...])
    h = a * h0_ref[...] + mult * ig * xc
    h_ref[...] = h
    y_ref[...] = (h * jax.nn.gelu(gt_ref[...])).astype(y_ref.dtype)


def _rglru_sample(zug, hist_t, h0, P, l):
    Bs, C2 = zug.shape
    C = C2 // 2
    nb = C // RNN_BLOCK
    vec = lambda: pl.BlockSpec((None, 1, C), lambda i: (l, 0, 0))
    blk = lambda: pl.BlockSpec((None, nb, RNN_BLOCK, RNN_BLOCK), lambda i: (l, 0, 0, 0))
    return pl.pallas_call(
        _rglru_sample_body,
        grid=(1,),
        in_specs=[pl.BlockSpec((Bs, C), lambda i: (0, 0)),
                  pl.BlockSpec((Bs, C), lambda i: (0, 1)),
                  pl.BlockSpec((HIST, Bs, C), lambda i: (0, 0, 0)),
                  pl.BlockSpec((Bs, C), lambda i: (0, 0)),
                  pl.BlockSpec((None, CONV_W, C), lambda i: (l, 0, 0)),
                  vec(), blk(), vec(), blk(), vec(), vec()],
        out_specs=[pl.BlockSpec((Bs, C), lambda i: (0, 0)),
                   pl.BlockSpec((Bs, C), lambda i: (0, 0))],
        out_shape=[jax.ShapeDtypeStruct((Bs, C), BF16), jax.ShapeDtypeStruct((Bs, C), F32)],
        compiler_params=_params("arbitrary"),
        name="rglru_sample",
    )(zug, zug, hist_t, h0, P["conv_w"], P["conv_b3"], P["lru_wa"], P["lru_ba3"], P["lru_wx"],
      P["lru_bx3"], P["lru_lambda3"])


def _rel_bucket(dist):
    n = jnp.maximum(dist, 0)
    max_exact = N_BUCKETS // 2
    nf = jnp.maximum(n, 1).astype(F32)
    large = max_exact + (jnp.log(nf / max_exact) / math.log(MAX_DISTANCE / max_exact)
                         * (N_BUCKETS - max_exact)).astype(I32)
    large = jnp.minimum(large, N_BUCKETS - 1)
    return jnp.where(n < max_exact, n, large)


def _sort_key(x):
    bits = pltpu.bitcast(x, I32)
    return jnp.where(bits < 0, bits ^ jnp.int32(0x7FFFFFFF), bits)


def _kth_largest_key(count_ge, shape, k):
    c = count_ge(jnp.zeros(shape, I32))
    t = jnp.where(c >= k, jnp.int32(0), jnp.int32(INT_MIN))
    for bit in range(30, -1, -1):
        cand = t + jnp.int32(1 << bit)
        c = count_ge(cand)
        t = jnp.where(c >= k, cand, t)
    return t


KB = 2 * Q_BLOCK


def _attn_prompt_body(rb_ref, q_ref, k_ref, v_ref, qi_ref, ki_ref, wit_ref, o_ref,
                      key_s, msk_s, bias_s, qis_s, m_s, l_s, al_s, acc_s, lg_s, p_s, *, n_sel, n_heads, n_idx_heads):
    b = pl.program_id(0)
    i = pl.program_id(1)
    QB = Q_BLOCK
    n_groups = n_heads // KV_GROUP
    GQ = KV_GROUP * QB
    npair = (i + 2) // 2
    s_io = lax.broadcasted_iota(I32, (QB, QB), 0)
    t_io = lax.broadcasted_iota(I32, (QB, QB), 1)

    @pl.when((b == 0) & (i == 0))
    def _():
        for d in range(2):
            bucket = _rel_bucket(t_io - s_io + d * QB)
            for h in range(n_heads):
                acc = jnp.zeros((QB, QB), F32)
                for bb in range(N_BUCKETS):
                    acc = jnp.where(bucket == bb, rb_ref[bb, h], acc)
                bias_s[d, h] = acc
        for h in range(n_heads):
            bias_s[2, h] = jnp.full((QB, QB), rb_ref[N_BUCKETS - 1, h], F32)

    def causal_masked(j):
        off = jnp.where(j < i, QB, jnp.where(j == i, 0, -QB))
        return s_io > t_io + off

    for h in range(n_idx_heads):
        qis_s[h * QB:(h + 1) * QB, :] = qi_ref[0, :, h * IDX_DIM:(h + 1) * IDX_DIM]
    wt = wit_ref[0] * (n_idx_heads * IDX_DIM) ** -0.5

    def score_pair(jj, carry):
        ks = pl.multiple_of(jj * KB, KB)
        d = _dot_nt(ki_ref[0, pl.ds(ks, KB), :], qis_s[...])
        st = jnp.zeros((KB, QB), F32)
        for h in range(n_idx_heads):
            st = st + jnp.maximum(d[:, h * QB:(h + 1) * QB], 0.0) * wt[h:h + 1, :]
        for r in range(2):
            blk = jnp.where(causal_masked(2 * jj + r), NEG_INF, st[r * QB:(r + 1) * QB])
            key_s[pl.ds(ks + r * QB, QB), :] = _sort_key(blk)
        return carry

    lax.fori_loop(0, npair, score_pair, 0)

    def count(pred):
        def body(jj, acc):
            blk = key_s[pl.ds(pl.multiple_of(jj * KB, KB), KB), :]
            return acc + jnp.sum(jnp.where(pred(blk), 1.0, 0.0).reshape(KB // 32, 4, 8, QB), axis=0)
        acc = lax.fori_loop(0, npair, body, jnp.zeros((4, 8, QB), F32))
        return jnp.sum(acc.reshape(32, QB), axis=0, keepdims=True)

    thr = _kth_largest_key(lambda cand: count(lambda blk: blk >= cand), (1, QB), n_sel)
    need = n_sel - count(lambda blk: blk > thr)
    r_io = lax.broadcasted_iota(I32, (KB, KB), 0)
    c_io = lax.broadcasted_iota(I32, (KB, KB), 1)
    ltri = jnp.where(c_io <= r_io, 1.0, 0.0).astype(BF16)

    def mask_pair(jj, carry):
        ks = pl.multiple_of(jj * KB, KB)
        blk = key_s[pl.ds(ks, KB), :]
        eq = blk == thr
        rank = carry + _dot(ltri, jnp.where(eq, 1.0, 0.0).astype(BF16))
        keep = jnp.where(blk > thr, 1.0, jnp.where(eq & (rank <= need), 1.0, 0.0))
        for r in range(2):
            kr = jnp.where(causal_masked(2 * jj + r), 0.0, keep[r * QB:(r + 1) * QB])
            msk_s[pl.ds(ks + r * QB, QB), :] = jnp.where(kr > 0.0, 0.0, NEG_INF)
        return rank[KB - 1:KB, :]

    lax.fori_loop(0, npair, mask_pair, jnp.zeros((1, QB), F32))

    scale = HEAD_DIM ** -0.5
    m_s[...] = jnp.full(m_s.shape, NEG_INF, F32)
    l_s[...] = jnp.zeros(l_s.shape, F32)
    acc_s[...] = jnp.zeros(acc_s.shape, F32)

    def kv_pair(jj, carry):
        ks = pl.multiple_of(jj * KB, KB)
        mk = msk_s[pl.ds(ks, KB), :]
        mk = jnp.concatenate([mk] * KV_GROUP, axis=1)
        dsel = [jnp.clip(i - 2 * jj - r, 0, 2) for r in range(2)]
        for g in range(n_groups):
            heads = [g * KV_GROUP + r for r in range(KV_GROUP)]
            qp = jnp.concatenate([q_ref[0, :, h * HEAD_DIM:(h + 1) * HEAD_DIM] for h in heads], axis=0)
            kb = k_ref[0, pl.ds(ks, KB), g * HEAD_DIM:(g + 1) * HEAD_DIM]
            lg_s[g] = _dot_nt(kb, qp)
        for g in range(n_groups):
            heads = [g * KV_GROUP + r for r in range(KV_GROUP)]
            bias = jnp.concatenate(
                [jnp.concatenate([bias_s[dsel[r], h] for h in heads], axis=1) for r in range(2)], axis=0)
            lt = lg_s[g] * scale + bias + mk
            m_old = m_s[g]
            m_new = jnp.maximum(m_old, jnp.max(jnp.max(lt.reshape(4, KB // 4, GQ), axis=0), axis=0, keepdims=True))
            m_fin = jnp.where(m_new == NEG_INF, 0.0, m_new)
            alpha = jnp.exp(m_old - m_fin)
            p = jnp.exp(lt - m_fin)
            l_s[g] = alpha * l_s[g] + jnp.sum(jnp.sum(p.reshape(4, KB // 4, GQ), axis=0), axis=0, keepdims=True)
            al_s[g] = alpha
            p_s[g] = p.astype(BF16)
            m_s[g] = m_new
        for g in range(n_groups):
            vb = v_ref[0, pl.ds(ks, KB), g * HEAD_DIM:(g + 1) * HEAD_DIM]
            acc_s[g] = al_s[g] * acc_s[g] + _dot_tn(vb, p_s[g])
        return carry

    lax.fori_loop(0, npair, kv_pair, 0)
    for g in range(n_groups):
        ot = acc_s[g] / l_s[g]
        for r in range(KV_GROUP):
            h = g * KV_GROUP + r
            o_ref[0, :, h * HEAD_DIM:(h + 1) * HEAD_DIM] = ot[:, r * QB:(r + 1) * QB].T.astype(o_ref.dtype)


def _attn_prompt(q, kv, qi, ki, wit, rel_bias):
    B, S, HD = q.shape
    n_heads = HD // HEAD_DIM
    n_groups = n_heads // KV_GROUP
    KD = kv.shape[2] // 2
    n_idx_heads = wit.shape[1]
    n_sel = min(TOPK_MAX, S // 4)
    QB = Q_BLOCK
    assert S % KB == 0
    return pl.pallas_call(
        functools.partial(_attn_prompt_body, n_sel=n_sel, n_heads=n_heads, n_idx_heads=n_idx_heads),
        grid=(B, S // QB),
        in_specs=[pl.BlockSpec(memory_space=pltpu.SMEM),
                  pl.BlockSpec((1, QB, HD), lambda b, i: (b, i, 0)),
                  pl.BlockSpec((1, S, KD), lambda b, i: (b, 0, 0)),
                  pl.BlockSpec((1, S, KD), lambda b, i: (b, 0, 1)),
                  pl.BlockSpec((1, QB, qi.shape[2]), lambda b, i: (b, i, 0)),
                  pl.BlockSpec((1, S, IDX_DIM), lambda b, i: (b, 0, 0)),
                  pl.BlockSpec((1, n_idx_heads, QB), lambda b, i: (b, 0, i))],
        out_specs=pl.BlockSpec((1, QB, HD), lambda b, i: (b, i, 0)),
        out_shape=jax.ShapeDtypeStruct((B, S, HD), BF16),
        scratch_shapes=[pltpu.VMEM((S, QB), I32), pltpu.VMEM((S, QB), F32),
                        pltpu.VMEM((3, n_heads, QB, QB), F32),
                        pltpu.VMEM((n_idx_heads * QB, IDX_DIM), BF16),
                        pltpu.VMEM((n_groups, 1, KV_GROUP * QB), F32),
                        pltpu.VMEM((n_groups, 1, KV_GROUP * QB), F32),
                        pltpu.VMEM((n_groups, 1, KV_GROUP * QB), F32),
                        pltpu.VMEM((n_groups, HEAD_DIM, KV_GROUP * QB), F32),
                        pltpu.VMEM((n_groups, KB, KV_GROUP * QB), F32),
                        pltpu.VMEM((n_groups, KB, KV_GROUP * QB), BF16)],
        compiler_params=_params("arbitrary", "arbitrary"),
        name="attn_prompt",
    )(rel_bias, q, kv, kv, qi, ki, wit)


SCORE_ROWS = 24


def _smp_scores_body(pt_ref, qi_ref, wcol_ref, kin_ref, *refs, n_pages, n_idx_heads):
    pages = refs[:n_pages]
    out_ref = refs[n_pages]
    q16 = qi_ref[0].astype(BF16)
    wcol = wcol_ref[0] * (n_idx_heads * IDX_DIM) ** -0.5
    rows = []
    for p in range(n_pages):
        d = _dot_nt(q16, pages[p][...].astype(BF16))
        rows.append(jnp.sum(jnp.maximum(d, 0.0) * wcol, axis=0, keepdims=True))
    kin = kin_ref[0].astype(BF16).astype(F32)
    dn = jnp.sum(q16.astype(F32) * kin, axis=1, keepdims=True)
    snew = jnp.sum(jnp.maximum(dn, 0.0) * wcol, axis=0, keepdims=True)
    lane = lax.broadcasted_iota(I32, (1, PAGE), 1)
    rows.append(jnp.where(lane == 0, snew, NEG_INF))
    rows.append(jnp.full((SCORE_ROWS - n_pages - 1, PAGE), NEG_INF, F32))
    out_ref[0] = jnp.concatenate(rows, axis=0)


def _smp_scores(page_table, qi3, wcol, kin3, cache_idx_k, layer):
    Bs, n_pages = page_table.shape
    n_idx_heads = qi3.shape[1]
    page_spec = lambda p: pl.BlockSpec((None, None, PAGE, IDX_DIM), lambda b, pt: (layer, pt[b, p], 0, 0))
    grid_spec = pltpu.PrefetchScalarGridSpec(
        num_scalar_prefetch=1,
        grid=(Bs,),
        in_specs=[pl.BlockSpec((1, n_idx_heads, IDX_DIM), lambda b, pt: (b, 0, 0)),
                  pl.BlockSpec((1, n_idx_heads, 1), lambda b, pt: (b, 0, 0)),
                  pl.BlockSpec((1, 1, IDX_DIM), lambda b, pt: (b, 0, 0))]
        + [page_spec(p) for p in range(n_pages)],
        out_specs=pl.BlockSpec((1, SCORE_ROWS, PAGE), lambda b, pt: (b, 0, 0)),
    )
    return pl.pallas_call(
        functools.partial(_smp_scores_body, n_pages=n_pages, n_idx_heads=n_idx_heads),
        grid_spec=grid_spec,
        out_shape=jax.ShapeDtypeStruct((Bs, SCORE_ROWS, PAGE), F32),
        compiler_params=_params("arbitrary"),
        name="smp_scores",
    )(page_table, qi3, wcol, kin3, *([cache_idx_k] * n_pages))


def _smp_select_body(sc_ref, o_ref, key_s, *, n_sel):
    Bs, W = sc_ref.shape
    key_s[...] = _sort_key(sc_ref[...])

    def count_ge(cand):
        return jnp.sum(jnp.where(key_s[...] >= cand, 1.0, 0.0), axis=1, keepdims=True)

    thr = _kth_largest_key(count_ge, (Bs, 1), n_sel)
    n_gt = jnp.sum(jnp.where(key_s[...] > thr, 1.0, 0.0), axis=1, keepdims=True)
    need = n_sel - n_gt
    r_io = lax.broadcasted_iota(I32, (LANES, LANES), 0)
    c_io = lax.broadcasted_iota(I32, (LANES, LANES), 1)
    utri = jnp.where(r_io <= c_io, 1.0, 0.0).astype(BF16)
    carry = jnp.zeros((Bs, 1), F32)
    for j in range(W // LANES):
        blk = key_s[:, j * LANES:(j + 1) * LANES]
        eq = blk == thr
        rank = carry + _dot(jnp.where(eq, 1.0, 0.0).astype(BF16), utri)
        keep = jnp.where(blk > thr, 1.0, jnp.where(eq & (rank <= need), 1.0, 0.0))
        o_ref[:, j * LANES:(j + 1) * LANES] = keep
        carry = rank[:, LANES - 1:LANES]


def _smp_select(scores, n_sel):
    Bs, W = scores.shape
    return pl.pallas_call(
        functools.partial(_smp_select_body, n_sel=n_sel),
        grid=(1,),
        in_specs=[pl.BlockSpec((Bs, W), lambda i: (0, 0))],
        out_specs=pl.BlockSpec((Bs, W), lambda i: (0, 0)),
        out_shape=jax.ShapeDtypeStruct((Bs, W), F32),
        scratch_shapes=[pltpu.VMEM((Bs, W), I32)],
        compiler_params=_params("arbitrary"),
        name="smp_select",
    )(scores)


def _smp_attn_body(pt_ref, rbt_ref, q_ref, kn_ref, vn_ref, sel_ref, *refs, n_pages, n_heads, n_kv):
    kpages = refs[:n_pages]
    vpages = refs[n_pages:2 * n_pages]
    o_ref = refs[2 * n_pages]
    lg_s, bias_s, expand_s = refs[2 * n_pages + 1:]
    R = PAGE * n_kv
    row = lax.broadcasted_iota(I32, (1, R), 1)
    head = lax.broadcasted_iota(I32, (n_heads, 1), 0)
    own_group = (row % n_kv) == (head // KV_GROUP)

    @pl.when(pl.program_id(0) == 0)
    def _():
        k_io = lax.broadcasted_iota(I32, (PAGE, R), 0)
        r_io = lax.broadcasted_iota(I32, (PAGE, R), 1)
        expand_s[...] = jnp.where(r_io // n_kv == k_io, 1.0, 0.0).astype(BF16)
        bucket = _rel_bucket(PAGE - row // n_kv)
        acc = jnp.zeros((n_heads, R), F32)
        for bb in range(N_BUCKETS):
            acc = jnp.where(bucket == bb, rbt_ref[:, bb:bb + 1], acc)
        bias_s[...] = acc

    q16 = q_ref[0]
    scale = HEAD_DIM ** -0.5
    far_bias = rbt_ref[:, N_BUCKETS - 1:N_BUCKETS]
    sel = sel_ref[0]
    selx = _dot(sel.astype(BF16), expand_s[...])

    m = jnp.full((n_heads, LANES), NEG_INF, F32)
    for p in range(n_pages):
        lt = _dot_nt(q16, kpages[p][...].astype(BF16)) * scale
        lt = lt + (bias_s[...] if p == n_pages - 1 else far_bias)
        lt = jnp.where(own_group, jnp.where(selx[p:p + 1, :] > 0.5, lt, NEG_INF), NEG_INF)
        lg_s[p] = lt
        for c in range(R // LANES):
            m = jnp.maximum(m, lt[:, c * LANES:(c + 1) * LANES])
    m = jnp.max(m, axis=1, keepdims=True)
    qf = q16.astype(F32)
    ln = jnp.sum(qf * kn_ref[0].astype(BF16).astype(F32), axis=1, keepdims=True) * scale + rbt_ref[:, 0:1]
    ln = jnp.where(sel[n_pages:n_pages + 1, 0:1] > 0.5, ln, NEG_INF)
    m = jnp.maximum(m, ln)

    pn = jnp.exp(ln - m)
    l = pn
    acc = pn.astype(BF16).astype(F32) * vn_ref[0].astype(BF16).astype(F32)
    for p in range(n_pages):
        pr = jnp.exp(lg_s[p] - m)
        l = l + jnp.sum(pr, axis=1, keepdims=True)
        acc = acc + _dot(pr.astype(BF16), vpages[p][...].astype(BF16))
    o_ref[0] = (acc / l).astype(o_ref.dtype)


def _smp_attn(page_table, rbt_pad, q3, kn_rep, vn_rep, sel3, cache_k4, cache_v4, layer, n_kv):
    Bs, n_pages = page_table.shape
    n_heads = q3.shape[1]
    R = PAGE * n_kv
    page_spec = lambda p: pl.BlockSpec((None, None, R, HEAD_DIM), lambda b, pt: (layer, pt[b, p], 0, 0))
    head_spec = lambda: pl.BlockSpec((1, n_heads, HEAD_DIM), lambda b, pt: (b, 0, 0))
    grid_spec = pltpu.PrefetchScalarGridSpec(
        num_scalar_prefetch=1,
        grid=(Bs,),
        in_specs=[pl.BlockSpec((n_heads, LANES), lambda b, pt: (0, 0)),
                  head_spec(), head_spec(), head_spec(),
                  pl.BlockSpec((1, SCORE_ROWS, PAGE), lambda b, pt: (b, 0, 0))]
        + [page_spec(p) for p in range(n_pages)] * 2,
        out_specs=head_spec(),
        scratch_shapes=[pltpu.VMEM((n_pages, n_heads, R), F32), pltpu.VMEM((n_heads, R), F32),
                        pltpu.VMEM((PAGE, R), BF16)],
    )
    return pl.pallas_call(
        functools.partial(_smp_attn_body, n_pages=n_pages, n_heads=n_heads, n_kv=n_kv),
        grid_spec=grid_spec,
        out_shape=jax.ShapeDtypeStruct((Bs, n_heads, HEAD_DIM), BF16),
        compiler_params=_params("arbitrary"),
        name="smp_attn",
    )(page_table, rbt_pad, q3, kn_rep, vn_rep, sel3, *([cache_k4] * n_pages), *([cache_v4] * n_pages))


def _merge_body(yr_ref, ya_ref, wa_ref, wb_ref, ga0_ref, ga1_ref, gb0_ref, gb1_ref, o_ref, *, off):
    ga = jnp.concatenate([ga0_ref[:, off:], ga1_ref[:, :off]], axis=1)
    gb = jnp.concatenate([gb0_ref[:, off:], gb1_ref[:, :off]], axis=1)
    ba = _dot(yr_ref[...], wa_ref[...].astype(BF16))
    bb = _dot(ya_ref[...], wb_ref[...].astype(BF16))
    o_ref[...] = (jax.nn.sigmoid(ga) * ba + jax.nn.sigmoid(gb) * bb).astype(o_ref.dtype)


def _merge(yr, ya, wa, wb, ztail, off, l, *, tm, tn):
    M, K = yr.shape
    D = wa.shape[2]
    nj = D // tn
    assert 0 < off < tn and ztail.shape[1] >= (2 * nj + 1) * tn
    gate = lambda blk: pl.BlockSpec((tm, tn), lambda j, i: (i, blk + j))
    return pl.pallas_call(
        functools.partial(_merge_body, off=off),
        grid=(nj, M // tm),
        in_specs=[pl.BlockSpec((tm, K), lambda j, i: (i, 0)),
                  pl.BlockSpec((tm, K), lambda j, i: (i, 0)),
                  pl.BlockSpec((None, K, tn), lambda j, i: (l, 0, j)),
                  pl.BlockSpec((None, K, tn), lambda j, i: (l, 0, j)),
                  gate(0), gate(1), gate(nj), gate(nj + 1)],
        out_specs=pl.BlockSpec((tm, tn), lambda j, i: (i, j)),
        out_shape=jax.ShapeDtypeStruct((M, D), BF16),
        compiler_params=_params("arbitrary", "arbitrary"),
        name="merge",
    )(yr, ya, wa, wb, ztail, ztail, ztail, ztail)


def _proj_res_body(a_ref, w_ref, x_ref, g_ref, o_ref):
    o_ref[0] = x_ref[0] + g_ref[0] * _dot(a_ref[...], w_ref[...].astype(BF16))


def _proj_res(a, w, l, x, mod, g_chunk, *, tm, tn):
    B, T, D = x.shape
    K = a.shape[1]
    tpb = T // tm
    nj = D // tn
    per_tok = mod.shape[1] == T
    tg = tm if per_tok else 1
    gmap = (lambda j, i: (i // tpb, i % tpb, g_chunk * nj + j)) if per_tok else \
        (lambda j, i: (i // tpb, 0, g_chunk * nj + j))
    return pl.pallas_call(
        _proj_res_body,
        grid=(nj, B * tpb),
        in_specs=[pl.BlockSpec((tm, K), lambda j, i: (i, 0)),
                  pl.BlockSpec((None, K, tn), lambda j, i: (l, 0, j)),
                  pl.BlockSpec((1, tm, tn), lambda j, i: (i // tpb, i % tpb, j)),
                  pl.BlockSpec((1, tg, tn), gmap)],
        out_specs=pl.BlockSpec((1, tm, tn), lambda j, i: (i // tpb, i % tpb, j)),
        out_shape=jax.ShapeDtypeStruct((B, T, D), F32),
        compiler_params=_params("arbitrary", "arbitrary"),
        name="proj_res",
    )(a, w, x, mod)


def _router_body(h_ref, r_ref, o_ref, *, n_experts):
    lg = _dot(h_ref[...].astype(BF16), r_ref[...].astype(BF16))
    lane = lax.broadcasted_iota(I32, lg.shape, 1)
    lg = jnp.where(lane < n_experts, lg, NEG_INF)
    m1 = jnp.max(lg, axis=1, keepdims=True)
    i1 = jnp.min(jnp.where(lg == m1, lane, LANES), axis=1, keepdims=True)
    rest = jnp.where(lane == i1, NEG_INF, lg)
    m2 = jnp.max(rest, axis=1, keepdims=True)
    i2 = jnp.min(jnp.where(rest == m2, lane, LANES), axis=1, keepdims=True)
    e2 = jnp.exp(m2 - m1)
    den = 1.0 + e2
    o_ref[...] = jnp.where(lane == 0, i1.astype(F32),
                           jnp.where(lane == 1, i2.astype(F32),
                                     jnp.where(lane == 2, 1.0 / den, jnp.where(lane == 3, e2 / den, 0.0))))


def _router(h, router_pad, *, tm, n_experts):
    M, D = h.shape
    return pl.pallas_call(
        functools.partial(_router_body, n_experts=n_experts),
        grid=(M // tm,),
        in_specs=[pl.BlockSpec((tm, D), lambda i: (i, 0)),
                  pl.BlockSpec((D, LANES), lambda i: (0, 0))],
        out_specs=pl.BlockSpec((tm, LANES), lambda i: (i, 0)),
        out_shape=jax.ShapeDtypeStruct((M, LANES), F32),
        compiler_params=_params("arbitrary"),
        name="router",
    )(h, router_pad)


def _ffn_body(h_ref, w1_ref, w3_ref, w2_ref, x_ref, g2_ref, o_ref):
    f = pl.program_id(1)

    @pl.when(f == 0)
    def _():
        o_ref[...] = jnp.zeros(o_ref.shape, F32)

    h = h_ref[...]
    a = _dot(h, w1_ref[...].astype(BF16))
    b = _dot(h, w3_ref[...].astype(BF16))
    act = a * jax.nn.sigmoid(a) * b
    o_ref[0] += _dot(act.astype(BF16), w2_ref[...].astype(BF16))

    @pl.when(f == pl.num_programs(1) - 1)
    def _():
        o_ref[0] = x_ref[0] + g2_ref[0] * o_ref[0]


def _ffn(h, w1, w3, w2, layer_idx, x, mod, g_chunk, *, tm, tf):
    B, T, D = x.shape
    tpb = T // tm
    F = w1.shape[2]
    per_tok = mod.shape[1] == T
    tg = tm if per_tok else 1
    gmap = (lambda i, f: (i // tpb, i % tpb, g_chunk)) if per_tok else (lambda i, f: (i // tpb, 0, g_chunk))
    return pl.pallas_call(
        _ffn_body,
        grid=(B * tpb, F // tf),
        in_specs=[pl.BlockSpec((tm, D), lambda i, f: (i, 0)),
                  pl.BlockSpec((None, D, tf), lambda i, f: (layer_idx, 0, f)),
                  pl.BlockSpec((None, D, tf), lambda i, f: (layer_idx, 0, f)),
                  pl.BlockSpec((None, tf, D), lambda i, f: (layer_idx, f, 0)),
                  pl.BlockSpec((1, tm, D), lambda i, f: (i // tpb, i % tpb, 0)),
                  pl.BlockSpec((1, tg, D), gmap)],
        out_specs=pl.BlockSpec((1, tm, D), lambda i, f: (i // tpb, i % tpb, 0)),
        out_shape=jax.ShapeDtypeStruct((B, T, D), F32),
        compiler_params=_params("arbitrary", "arbitrary"),
        name="ffn_dense",
    )(h, w1, w3, w2, x, mod)


def _moe_body(te_ref, dst_ref, nval_ref, nused_ref, h_hbm, w1_ref, w3_ref, w2_ref, rw_ref, out_hbm,
              xg_s, xb_s, acc_s, sem_in, sem_out, *, n_tok):
    i = pl.program_id(0)
    f = pl.program_id(1)
    tm = xg_s.shape[0]
    base = i * tm

    def row_in(r, tok):
        return pltpu.make_async_copy(h_hbm.at[pl.ds(tok, 1), :], xg_s.at[pl.ds(r, 1), :], sem_in)

    def row_out(r, dst):
        return pltpu.make_async_copy(acc_s.at[pl.ds(r, 1), :], out_hbm.at[pl.ds(dst, 1), :], sem_out)

    @pl.when(i < nused_ref[0])
    def _():
        nv = nval_ref[i]

        @pl.when(f == 0)
        def _():
            def start(r, c):
                dst = dst_ref[base + r]
                row_in(r, jnp.maximum(dst - jnp.where(dst >= n_tok, n_tok, 0), 0)).start()
                return c

            lax.fori_loop(0, tm, start, 0, unroll=8)
            pltpu.make_async_copy(h_hbm.at[pl.ds(0, tm), :], xg_s, sem_in).wait()
            xb_s[...] = xg_s[...].astype(BF16)
            acc_s[...] = jnp.zeros(acc_s.shape, F32)

        x = xb_s[...]
        a = _dot(x, w1_ref[...].astype(BF16))
        b = _dot(x, w3_ref[...].astype(BF16))
        act = a * jax.nn.sigmoid(a) * b
        acc_s[...] += _dot(act.astype(BF16), w2_ref[...].astype(BF16))

        @pl.when(f == pl.num_programs(1) - 1)
        def _():
            acc_s[...] = acc_s[...] * rw_ref[...]

            def start(r, c):
                row_out(r, dst_ref[base + r]).start()
                return c

            lax.fori_loop(0, nv, start, 0)

            def wait(r, c):
                row_out(r, 0).wait()
                return c

            lax.fori_loop(0, nv, wait, 0)


def _moe_sparse(h_all, top_i, top_w, w1, w3, w2, layer_idx, *, tm, tf):
    M, D = h_all.shape
    nE, _, F = w1.shape[1:]
    n_asg = 2 * M
    n_tiles = -(-(n_asg + nE * (tm - 1)) // tm)
    m_pad = n_tiles * tm
    e_flat = top_i.T.reshape(n_asg)
    onehot = (e_flat[:, None] == jnp.arange(nE, dtype=I32)[None, :]).astype(I32)
    cum = jnp.cumsum(onehot, axis=0)
    counts = cum[-1]
    rank = jnp.take_along_axis(cum, e_flat[:, None], axis=1)[:, 0] - 1
    padded = -(-counts // tm) * tm
    ends = jnp.cumsum(padded)
    starts = ends - padded
    pos = starts[e_flat] + rank
    dst = jnp.full((m_pad,), -1, I32).at[pos].set(jnp.arange(n_asg, dtype=I32))
    roww = jnp.zeros((m_pad,), F32).at[pos].set(top_w.T.reshape(n_asg))
    tile_start = jnp.arange(n_tiles, dtype=I32) * tm
    n_used = (ends[-1] // tm).astype(I32)
    tile_e = jnp.minimum(jnp.searchsorted(ends, tile_start, side="right").astype(I32), nE - 1)
    tile_e = jnp.where(tile_start < ends[-1], tile_e, tile_e[jnp.maximum(n_used - 1, 0)])
    nval = jnp.clip(counts[tile_e] - (tile_start - starts[tile_e]), 0, tm).astype(I32)
    nval = jnp.where(tile_start < ends[-1], nval, 0)
    nf = F // tf

    def fidx(i, f, nused):
        return jnp.where(i < nused[0], f, nf - 1)

    grid_spec = pltpu.PrefetchScalarGridSpec(
        num_scalar_prefetch=4,
        grid=(n_tiles, nf),
        in_specs=[pl.BlockSpec(memory_space=pl.ANY),
                  pl.BlockSpec((None, None, D, tf), lambda i, f, te, ds_, nv, nu: (layer_idx, te[i], 0, fidx(i, f, nu))),
                  pl.BlockSpec((None, None, D, tf), lambda i, f, te, ds_, nv, nu: (layer_idx, te[i], 0, fidx(i, f, nu))),
                  pl.BlockSpec((None, None, tf, D), lambda i, f, te, ds_, nv, nu: (layer_idx, te[i], fidx(i, f, nu), 0)),
                  pl.BlockSpec((tm, 1), lambda i, f, te, ds_, nv, nu: (i, 0))],
        out_specs=pl.BlockSpec(memory_space=pl.ANY),
        scratch_shapes=[pltpu.VMEM((tm, D), F32), pltpu.VMEM((tm, D), BF16), pltpu.VMEM((tm, D), F32),
                        pltpu.SemaphoreType.DMA(()), pltpu.SemaphoreType.DMA(())],
    )
    return pl.pallas_call(
        functools.partial(_moe_body, n_tok=M),
        grid_spec=grid_spec,
        out_shape=jax.ShapeDtypeStruct((n_asg, D), F32),
        compiler_params=_params("arbitrary", "arbitrary"),
        name="moe_sparse",
    )(tile_e, dst, nval, n_used.reshape(1), h_all, w1, w3, w2, roww.reshape(m_pad, 1))


def _moe_combine_body(x_ref, g2_ref, a_ref, b_ref, o_ref):
    o_ref[0] = x_ref[0] + g2_ref[0] * (a_ref[...] + b_ref[...])


def _moe_combine(x, mod, g_chunk, y2, row0, n_tok, *, tt):
    B, T, D = x.shape
    per_tok = mod.shape[1] == T
    tg = tt if per_tok else 1
    tpb = T // tt
    assert row0 % tt == 0 and n_tok % tt == 0
    rb0, rb1 = row0 // tt, (n_tok + row0) // tt
    return pl.pallas_call(
        _moe_combine_body,
        grid=(B, tpb),
        in_specs=[pl.BlockSpec((1, tt, D), lambda b, t: (b, t, 0)),
                  pl.BlockSpec((1, tg, D), lambda b, t: (b, t if per_tok else 0, g_chunk)),
                  pl.BlockSpec((tt, D), lambda b, t: (rb0 + b * tpb + t, 0)),
                  pl.BlockSpec((tt, D), lambda b, t: (rb1 + b * tpb + t, 0))],
        out_specs=pl.BlockSpec((1, tt, D), lambda b, t: (b, t, 0)),
        out_shape=jax.ShapeDtypeStruct((B, T, D), F32),
        compiler_params=_params("arbitrary", "arbitrary"),
        name="moe_combine",
    )(x, mod, y2, y2)


def _tile(n, prefs):
    for p in prefs:
        if n % p == 0:
            return p
    return n


def _mixer_half(l, x, mod, P, attend, rglru, *, tt, tm):
    B, T, D = x.shape
    M = B * T
    h = _norm_mod(x, P["norm_g"][l, 0:1], mod, 1, 0, BF16, tt=tt).reshape(M, D)
    w_in = P["w_in"]
    (zug,) = _mm(h, w_in, l, 0, 2 * D, [F32], tm=tm, tn=1024)
    (q,) = _mm(h, w_in, l, 2 * D, D, [BF16], tm=tm, tn=1024)
    kv32, kv16 = _mm(h, w_in, l, P["col_kv"], P["n_kv_cols"], [F32, BF16], tm=tm, tn=1024)
    (qi,) = _mm(h, w_in, l, P["col_qi"], P["n_qi_cols"], [BF16], tm=tm, tn=1024)
    tnt = 512
    n_tail = -(-(P["n_kiwi"] + 2 * D) // tnt) * tnt
    (ztail,) = _mm(h, w_in, l, P["col_ki"], n_tail, [F32], tm=tm, tn=tnt)
    kiwi = ztail[:, :LANES]
    y_rnn, h_last, new_hist = rglru(l, zug)
    y_att = attend(l, q, kv32, kv16, qi, kiwi)
    merged = _merge(y_rnn.reshape(M, D), y_att.reshape(M, D), P["w_branch_a"], P["w_branch_b"], ztail,
                    P["n_kiwi"], l, tm=_tile(M, (512,)), tn=tnt)
    x = _proj_res(merged, P["w_out"], l, x, mod, 2, tm=tm, tn=1024)
    return x, h_last, new_hist, kv32, kiwi


def kernel(x_prompt, x_sample, c_prompt, c_sample, cache_k, cache_v, cache_idx_k, state_h, state_conv,
           page_table, w_ada, b_ada, norm_g, w_in, conv_w, conv_b, lru_wa, lru_ba, lru_wx, lru_bx,
           lru_lambda, w_branch_a, w_branch_b, w_out, rel_bias, ffn_w1, ffn_w3, ffn_w2,
           moe_router, moe_w1, moe_w3, moe_w2, final_g):
    Bp, S, D = x_prompt.shape
    Bs = x_sample.shape[0]
    Mp = Bp * S
    depth = w_in.shape[0]
    n_heads = rel_bias.shape[1]
    n_phys, _, n_kv = cache_k.shape[1:4]
    KD = n_kv * HEAD_DIM
    n_idx_heads = (w_in.shape[2] - (5 * D + 2 * KD + IDX_DIM)) // (IDX_DIM + 1)
    n_pages = page_table.shape[1]
    past = n_pages * PAGE
    col_qi = 3 * D + 2 * KD
    col_ki = col_qi + n_idx_heads * IDX_DIM
    n_kiwi = IDX_DIM + n_idx_heads
    n_experts = moe_router.shape[2]

    P = dict(
        col_kv=3 * D, n_kv_cols=2 * KD, col_qi=col_qi, n_qi_cols=n_idx_heads * IDX_DIM,
        col_ki=col_ki, n_kiwi=n_kiwi, norm_g=norm_g, w_in=w_in, conv_w=conv_w, lru_wa=lru_wa, lru_wx=lru_wx,
        conv_b3=conv_b[:, None, :], lru_ba3=lru_ba[:, None, :], lru_bx3=lru_bx[:, None, :],
        lru_lambda3=lru_lambda[:, None, :],
        w_branch_a=w_branch_a, w_branch_b=w_branch_b, w_out=w_out,
    )
    router_pad = jnp.pad(moe_router, ((0, 0), (0, 0), (0, LANES - n_experts)))
    rbt_pad = jnp.pad(rel_bias.T, ((0, 0), (0, LANES - rel_bias.shape[0])))
    cache_k4 = cache_k.reshape(depth, n_phys, PAGE * n_kv, HEAD_DIM)
    cache_v4 = cache_v.reshape(depth, n_phys, PAGE * n_kv, HEAD_DIM)

    n_c = Bp + Bs
    n_c_pad = -(-n_c // 8) * 8
    c_all = jnp.concatenate([c_prompt, c_sample, jnp.zeros((n_c_pad - n_c, D), F32)], axis=0)
    b_ada3 = b_ada[:, None, :]

    def attend_prompt(l, q, kv32, kv16, qi, kiwi):
        ki = kiwi[:, :IDX_DIM].astype(BF16).reshape(Bp, S, IDX_DIM)
        wit = kiwi[:, IDX_DIM:n_kiwi].reshape(Bp, S, n_idx_heads).transpose(0, 2, 1)
        return _attn_prompt(q.reshape(Bp, S, D), kv16.reshape(Bp, S, 2 * KD), qi.reshape(Bp, S, -1), ki, wit,
                            rel_bias)

    def attend_sample(l, q, kv32, kv16, qi, kiwi):
        qi3 = qi.reshape(Bs, n_idx_heads, IDX_DIM)
        wcol = kiwi[:, IDX_DIM:n_kiwi].reshape(Bs, n_idx_heads, 1)
        kin3 = kiwi[:, :IDX_DIM].reshape(Bs, 1, IDX_DIM)
        scores = _smp_scores(page_table, qi3, wcol, kin3, cache_idx_k, l)
        n_sel = min(TOPK_MAX, (past + 1) // 4)
        sel3 = _smp_select(scores.reshape(Bs, SCORE_ROWS * PAGE), n_sel).reshape(Bs, SCORE_ROWS, PAGE)
        q3 = q.reshape(Bs, n_heads, HEAD_DIM)
        kn_rep = jnp.repeat(kv32[:, :KD].reshape(Bs, n_kv, HEAD_DIM), KV_GROUP, axis=1)
        vn_rep = jnp.repeat(kv32[:, KD:].reshape(Bs, n_kv, HEAD_DIM), KV_GROUP, axis=1)
        return _smp_attn(page_table, rbt_pad, q3, kn_rep, vn_rep, sel3, cache_k4, cache_v4, l, n_kv)

    xp, xs = x_prompt, x_sample.reshape(1, Bs, D)
    outs_p, outs_s = [], []
    for l in range(depth):
        (mod,) = _mm(c_all, w_ada, l, 0, 6 * D, [F32], tm=n_c_pad, tn=1024, bias=b_ada3, silu_in=True)
        mod_p = mod[:Bp].reshape(Bp, 1, 6 * D)
        mod_s = mod[Bp:n_c].reshape(1, Bs, 6 * D)
        hist_t = state_conv[l].transpose(1, 0, 2)

        xp, hp, cp, kvp, kiwip = _mixer_half(
            l, xp, mod_p, P, attend_prompt, lambda l_, zug: _rglru_prompt(zug.reshape(Bp, S, 2 * D), P, l_, tc=256),
            tt=512, tm=1024)

        def rglru_s(l_, zug, hist_t=hist_t):
            y, h = _rglru_sample(zug, hist_t, state_h[l_], P, l_)
            new_hist = jnp.concatenate([state_conv[l_][:, 1:], zug[:, None, :D]], axis=1)
            return y, h, new_hist

        xs, hs, cs, kvs, kiwis = _mixer_half(l, xs, mod_s, P, attend_sample, rglru_s, tt=Bs, tm=Bs)

        g2p, g2s = P["norm_g"][l, 1:2], P["norm_g"][l, 1:2]
        if l % 2 == 0:
            h2p = _norm_mod(xp, g2p, mod_p, 4, 3, BF16, tt=512).reshape(Mp, D)
            h2s = _norm_mod(xs, g2s, mod_s, 4, 3, BF16, tt=Bs).reshape(Bs, D)
            xp = _ffn(h2p, ffn_w1, ffn_w3, ffn_w2, l // 2, xp, mod_p, 5, tm=512, tf=512)
            xs = _ffn(h2s, ffn_w1, ffn_w3, ffn_w2, l // 2, xs, mod_s, 5, tm=Bs, tf=512)
        else:
            h2p = _norm_mod(xp, g2p, mod_p, 4, 3, F32, tt=512).reshape(Mp, D)
            h2s = _norm_mod(xs, g2s, mod_s, 4, 3, F32, tt=Bs).reshape(Bs, D)
            h_all = jnp.concatenate([h2p, h2s], axis=0)
            n_tok = Mp + Bs
            route = _router(h_all, router_pad[l // 2], tm=_tile(n_tok, (1024, 640, 512, 256, 128)),
                            n_experts=n_experts)
            y2 = _moe_sparse(h_all, route[:, 0:2].astype(I32), route[:, 2:4], moe_w1, moe_w3, moe_w2, l // 2,
                             tm=512, tf=512)
            xp = _moe_combine(xp, mod_p, 5, y2, 0, n_tok, tt=Bs)
            xs = _moe_combine(xs, mod_s, 5, y2, Mp, n_tok, tt=Bs)

        outs_p.append((kvp[:, :KD].reshape(Bp, S, n_kv, HEAD_DIM), kvp[:, KD:].reshape(Bp, S, n_kv, HEAD_DIM),
                       kiwip[:, :IDX_DIM].reshape(Bp, S, IDX_DIM), hp.reshape(Bp, D), cp))
        outs_s.append((kvs[:, :KD].reshape(Bs, 1, n_kv, HEAD_DIM), kvs[:, KD:].reshape(Bs, 1, n_kv, HEAD_DIM),
                       kiwis[:, :IDX_DIM].reshape(Bs, 1, IDX_DIM), hs, cs))
    y_prompt = _final_norm(xp, final_g[None, :], tt=512)
    y_sample = _final_norm(xs, final_g[None, :], tt=Bs).reshape(Bs, 1, D)
    stack = lambda outs, i: jnp.stack([o[i] for o in outs])
    return (y_prompt, y_sample,
            stack(outs_p, 0), stack(outs_p, 1), stack(outs_p, 2), stack(outs_p, 3), stack(outs_p, 4),
            stack(outs_s, 0), stack(outs_s, 1), stack(outs_s, 2), stack(outs_s, 3), stack(outs_s, 4))
```

```python
import functools
import math

import jax
import jax.numpy as jnp
from jax import lax
from jax.experimental import pallas as pl
from jax.experimental.pallas import tpu as pltpu

F32 = jnp.float32
BF16 = jnp.bfloat16
I32 = jnp.int32

VMEM_LIMIT_BYTES = 56 * 1024 * 1024
LANES = 128

EPS = 1e-6
LRU_C = 8.0
RNN_BLOCK = 128
CONV_W = 4
HEAD_DIM = 128
KV_GROUP = 2
IDX_DIM = 64
TOPK_MAX = 256
Q_BLOCK = 128
N_BUCKETS = 32
MAX_DISTANCE = 128
PAGE = 128
NEG_INF = float("-inf")
INT_MIN = -(2 ** 31)


def _params(*sem):
    return pltpu.CompilerParams(dimension_semantics=sem, vmem_limit_bytes=VMEM_LIMIT_BYTES)


def _dot(a, b):
    return jnp.dot(a, b, preferred_element_type=F32)


def _dot_nt(a, b):
    return lax.dot_general(a, b, (((1,), (1,)), ((), ())), preferred_element_type=F32)


def _dot_tn(a, b):
    return lax.dot_general(a, b, (((0,), (0,)), ((), ())), preferred_element_type=F32)


def _mm_body(x_ref, w_ref, *refs, silu_in, has_bias):
    x = x_ref[...]
    if silu_in:
        x = x.astype(F32)
        x = x * jax.nn.sigmoid(x)
    acc = _dot(x.astype(BF16), w_ref[...].astype(BF16))
    outs = refs
    if has_bias:
        acc = acc + refs[0][...]
        outs = refs[1:]
    for o in outs:
        o[...] = acc.astype(o.dtype)


def _mm(x, w, layer, col0, ncols, out_dtypes, *, tm, tn, bias=None, silu_in=False):
    M, K = x.shape
    assert M % tm == 0 and ncols % tn == 0 and col0 % tn == 0
    cb0 = col0 // tn
    in_specs = [pl.BlockSpec((tm, K), lambda j, i: (i, 0)),
                pl.BlockSpec((None, K, tn), lambda j, i: (layer, 0, cb0 + j))]
    args = [x, w]
    if bias is not None:
        in_specs.append(pl.BlockSpec((None, 1, tn), lambda j, i: (layer, 0, cb0 + j)))
        args.append(bias)
    outs = pl.pallas_call(
        functools.partial(_mm_body, silu_in=silu_in, has_bias=bias is not None),
        grid=(ncols // tn, M // tm),
        in_specs=in_specs,
        out_specs=[pl.BlockSpec((tm, tn), lambda j, i: (i, j)) for _ in out_dtypes],
        out_shape=[jax.ShapeDtypeStruct((M, ncols), dt) for dt in out_dtypes],
        compiler_params=_params("arbitrary", "arbitrary"),
        name="mm",
    )(*args)
    return outs


def _mm_stacked_body(x_ref, w_ref, *refs):
    o32_ref, o16_ref = refs[-2:]
    acc = _dot(x_ref[...], w_ref[...].astype(BF16))
    o32_ref[...] = acc
    o16_ref[...] = acc.astype(BF16)


def _mm_stacked(x, w, layer, depth, col0, ncols, prev, *, tm, tn):
    M, K = x.shape
    assert M % tm == 0 and ncols % tn == 0 and col0 % tn == 0
    cb0 = col0 // tn
    in_specs = [pl.BlockSpec((tm, K), lambda j, i: (i, 0)),
                pl.BlockSpec((None, K, tn), lambda j, i: (layer, 0, cb0 + j))]
    args = [x, w]
    aliases = {}
    if prev is not None:
        in_specs.append(pl.BlockSpec(memory_space=pl.ANY))
        args.append(prev)
        aliases = {2: 0}
    return pl.pallas_call(
        _mm_stacked_body,
        grid=(ncols // tn, M // tm),
        in_specs=in_specs,
        out_specs=[pl.BlockSpec((None, tm, tn), lambda j, i: (layer, i, j)),
                   pl.BlockSpec((tm, tn), lambda j, i: (i, j))],
        out_shape=[jax.ShapeDtypeStruct((depth, M, ncols), F32), jax.ShapeDtypeStruct((M, ncols), BF16)],
        input_output_aliases=aliases,
        compiler_params=_params("arbitrary", "arbitrary"),
        name="mm_stacked",
    )(*args)


def _rms(x, g):
    return x * lax.rsqrt(jnp.mean(x * x, axis=-1, keepdims=True) + EPS) * g


def _norm_mod_body(x_ref, g_ref, sc_ref, sh_ref, o_ref):
    y = _rms(x_ref[0], g_ref[...])
    o_ref[0] = (y * (1.0 + sc_ref[0]) + sh_ref[0]).astype(o_ref.dtype)


def _norm_mod(x, g, mod, sc_chunk, sh_chunk, out_dtype, *, tt):
    B, T, D = x.shape
    per_tok = mod.shape[1] == T
    tg = tt if per_tok else 1
    mod_spec = lambda chunk: pl.BlockSpec((1, tg, D), lambda b, t: (b, t if per_tok else 0, chunk))
    return pl.pallas_call(
        _norm_mod_body,
        grid=(B, T // tt),
        in_specs=[pl.BlockSpec((1, tt, D), lambda b, t: (b, t, 0)),
                  pl.BlockSpec((1, D), lambda b, t: (0, 0)),
                  mod_spec(sc_chunk), mod_spec(sh_chunk)],
        out_specs=pl.BlockSpec((1, tt, D), lambda b, t: (b, t, 0)),
        out_shape=jax.ShapeDtypeStruct((B, T, D), out_dtype),
        compiler_params=_params("arbitrary", "arbitrary"),
        name="norm_mod",
    )(x, g, mod, mod)


def _norm_mod_rows_body(x_ref, g_ref, sc_ref, sh_ref, *refs):
    o_ref = refs[-1]
    y = _rms(x_ref[0], g_ref[...])
    o_ref[...] = y * (1.0 + sc_ref[0]) + sh_ref[0]


def _norm_mod_rows(x, g, mod, sc_chunk, sh_chunk, n_rows, row0, prev, *, tt):
    B, T, D = x.shape
    per_tok = mod.shape[1] == T
    tg = tt if per_tok else 1
    tpb = T // tt
    assert row0 % tt == 0
    mod_spec = lambda chunk: pl.BlockSpec((1, tg, D), lambda b, t: (b, t if per_tok else 0, chunk))
    in_specs = [pl.BlockSpec((1, tt, D), lambda b, t: (b, t, 0)),
                pl.BlockSpec((1, D), lambda b, t: (0, 0)),
                mod_spec(sc_chunk), mod_spec(sh_chunk)]
    args = [x, g, mod, mod]
    aliases = {}
    if prev is not None:
        in_specs.append(pl.BlockSpec(memory_space=pl.ANY))
        args.append(prev)
        aliases = {4: 0}
    return pl.pallas_call(
        _norm_mod_rows_body,
        grid=(B, tpb),
        in_specs=in_specs,
        out_specs=pl.BlockSpec((tt, D), lambda b, t: (row0 // tt + b * tpb + t, 0)),
        out_shape=jax.ShapeDtypeStruct((n_rows, D), F32),
        input_output_aliases=aliases,
        compiler_params=_params("arbitrary", "arbitrary"),
        name="norm_mod_rows",
    )(*args)


def _final_norm_body(x_ref, g_ref, o_ref):
    o_ref[0] = _rms(x_ref[0], g_ref[...])


def _final_norm(x, g, *, tt):
    B, T, D = x.shape
    return pl.pallas_call(
        _final_norm_body,
        grid=(B, T // tt),
        in_specs=[pl.BlockSpec((1, tt, D), lambda b, t: (b, t, 0)),
                  pl.BlockSpec((1, D), lambda b, t: (0, 0))],
        out_specs=pl.BlockSpec((1, tt, D), lambda b, t: (b, t, 0)),
        out_shape=jax.ShapeDtypeStruct((B, T, D), F32),
        compiler_params=_params("arbitrary", "arbitrary"),
        name="final_norm",
    )(x, g)


def _softplus(x):
    return jnp.maximum(x, 0.0) + jnp.log1p(jnp.exp(-jnp.abs(x)))


def _expm1(x):
    u = jnp.exp(x)
    return jnp.where(u == 1.0, x, jnp.where(u == 0.0, -1.0, (u - 1.0) * x / jnp.log(u)))


def _block_diag_dot(xc, w_ref):
    nb = w_ref.shape[0]
    outs = []
    for n in range(nb):
        xb = xc[:, n * RNN_BLOCK:(n + 1) * RNN_BLOCK].astype(BF16)
        outs.append(_dot(xb, w_ref[n].astype(BF16)))
    return jnp.concatenate(outs, axis=-1)


def _lru_gates(xc, wa_ref, ba, wx_ref, bx, lam):
    r = jax.nn.sigmoid(_block_diag_dot(xc, wa_ref) + ba)
    ig = jax.nn.sigmoid(_block_diag_dot(xc, wx_ref) + bx)
    log_a = (-LRU_C * r) * _softplus(-lam)
    a = jnp.exp(log_a)
    mult = jnp.sqrt(-_expm1(2.0 * log_a))
    return a, mult, ig


HIST = CONV_W - 1
EXT_PAD = 8


def _rglru_prompt_body(u_ref, gt_ref, cw_ref, cb_ref, wa_ref, ba_ref, wx_ref, bx_ref, lam_ref,
                       y_ref, hl_ref, nh_ref, ext_s, a_s, x_s, h_s):
    c = pl.program_id(1)
    tc = u_ref.shape[1]

    @pl.when(c == 0)
    def _():
        ext_s[0:EXT_PAD, :] = jnp.zeros((EXT_PAD, ext_s.shape[1]), F32)
        h_s[...] = jnp.zeros(h_s.shape, F32)

    @pl.when(c > 0)
    def _():
        ext_s[0:EXT_PAD, :] = ext_s[tc:tc + EXT_PAD, :]

    ext_s[EXT_PAD:EXT_PAD + tc, :] = u_ref[0]
    xc = cb_ref[...]
    for j in range(CONV_W):
        off = EXT_PAD - HIST + j
        xc = xc + ext_s[off:off + tc, :] * cw_ref[j:j + 1, :]
    a, mult, ig = _lru_gates(xc, wa_ref, ba_ref[...], wx_ref, bx_ref[...], lam_ref[...])
    row = c * tc + lax.broadcasted_iota(I32, (tc, 1), 0)
    mult = jnp.where(row == 0, 1.0, mult)
    a_s[...] = a
    x_s[...] = mult * ig * xc

    def step(t, h):
        h = a_s[pl.ds(t, 1), :] * h + x_s[pl.ds(t, 1), :]
        x_s[pl.ds(t, 1), :] = h
        return h

    h = lax.fori_loop(0, tc, step, h_s[0:1, :], unroll=8)
    h_s[0:1, :] = h
    y_ref[0] = (x_s[...] * jax.nn.gelu(gt_ref[0])).astype(y_ref.dtype)

    @pl.when(c == pl.num_programs(1) - 1)
    def _():
        hl_ref[0] = h
        nh_ref[0] = ext_s[EXT_PAD + tc - HIST:EXT_PAD + tc, :]


def _rglru_prompt(zug, P, l, *, tc):
    B, T, C2 = zug.shape
    C = C2 // 2
    nb = C // RNN_BLOCK
    vec = lambda: pl.BlockSpec((None, 1, C), lambda b, c: (l, 0, 0))
    blk = lambda: pl.BlockSpec((None, nb, RNN_BLOCK, RNN_BLOCK), lambda b, c: (l, 0, 0, 0))
    return pl.pallas_call(
        _rglru_prompt_body,
        grid=(B, T // tc),
        in_specs=[pl.BlockSpec((1, tc, C), lambda b, c: (b, c, 0)),
                  pl.BlockSpec((1, tc, C), lambda b, c: (b, c, 1)),
                  pl.BlockSpec((None, CONV_W, C), lambda b, c: (l, 0, 0)),
                  vec(), blk(), vec(), blk(), vec(), vec()],
        out_specs=[pl.BlockSpec((1, tc, C), lambda b, c: (b, c, 0)),
                   pl.BlockSpec((1, 1, C), lambda b, c: (b, 0, 0)),
                   pl.BlockSpec((1, HIST, C), lambda b, c: (b, 0, 0))],
        out_shape=[jax.ShapeDtypeStruct((B, T, C), BF16),
                   jax.ShapeDtypeStruct((B, 1, C), F32),
                   jax.ShapeDtypeStruct((B, HIST, C), F32)],
        scratch_shapes=[pltpu.VMEM((tc + EXT_PAD, C), F32), pltpu.VMEM((tc, C), F32),
                        pltpu.VMEM((tc, C), F32), pltpu.VMEM((8, C), F32)],
        compiler_params=_params("arbitrary", "arbitrary"),
        name="rglru_prompt",
    )(zug, zug, P["conv_w"], P["conv_b3"], P["lru_wa"], P["lru_ba3"], P["lru_wx"], P["lru_bx3"],
      P["lru_lambda3"])


def _rglru_sample_body(u_ref, gt_ref, hist_ref, h0_ref, cw_ref, cb_ref, wa_ref, ba_ref, wx_ref, bx_ref,
                       lam_ref, y_ref, h_ref):
    u = u_ref[...]
    xc = cb_ref[...]
    for j in range(HIST):
        xc = xc + hist_ref[j] * cw_ref[j:j + 1, :]
    xc = xc + u * cw_ref[HIST:HIST + 1, :]
    a, mult, ig = _lru_gates(xc, wa_ref, ba_ref[...], wx_ref, bx_ref[...], lam_ref[...])
    h = a * h0_ref[...] + mult * ig * xc
    h_ref[...] = h
    y_ref[...] = (h * jax.nn.gelu(gt_ref[...])).astype(y_ref.dtype)


def _rglru_sample(zug, hist_t, h0, P, l):
    Bs, C2 = zug.shape
    C = C2 // 2
    nb = C // RNN_BLOCK
    vec = lambda: pl.BlockSpec((None, 1, C), lambda i: (l, 0, 0))
    blk = lambda: pl.BlockSpec((None, nb, RNN_BLOCK, RNN_BLOCK), lambda i: (l, 0, 0, 0))
    return pl.pallas_call(
        _rglru_sample_body,
        grid=(1,),
        in_specs=[pl.BlockSpec((Bs, C), lambda i: (0, 0)),
                  pl.BlockSpec((Bs, C), lambda i: (0, 1)),
                  pl.BlockSpec((HIST, Bs, C), lambda i: (0, 0, 0)),
                  pl.BlockSpec((Bs, C), lambda i: (0, 0)),
                  pl.BlockSpec((None, CONV_W, C), lambda i: (l, 0, 0)),
                  vec(), blk(), vec(), blk(), vec(), vec()],
        out_specs=[pl.BlockSpec((Bs, C), lambda i: (0, 0)),
                   pl.BlockSpec((Bs, C), lambda i: (0, 0))],
        out_shape=[jax.ShapeDtypeStruct((Bs, C), BF16), jax.ShapeDtypeStruct((Bs, C), F32)],
        compiler_params=_params("arbitrary"),
        name="rglru_sample",
    )(zug, zug, hist_t, h0, P["conv_w"], P["conv_b3"], P["lru_wa"], P["lru_ba3"], P["lru_wx"],
      P["lru_bx3"], P["lru_lambda3"])


def _rel_bucket(dist):
    n = jnp.maximum(dist, 0)
    max_exact = N_BUCKETS // 2
    nf = jnp.maximum(n, 1).astype(F32)
    large = max_exact + (jnp.log(nf / max_exact) / math.log(MAX_DISTANCE / max_exact)
                         * (N_BUCKETS - max_exact)).astype(I32)
    large = jnp.minimum(large, N_BUCKETS - 1)
    return jnp.where(n < max_exact, n, large)


def _sort_key(x):
    bits = pltpu.bitcast(x, I32)
    return jnp.where(bits < 0, bits ^ jnp.int32(0x7FFFFFFF), bits)


def _kth_largest_key(count_ge, shape, k):
    c = count_ge(jnp.zeros(shape, I32))
    t = jnp.where(c >= k, jnp.int32(0), jnp.int32(INT_MIN))
    for bit in range(30, -1, -1):
        cand = t + jnp.int32(1 << bit)
        c = count_ge(cand)
        t = jnp.where(c >= k, cand, t)
    return t


KB = 2 * Q_BLOCK


def _attn_prompt_body(rb_ref, q_ref, k_ref, v_ref, qi_ref, ki_ref, wit_ref, o_ref,
                      key_s, msk_s, bias_s, qis_s, m_s, l_s, al_s, acc_s, lg_s, p_s, *, n_sel, n_heads, n_idx_heads):
    b = pl.program_id(0)
    i = pl.program_id(1)
    QB = Q_BLOCK
    n_groups = n_heads // KV_GROUP
    GQ = KV_GROUP * QB
    npair = (i + 2) // 2
    s_io = lax.broadcasted_iota(I32, (QB, QB), 0)
    t_io = lax.broadcasted_iota(I32, (QB, QB), 1)

    @pl.when((b == 0) & (i == 0))
    def _():
        for d in range(2):
            bucket = _rel_bucket(t_io - s_io + d * QB)
            for h in range(n_heads):
                acc = jnp.zeros((QB, QB), F32)
                for bb in range(N_BUCKETS):
                    acc = jnp.where(bucket == bb, rb_ref[bb, h], acc)
                bias_s[d, h] = acc
        for h in range(n_heads):
            bias_s[2, h] = jnp.full((QB, QB), rb_ref[N_BUCKETS - 1, h], F32)

    def causal_masked(j):
        off = jnp.where(j < i, QB, jnp.where(j == i, 0, -QB))
        return s_io > t_io + off

    for h in range(n_idx_heads):
        qis_s[h * QB:(h + 1) * QB, :] = qi_ref[0, :, h * IDX_DIM:(h + 1) * IDX_DIM]
    wt = wit_ref[0] * (n_idx_heads * IDX_DIM) ** -0.5

    def score_pair(jj, carry):
        ks = pl.multiple_of(jj * KB, KB)
        d = _dot_nt(ki_ref[0, pl.ds(ks, KB), :], qis_s[...])
        st = jnp.zeros((KB, QB), F32)
        for h in range(n_idx_heads):
            st = st + jnp.maximum(d[:, h * QB:(h + 1) * QB], 0.0) * wt[h:h + 1, :]
        for r in range(2):
            blk = jnp.where(causal_masked(2 * jj + r), NEG_INF, st[r * QB:(r + 1) * QB])
            key_s[pl.ds(ks + r * QB, QB), :] = _sort_key(blk)
        return carry

    lax.fori_loop(0, npair, score_pair, 0)

    def count(pred):
        def body(jj, acc):
            blk = key_s[pl.ds(pl.multiple_of(jj * KB, KB), KB), :]
            return acc + jnp.sum(jnp.where(pred(blk), 1.0, 0.0).reshape(KB // 32, 4, 8, QB), axis=0)
        acc = lax.fori_loop(0, npair, body, jnp.zeros((4, 8, QB), F32))
        return jnp.sum(acc.reshape(32, QB), axis=0, keepdims=True)

    thr = _kth_largest_key(lambda cand: count(lambda blk: blk >= cand), (1, QB), n_sel)
    need = n_sel - count(lambda blk: blk > thr)
    r_io = lax.broadcasted_iota(I32, (KB, KB), 0)
    c_io = lax.broadcasted_iota(I32, (KB, KB), 1)
    ltri = jnp.where(c_io <= r_io, 1.0, 0.0).astype(BF16)

    def mask_pair(jj, carry):
        ks = pl.multiple_of(jj * KB, KB)
        blk = key_s[pl.ds(ks, KB), :]
        eq = blk == thr
        rank = carry + _dot(ltri, jnp.where(eq, 1.0, 0.0).astype(BF16))
        keep = jnp.where(blk > thr, 1.0, jnp.where(eq & (rank <= need), 1.0, 0.0))
        for r in range(2):
            kr = jnp.where(causal_masked(2 * jj + r), 0.0, keep[r * QB:(r + 1) * QB])
            msk_s[pl.ds(ks + r * QB, QB), :] = jnp.where(kr > 0.0, 0.0, NEG_INF)
        return rank[KB - 1:KB, :]

    lax.fori_loop(0, npair, mask_pair, jnp.zeros((1, QB), F32))

    scale = HEAD_DIM ** -0.5
    m_s[...] = jnp.full(m_s.shape, NEG_INF, F32)
    l_s[...] = jnp.zeros(l_s.shape, F32)
    acc_s[...] = jnp.zeros(acc_s.shape, F32)

    def kv_pair(jj, carry):
        ks = pl.multiple_of(jj * KB, KB)
        mk = msk_s[pl.ds(ks, KB), :]
        mk = jnp.concatenate([mk] * KV_GROUP, axis=1)
        dsel = [jnp.clip(i - 2 * jj - r, 0, 2) for r in range(2)]
        for g in range(n_groups):
            heads = [g * KV_GROUP + r for r in range(KV_GROUP)]
            qp = jnp.concatenate([q_ref[0, :, h * HEAD_DIM:(h + 1) * HEAD_DIM] for h in heads], axis=0)
            kb = k_ref[0, pl.ds(ks, KB), g * HEAD_DIM:(g + 1) * HEAD_DIM]
            lg_s[g] = _dot_nt(kb, qp)
        for g in range(n_groups):
            heads = [g * KV_GROUP + r for r in range(KV_GROUP)]
            bias = jnp.concatenate(
                [jnp.concatenate([bias_s[dsel[r], h] for h in heads], axis=1) for r in range(2)], axis=0)
            lt = lg_s[g] * scale + bias + mk
            m_old = m_s[g]
            m_new = jnp.maximum(m_old, jnp.max(jnp.max(lt.reshape(4, KB // 4, GQ), axis=0), axis=0, keepdims=True))
            m_fin = jnp.where(m_new == NEG_INF, 0.0, m_new)
            alpha = jnp.exp(m_old - m_fin)
            p = jnp.exp(lt - m_fin)
            l_s[g] = alpha * l_s[g] + jnp.sum(jnp.sum(p.reshape(4, KB // 4, GQ), axis=0), axis=0, keepdims=True)
            al_s[g] = alpha
            p_s[g] = p.astype(BF16)
            m_s[g] = m_new
        for g in range(n_groups):
            vb = v_ref[0, pl.ds(ks, KB), g * HEAD_DIM:(g + 1) * HEAD_DIM]
            acc_s[g] = al_s[g] * acc_s[g] + _dot_tn(vb, p_s[g])
        return carry

    lax.fori_loop(0, npair, kv_pair, 0)
    for g in range(n_groups):
        ot = acc_s[g] / l_s[g]
        for r in range(KV_GROUP):
            h = g * KV_GROUP + r
            o_ref[0, :, h * HEAD_DIM:(h + 1) * HEAD_DIM] = ot[:, r * QB:(r + 1) * QB].T.astype(o_ref.dtype)


def _attn_prompt(q, k, v, qi, ki, wit, rel_bias):
    B, S, HD = q.shape
    n_heads = HD // HEAD_DIM
    n_groups = n_heads // KV_GROUP
    KD = k.shape[2]
    n_idx_heads = wit.shape[1]
    n_sel = min(TOPK_MAX, S // 4)
    QB = Q_BLOCK
    assert S % KB == 0
    return pl.pallas_call(
        functools.partial(_attn_prompt_body, n_sel=n_sel, n_heads=n_heads, n_idx_heads=n_idx_heads),
        grid=(B, S // QB),
        in_specs=[pl.BlockSpec(memory_space=pltpu.SMEM),
                  pl.BlockSpec((1, QB, HD), lambda b, i: (b, i, 0)),
                  pl.BlockSpec((1, S, KD), lambda b, i: (b, 0, 0)),
                  pl.BlockSpec((1, S, KD), lambda b, i: (b, 0, 0)),
                  pl.BlockSpec((1, QB, qi.shape[2]), lambda b, i: (b, i, 0)),
                  pl.BlockSpec((1, S, IDX_DIM), lambda b, i: (b, 0, 0)),
                  pl.BlockSpec((1, n_idx_heads, QB), lambda b, i: (b, 0, i))],
        out_specs=pl.BlockSpec((1, QB, HD), lambda b, i: (b, i, 0)),
        out_shape=jax.ShapeDtypeStruct((B, S, HD), BF16),
        scratch_shapes=[pltpu.VMEM((S, QB), I32), pltpu.VMEM((S, QB), F32),
                        pltpu.VMEM((3, n_heads, QB, QB), F32),
                        pltpu.VMEM((n_idx_heads * QB, IDX_DIM), BF16),
                        pltpu.VMEM((n_groups, 1, KV_GROUP * QB), F32),
                        pltpu.VMEM((n_groups, 1, KV_GROUP * QB), F32),
                        pltpu.VMEM((n_groups, 1, KV_GROUP * QB), F32),
                        pltpu.VMEM((n_groups, HEAD_DIM, KV_GROUP * QB), F32),
                        pltpu.VMEM((n_groups, KB, KV_GROUP * QB), F32),
                        pltpu.VMEM((n_groups, KB, KV_GROUP * QB), BF16)],
        compiler_params=_params("arbitrary", "arbitrary"),
        name="attn_prompt",
    )(rel_bias, q, k, v, qi, ki, wit)


SCORE_ROWS = 24


def _smp_scores_body(pt_ref, qi_ref, wcol_ref, kin_ref, *refs, n_pages, n_idx_heads):
    pages = refs[:n_pages]
    out_ref = refs[n_pages]
    q16 = qi_ref[0].astype(BF16)
    wcol = wcol_ref[0] * (n_idx_heads * IDX_DIM) ** -0.5
    rows = []
    for p in range(n_pages):
        d = _dot_nt(q16, pages[p][...].astype(BF16))
        rows.append(jnp.sum(jnp.maximum(d, 0.0) * wcol, axis=0, keepdims=True))
    kin = kin_ref[0].astype(BF16).astype(F32)
    dn = jnp.sum(q16.astype(F32) * kin, axis=1, keepdims=True)
    snew = jnp.sum(jnp.maximum(dn, 0.0) * wcol, axis=0, keepdims=True)
    lane = lax.broadcasted_iota(I32, (1, PAGE), 1)
    rows.append(jnp.where(lane == 0, snew, NEG_INF))
    rows.append(jnp.full((SCORE_ROWS - n_pages - 1, PAGE), NEG_INF, F32))
    out_ref[0] = jnp.concatenate(rows, axis=0)


def _smp_scores(page_table, qi3, wcol, kin3, cache_idx_k, layer):
    Bs, n_pages = page_table.shape
    n_idx_heads = qi3.shape[1]
    page_spec = lambda p: pl.BlockSpec((None, None, PAGE, IDX_DIM), lambda b, pt: (layer, pt[b, p], 0, 0))
    grid_spec = pltpu.PrefetchScalarGridSpec(
        num_scalar_prefetch=1,
        grid=(Bs,),
        in_specs=[pl.BlockSpec((1, n_idx_heads, IDX_DIM), lambda b, pt: (b, 0, 0)),
                  pl.BlockSpec((1, n_idx_heads, 1), lambda b, pt: (b, 0, 0)),
                  pl.BlockSpec((1, 1, IDX_DIM), lambda b, pt: (b, 0, 0))]
        + [page_spec(p) for p in range(n_pages)],
        out_specs=pl.BlockSpec((1, SCORE_ROWS, PAGE), lambda b, pt: (b, 0, 0)),
    )
    return pl.pallas_call(
        functools.partial(_smp_scores_body, n_pages=n_pages, n_idx_heads=n_idx_heads),
        grid_spec=grid_spec,
        out_shape=jax.ShapeDtypeStruct((Bs, SCORE_ROWS, PAGE), F32),
        compiler_params=_params("arbitrary"),
        name="smp_scores",
    )(page_table, qi3, wcol, kin3, *([cache_idx_k] * n_pages))


def _smp_select_body(sc_ref, o_ref, key_s, *, n_sel):
    Bs, W = sc_ref.shape
    key_s[...] = _sort_key(sc_ref[...])

    def count_ge(cand):
        return jnp.sum(jnp.where(key_s[...] >= cand, 1.0, 0.0), axis=1, keepdims=True)

    thr = _kth_largest_key(count_ge, (Bs, 1), n_sel)
    n_gt = jnp.sum(jnp.where(key_s[...] > thr, 1.0, 0.0), axis=1, keepdims=True)
    need = n_sel - n_gt
    r_io = lax.broadcasted_iota(I32, (LANES, LANES), 0)
    c_io = lax.broadcasted_iota(I32, (LANES, LANES), 1)
    utri = jnp.where(r_io <= c_io, 1.0, 0.0).astype(BF16)
    carry = jnp.zeros((Bs, 1), F32)
    for j in range(W // LANES):
        blk = key_s[:, j * LANES:(j + 1) * LANES]
        eq = blk == thr
        rank = carry + _dot(jnp.where(eq, 1.0, 0.0).astype(BF16), utri)
        keep = jnp.where(blk > thr, 1.0, jnp.where(eq & (rank <= need), 1.0, 0.0))
        o_ref[:, j * LANES:(j + 1) * LANES] = keep
        carry = rank[:, LANES - 1:LANES]


def _smp_select(scores, n_sel):
    Bs, W = scores.shape
    return pl.pallas_call(
        functools.partial(_smp_select_body, n_sel=n_sel),
        grid=(1,),
        in_specs=[pl.BlockSpec((Bs, W), lambda i: (0, 0))],
        out_specs=pl.BlockSpec((Bs, W), lambda i: (0, 0)),
        out_shape=jax.ShapeDtypeStruct((Bs, W), F32),
        scratch_shapes=[pltpu.VMEM((Bs, W), I32)],
        compiler_params=_params("arbitrary"),
        name="smp_select",
    )(scores)


def _smp_attn_body(pt_ref, rbt_ref, q_ref, kn_ref, vn_ref, sel_ref, *refs, n_pages, n_heads, n_kv):
    kpages = refs[:n_pages]
    vpages = refs[n_pages:2 * n_pages]
    o_ref = refs[2 * n_pages]
    lg_s, bias_s, expand_s = refs[2 * n_pages + 1:]
    R = PAGE * n_kv
    row = lax.broadcasted_iota(I32, (1, R), 1)
    head = lax.broadcasted_iota(I32, (n_heads, 1), 0)
    own_group = (row % n_kv) == (head // KV_GROUP)

    @pl.when(pl.program_id(0) == 0)
    def _():
        k_io = lax.broadcasted_iota(I32, (PAGE, R), 0)
        r_io = lax.broadcasted_iota(I32, (PAGE, R), 1)
        expand_s[...] = jnp.where(r_io // n_kv == k_io, 1.0, 0.0).astype(BF16)
        bucket = _rel_bucket(PAGE - row // n_kv)
        acc = jnp.zeros((n_heads, R), F32)
        for bb in range(N_BUCKETS):
            acc = jnp.where(bucket == bb, rbt_ref[:, bb:bb + 1], acc)
        bias_s[...] = acc

    q16 = q_ref[0]
    scale = HEAD_DIM ** -0.5
    far_bias = rbt_ref[:, N_BUCKETS - 1:N_BUCKETS]
    sel = sel_ref[0]
    selx = _dot(sel.astype(BF16), expand_s[...])

    m = jnp.full((n_heads, LANES), NEG_INF, F32)
    for p in range(n_pages):
        lt = _dot_nt(q16, kpages[p][...].astype(BF16)) * scale
        lt = lt + (bias_s[...] if p == n_pages - 1 else far_bias)
        lt = jnp.where(own_group, jnp.where(selx[p:p + 1, :] > 0.5, lt, NEG_INF), NEG_INF)
        lg_s[p] = lt
        for c in range(R // LANES):
            m = jnp.maximum(m, lt[:, c * LANES:(c + 1) * LANES])
    m = jnp.max(m, axis=1, keepdims=True)
    qf = q16.astype(F32)
    ln = jnp.sum(qf * kn_ref[0].astype(BF16).astype(F32), axis=1, keepdims=True) * scale + rbt_ref[:, 0:1]
    ln = jnp.where(sel[n_pages:n_pages + 1, 0:1] > 0.5, ln, NEG_INF)
    m = jnp.maximum(m, ln)

    pn = jnp.exp(ln - m)
    l = pn
    acc = pn.astype(BF16).astype(F32) * vn_ref[0].astype(BF16).astype(F32)
    for p in range(n_pages):
        pr = jnp.exp(lg_s[p] - m)
        l = l + jnp.sum(pr, axis=1, keepdims=True)
        acc = acc + _dot(pr.astype(BF16), vpages[p][...].astype(BF16))
    o_ref[0] = (acc / l).astype(o_ref.dtype)


def _smp_attn(page_table, rbt_pad, q3, kn_rep, vn_rep, sel3, cache_k4, cache_v4, layer, n_kv):
    Bs, n_pages = page_table.shape
    n_heads = q3.shape[1]
    R = PAGE * n_kv
    page_spec = lambda p: pl.BlockSpec((None, None, R, HEAD_DIM), lambda b, pt: (layer, pt[b, p], 0, 0))
    head_spec = lambda: pl.BlockSpec((1, n_heads, HEAD_DIM), lambda b, pt: (b, 0, 0))
    grid_spec = pltpu.PrefetchScalarGridSpec(
        num_scalar_prefetch=1,
        grid=(Bs,),
        in_specs=[pl.BlockSpec((n_heads, LANES), lambda b, pt: (0, 0)),
                  head_spec(), head_spec(), head_spec(),
                  pl.BlockSpec((1, SCORE_ROWS, PAGE), lambda b, pt: (b, 0, 0))]
        + [page_spec(p) for p in range(n_pages)] * 2,
        out_specs=head_spec(),
        scratch_shapes=[pltpu.VMEM((n_pages, n_heads, R), F32), pltpu.VMEM((n_heads, R), F32),
                        pltpu.VMEM((PAGE, R), BF16)],
    )
    return pl.pallas_call(
        functools.partial(_smp_attn_body, n_pages=n_pages, n_heads=n_heads, n_kv=n_kv),
        grid_spec=grid_spec,
        out_shape=jax.ShapeDtypeStruct((Bs, n_heads, HEAD_DIM), BF16),
        compiler_params=_params("arbitrary"),
        name="smp_attn",
    )(page_table, rbt_pad, q3, kn_rep, vn_rep, sel3, *([cache_k4] * n_pages), *([cache_v4] * n_pages))


def _merge_body(yr_ref, ya_ref, wa_ref, wb_ref, ga0_ref, ga1_ref, gb0_ref, gb1_ref, o_ref, *, off):
    ga = jnp.concatenate([ga0_ref[:, off:], ga1_ref[:, :off]], axis=1)
    gb = jnp.concatenate([gb0_ref[:, off:], gb1_ref[:, :off]], axis=1)
    ba = _dot(yr_ref[...], wa_ref[...].astype(BF16))
    bb = _dot(ya_ref[...], wb_ref[...].astype(BF16))
    o_ref[...] = (jax.nn.sigmoid(ga) * ba + jax.nn.sigmoid(gb) * bb).astype(o_ref.dtype)


def _merge(yr, ya, wa, wb, ztail, off, l, *, tm, tn):
    M, K = yr.shape
    D = wa.shape[2]
    nj = D // tn
    assert 0 < off < tn and ztail.shape[1] >= (2 * nj + 1) * tn
    gate = lambda blk: pl.BlockSpec((tm, tn), lambda j, i: (i, blk + j))
    return pl.pallas_call(
        functools.partial(_merge_body, off=off),
        grid=(nj, M // tm),
        in_specs=[pl.BlockSpec((tm, K), lambda j, i: (i, 0)),
                  pl.BlockSpec((tm, K), lambda j, i: (i, 0)),
                  pl.BlockSpec((None, K, tn), lambda j, i: (l, 0, j)),
                  pl.BlockSpec((None, K, tn), lambda j, i: (l, 0, j)),
                  gate(0), gate(1), gate(nj), gate(nj + 1)],
        out_specs=pl.BlockSpec((tm, tn), lambda j, i: (i, j)),
        out_shape=jax.ShapeDtypeStruct((M, D), BF16),
        compiler_params=_params("arbitrary", "arbitrary"),
        name="merge",
    )(yr, ya, wa, wb, ztail, ztail, ztail, ztail)


def _proj_res_body(a_ref, w_ref, x_ref, g_ref, o_ref):
    o_ref[0] = x_ref[0] + g_ref[0] * _dot(a_ref[...], w_ref[...].astype(BF16))


def _proj_res(a, w, l, x, mod, g_chunk, *, tm, tn):
    B, T, D = x.shape
    K = a.shape[1]
    tpb = T // tm
    nj = D // tn
    per_tok = mod.shape[1] == T
    tg = tm if per_tok else 1
    gmap = (lambda j, i: (i // tpb, i % tpb, g_chunk * nj + j)) if per_tok else \
        (lambda j, i: (i // tpb, 0, g_chunk * nj + j))
    return pl.pallas_call(
        _proj_res_body,
        grid=(nj, B * tpb),
        in_specs=[pl.BlockSpec((tm, K), lambda j, i: (i, 0)),
                  pl.BlockSpec((None, K, tn), lambda j, i: (l, 0, j)),
                  pl.BlockSpec((1, tm, tn), lambda j, i: (i // tpb, i % tpb, j)),
                  pl.BlockSpec((1, tg, tn), gmap)],
        out_specs=pl.BlockSpec((1, tm, tn), lambda j, i: (i // tpb, i % tpb, j)),
        out_shape=jax.ShapeDtypeStruct((B, T, D), F32),
        compiler_params=_params("arbitrary", "arbitrary"),
        name="proj_res",
    )(a, w, x, mod)


def _router_body(h_ref, r_ref, o_ref, *, n_experts):
    lg = _dot(h_ref[...].astype(BF16), r_ref[...].astype(BF16))
    lane = lax.broadcasted_iota(I32, lg.shape, 1)
    lg = jnp.where(lane < n_experts, lg, NEG_INF)
    m1 = jnp.max(lg, axis=1, keepdims=True)
    i1 = jnp.min(jnp.where(lg == m1, lane, LANES), axis=1, keepdims=True)
    rest = jnp.where(lane == i1, NEG_INF, lg)
    m2 = jnp.max(rest, axis=1, keepdims=True)
    i2 = jnp.min(jnp.where(rest == m2, lane, LANES), axis=1, keepdims=True)
    e2 = jnp.exp(m2 - m1)
    den = 1.0 + e2
    o_ref[...] = jnp.where(lane == 0, i1.astype(F32),
                           jnp.where(lane == 1, i2.astype(F32),
                                     jnp.where(lane == 2, 1.0 / den, jnp.where(lane == 3, e2 / den, 0.0))))


def _router(h, router_pad, *, tm, n_experts):
    M, D = h.shape
    return pl.pallas_call(
        functools.partial(_router_body, n_experts=n_experts),
        grid=(M // tm,),
        in_specs=[pl.BlockSpec((tm, D), lambda i: (i, 0)),
                  pl.BlockSpec((D, LANES), lambda i: (0, 0))],
        out_specs=pl.BlockSpec((tm, LANES), lambda i: (i, 0)),
        out_shape=jax.ShapeDtypeStruct((M, LANES), F32),
        compiler_params=_params("arbitrary"),
        name="router",
    )(h, router_pad)


def _ffn_body(h_ref, w1_ref, w3_ref, w2_ref, x_ref, g2_ref, o_ref):
    f = pl.program_id(1)

    @pl.when(f == 0)
    def _():
        o_ref[...] = jnp.zeros(o_ref.shape, F32)

    h = h_ref[...]
    a = _dot(h, w1_ref[...].astype(BF16))
    b = _dot(h, w3_ref[...].astype(BF16))
    act = a * jax.nn.sigmoid(a) * b
    o_ref[0] += _dot(act.astype(BF16), w2_ref[...].astype(BF16))

    @pl.when(f == pl.num_programs(1) - 1)
    def _():
        o_ref[0] = x_ref[0] + g2_ref[0] * o_ref[0]


def _ffn(h, w1, w3, w2, layer_idx, x, mod, g_chunk, *, tm, tf):
    B, T, D = x.shape
    tpb = T // tm
    F = w1.shape[2]
    per_tok = mod.shape[1] == T
    tg = tm if per_tok else 1
    gmap = (lambda i, f: (i // tpb, i % tpb, g_chunk)) if per_tok else (lambda i, f: (i // tpb, 0, g_chunk))
    return pl.pallas_call(
        _ffn_body,
        grid=(B * tpb, F // tf),
        in_specs=[pl.BlockSpec((tm, D), lambda i, f: (i, 0)),
                  pl.BlockSpec((None, D, tf), lambda i, f: (layer_idx, 0, f)),
                  pl.BlockSpec((None, D, tf), lambda i, f: (layer_idx, 0, f)),
                  pl.BlockSpec((None, tf, D), lambda i, f: (layer_idx, f, 0)),
                  pl.BlockSpec((1, tm, D), lambda i, f: (i // tpb, i % tpb, 0)),
                  pl.BlockSpec((1, tg, D), gmap)],
        out_specs=pl.BlockSpec((1, tm, D), lambda i, f: (i // tpb, i % tpb, 0)),
        out_shape=jax.ShapeDtypeStruct((B, T, D), F32),
        compiler_params=_params("arbitrary", "arbitrary"),
        name="ffn_dense",
    )(h, w1, w3, w2, x, mod)


def _moe_chunk(tm, n_f):
    return -(-(-(-tm // n_f)) // 8) * 8


def _moe_body(te_ref, dst_ref, nused_ref, h_hbm, w1_ref, w3_ref, w2_ref, rw_ref, out_hbm,
              xg_s, xb_s, acc_s, os_s, sem_in, sem_out, *, n_tok, n_f, n_tiles):
    i = pl.program_id(0)
    f = pl.program_id(1)
    tm = xb_s.shape[0]
    chunk = _moe_chunk(tm, n_f)
    ts = n_f * chunk
    slot = i % 2

    def gather(base, r, s):
        d = dst_ref[base + r]
        tok = jnp.maximum(d - jnp.where(d >= n_tok, n_tok, 0), 0)
        pltpu.make_async_copy(h_hbm.at[pl.ds(tok, 1), :], xg_s.at[s, pl.ds(r, 1), :], sem_in.at[s]).start()

    def scatter(base, r):
        d = dst_ref[base + r]
        row = jnp.where(d >= 0, d, 2 * n_tok + r)
        pltpu.make_async_copy(os_s.at[pl.ds(r, 1), :], out_hbm.at[pl.ds(row, 1), :], sem_out).start()

    def wait_gather(s):
        pltpu.make_async_copy(h_hbm.at[pl.ds(0, ts), :], xg_s.at[s, pl.ds(0, ts), :], sem_in.at[s]).wait()

    def wait_scatter():
        pltpu.make_async_copy(os_s.at[pl.ds(0, ts), :], out_hbm.at[pl.ds(0, ts), :], sem_out).wait()

    @pl.when((i == 0) & (f == 0))
    def _():
        os_s[...] = jnp.zeros(os_s.shape, F32)

        def start(r, c):
            gather(ts, r, 0)
            return c

        lax.fori_loop(0, ts, start, 0)

    @pl.when(f == 0)
    def _():
        wait_gather(slot)
        xb_s[...] = xg_s[slot, 0:tm, :].astype(BF16)
        acc_s[...] = jnp.zeros(acc_s.shape, F32)

    def move_rows():
        for k in range(chunk):
            r = f * chunk + k
            gather((i + 2) * ts, r, 1 - slot)
            scatter(i * ts, r)

    @pl.when(i < nused_ref[0])
    def _():
        move_rows()
        x = xb_s[...]
        a = _dot(x, w1_ref[...].astype(BF16))
        b = _dot(x, w3_ref[...].astype(BF16))
        act = a * jax.nn.sigmoid(a) * b
        acc_s[...] += _dot(act.astype(BF16), w2_ref[...].astype(BF16))

    @pl.when(i >= nused_ref[0])
    def _():
        move_rows()

    @pl.when(f == n_f - 1)
    def _():
        wait_scatter()

        @pl.when(i < nused_ref[0])
        def _():
            os_s[0:tm, :] = acc_s[...] * rw_ref[...]

        @pl.when(i == n_tiles - 1)
        def _():
            def start(r, c):
                scatter((i + 1) * ts, r)
                return c

            lax.fori_loop(0, ts, start, 0)
            wait_scatter()
            wait_gather(1 - slot)


def _moe_sparse(h_all, top_i, top_w, w1, w3, w2, layer_idx, *, tm, tf):
    M, D = h_all.shape
    nE, _, F = w1.shape[1:]
    n_asg = 2 * M
    n_tiles = -(-(n_asg + nE * (tm - 1)) // tm)
    m_pad = n_tiles * tm
    nf = F // tf
    ts = nf * _moe_chunk(tm, nf)
    e_flat = top_i.T.reshape(n_asg)
    onehot = (e_flat[:, None] == jnp.arange(nE, dtype=I32)[None, :]).astype(I32)
    cum = jnp.cumsum(onehot, axis=0)
    counts = cum[-1]
    rank = jnp.take_along_axis(cum, e_flat[:, None], axis=1)[:, 0] - 1
    padded = -(-counts // tm) * tm
    ends = jnp.cumsum(padded)
    starts = ends - padded
    pos = starts[e_flat] + rank
    dst = jnp.full((m_pad,), -1, I32).at[pos].set(jnp.arange(n_asg, dtype=I32))
    dst = jnp.pad(dst.reshape(n_tiles, tm), ((1, 1), (0, ts - tm)), constant_values=-1).reshape(-1)
    roww = jnp.zeros((m_pad,), F32).at[pos].set(top_w.T.reshape(n_asg))
    tile_start = jnp.arange(n_tiles, dtype=I32) * tm
    n_used = (ends[-1] // tm).astype(I32)
    tile_e = jnp.minimum(jnp.searchsorted(ends, tile_start, side="right").astype(I32), nE - 1)
    tile_e = jnp.where(tile_start < ends[-1], tile_e, tile_e[jnp.maximum(n_used - 1, 0)])

    def fidx(i, f, nused):
        return jnp.where(i < nused[0], f, nf - 1)

    grid_spec = pltpu.PrefetchScalarGridSpec(
        num_scalar_prefetch=3,
        grid=(n_tiles, nf),
        in_specs=[pl.BlockSpec(memory_space=pl.ANY),
                  pl.BlockSpec((None, None, D, tf), lambda i, f, te, ds_, nu: (layer_idx, te[i], 0, fidx(i, f, nu))),
                  pl.BlockSpec((None, None, D, tf), lambda i, f, te, ds_, nu: (layer_idx, te[i], 0, fidx(i, f, nu))),
                  pl.BlockSpec((None, None, tf, D), lambda i, f, te, ds_, nu: (layer_idx, te[i], fidx(i, f, nu), 0)),
                  pl.BlockSpec((tm, 1), lambda i, f, te, ds_, nu: (i, 0))],
        out_specs=pl.BlockSpec(memory_space=pl.ANY),
        scratch_shapes=[pltpu.VMEM((2, ts, D), F32), pltpu.VMEM((tm, D), BF16), pltpu.VMEM((tm, D), F32),
                        pltpu.VMEM((ts, D), F32), pltpu.SemaphoreType.DMA((2,)), pltpu.SemaphoreType.DMA(())],
    )
    return pl.pallas_call(
        functools.partial(_moe_body, n_tok=M, n_f=nf, n_tiles=n_tiles),
        grid_spec=grid_spec,
        out_shape=jax.ShapeDtypeStruct((n_asg + ts, D), F32),
        compiler_params=_params("arbitrary", "arbitrary"),
        name="moe_sparse",
    )(tile_e, dst, n_used.reshape(1), h_all, w1, w3, w2, roww.reshape(m_pad, 1))


def _moe_combine_body(x_ref, g2_ref, a_ref, b_ref, o_ref):
    o_ref[0] = x_ref[0] + g2_ref[0] * (a_ref[...] + b_ref[...])


def _moe_combine(x, mod, g_chunk, y2, row0, n_tok, *, tt):
    B, T, D = x.shape
    per_tok = mod.shape[1] == T
    tg = tt if per_tok else 1
    tpb = T // tt
    assert row0 % tt == 0 and n_tok % tt == 0
    rb0, rb1 = row0 // tt, (n_tok + row0) // tt
    return pl.pallas_call(
        _moe_combine_body,
        grid=(B, tpb),
        in_specs=[pl.BlockSpec((1, tt, D), lambda b, t: (b, t, 0)),
                  pl.BlockSpec((1, tg, D), lambda b, t: (b, t if per_tok else 0, g_chunk)),
                  pl.BlockSpec((tt, D), lambda b, t: (rb0 + b * tpb + t, 0)),
                  pl.BlockSpec((tt, D), lambda b, t: (rb1 + b * tpb + t, 0))],
        out_specs=pl.BlockSpec((1, tt, D), lambda b, t: (b, t, 0)),
        out_shape=jax.ShapeDtypeStruct((B, T, D), F32),
        compiler_params=_params("arbitrary", "arbitrary"),
        name="moe_combine",
    )(x, mod, y2, y2)


def _tile(n, prefs):
    for p in prefs:
        if n % p == 0:
            return p
    return n


def _mixer_half(l, x, mod, P, attend, rglru, kv_prev, *, tt, tm):
    B, T, D = x.shape
    M = B * T
    h = _norm_mod(x, P["norm_g"][l, 0:1], mod, 1, 0, BF16, tt=tt).reshape(M, D)
    w_in = P["w_in"]
    (zug,) = _mm(h, w_in, l, 0, 2 * D, [F32], tm=tm, tn=1024)
    (q,) = _mm(h, w_in, l, 2 * D, D, [BF16], tm=tm, tn=1024)
    KD = P["n_kv_cols"] // 2
    depth = w_in.shape[0]
    k_all, k16 = _mm_stacked(h, w_in, l, depth, P["col_kv"], KD, kv_prev and kv_prev[0], tm=tm, tn=_tile(KD, (1024,)))
    v_all, v16 = _mm_stacked(h, w_in, l, depth, P["col_kv"] + KD, KD, kv_prev and kv_prev[1], tm=tm,
                             tn=_tile(KD, (1024,)))
    (qi,) = _mm(h, w_in, l, P["col_qi"], P["n_qi_cols"], [BF16], tm=tm, tn=1024)
    tnt = 512
    n_tail = -(-(P["n_kiwi"] + 2 * D) // tnt) * tnt
    (ztail,) = _mm(h, w_in, l, P["col_ki"], n_tail, [F32], tm=tm, tn=tnt)
    kiwi = ztail[:, :LANES]
    y_rnn, h_last, new_hist = rglru(l, zug)
    y_att = attend(l, q, (k_all, v_all), (k16, v16), qi, kiwi)
    merged = _merge(y_rnn.reshape(M, D), y_att.reshape(M, D), P["w_branch_a"], P["w_branch_b"], ztail,
                    P["n_kiwi"], l, tm=_tile(M, (512,)), tn=tnt)
    x = _proj_res(merged, P["w_out"], l, x, mod, 2, tm=tm, tn=1024)
    return x, h_last, new_hist, (k_all, v_all), kiwi


def kernel(x_prompt, x_sample, c_prompt, c_sample, cache_k, cache_v, cache_idx_k, state_h, state_conv,
           page_table, w_ada, b_ada, norm_g, w_in, conv_w, conv_b, lru_wa, lru_ba, lru_wx, lru_bx,
           lru_lambda, w_branch_a, w_branch_b, w_out, rel_bias, ffn_w1, ffn_w3, ffn_w2,
           moe_router, moe_w1, moe_w3, moe_w2, final_g):
    Bp, S, D = x_prompt.shape
    Bs = x_sample.shape[0]
    Mp = Bp * S
    depth = w_in.shape[0]
    n_heads = rel_bias.shape[1]
    n_phys, _, n_kv = cache_k.shape[1:4]
    KD = n_kv * HEAD_DIM
    n_idx_heads = (w_in.shape[2] - (5 * D + 2 * KD + IDX_DIM)) // (IDX_DIM + 1)
    n_pages = page_table.shape[1]
    past = n_pages * PAGE
    col_qi = 3 * D + 2 * KD
    col_ki = col_qi + n_idx_heads * IDX_DIM
    n_kiwi = IDX_DIM + n_idx_heads
    n_experts = moe_router.shape[2]

    P = dict(
        col_kv=3 * D, n_kv_cols=2 * KD, col_qi=col_qi, n_qi_cols=n_idx_heads * IDX_DIM,
        col_ki=col_ki, n_kiwi=n_kiwi, norm_g=norm_g, w_in=w_in, conv_w=conv_w, lru_wa=lru_wa, lru_wx=lru_wx,
        conv_b3=conv_b[:, None, :], lru_ba3=lru_ba[:, None, :], lru_bx3=lru_bx[:, None, :],
        lru_lambda3=lru_lambda[:, None, :],
        w_branch_a=w_branch_a, w_branch_b=w_branch_b, w_out=w_out,
    )
    router_pad = jnp.pad(moe_router, ((0, 0), (0, 0), (0, LANES - n_experts)))
    rbt_pad = jnp.pad(rel_bias.T, ((0, 0), (0, LANES - rel_bias.shape[0])))
    cache_k4 = cache_k.reshape(depth, n_phys, PAGE * n_kv, HEAD_DIM)
    cache_v4 = cache_v.reshape(depth, n_phys, PAGE * n_kv, HEAD_DIM)

    n_c = Bp + Bs
    n_c_pad = -(-n_c // 8) * 8
    c_all = jnp.concatenate([c_prompt, c_sample, jnp.zeros((n_c_pad - n_c, D), F32)], axis=0)
    b_ada3 = b_ada[:, None, :]

    def attend_prompt(l, q, kv32, kv16, qi, kiwi):
        ki = kiwi[:, :IDX_DIM].astype(BF16).reshape(Bp, S, IDX_DIM)
        wit = kiwi[:, IDX_DIM:n_kiwi].reshape(Bp, S, n_idx_heads).transpose(0, 2, 1)
        return _attn_prompt(q.reshape(Bp, S, D), kv16[0].reshape(Bp, S, KD), kv16[1].reshape(Bp, S, KD),
                            qi.reshape(Bp, S, -1), ki, wit, rel_bias)

    def attend_sample(l, q, kv32, kv16, qi, kiwi):
        qi3 = qi.reshape(Bs, n_idx_heads, IDX_DIM)
        wcol = kiwi[:, IDX_DIM:n_kiwi].reshape(Bs, n_idx_heads, 1)
        kin3 = kiwi[:, :IDX_DIM].reshape(Bs, 1, IDX_DIM)
        scores = _smp_scores(page_table, qi3, wcol, kin3, cache_idx_k, l)
        n_sel = min(TOPK_MAX, (past + 1) // 4)
        sel3 = _smp_select(scores.reshape(Bs, SCORE_ROWS * PAGE), n_sel).reshape(Bs, SCORE_ROWS, PAGE)
        q3 = q.reshape(Bs, n_heads, HEAD_DIM)
        kn_rep = jnp.repeat(kv32[0][l].reshape(Bs, n_kv, HEAD_DIM), KV_GROUP, axis=1)
        vn_rep = jnp.repeat(kv32[1][l].reshape(Bs, n_kv, HEAD_DIM), KV_GROUP, axis=1)
        return _smp_attn(page_table, rbt_pad, q3, kn_rep, vn_rep, sel3, cache_k4, cache_v4, l, n_kv)

    xp, xs = x_prompt, x_sample.reshape(1, Bs, D)
    outs_p, outs_s = [], []
    kvp = kvs = None
    for l in range(depth):
        (mod,) = _mm(c_all, w_ada, l, 0, 6 * D, [F32], tm=n_c_pad, tn=1024, bias=b_ada3, silu_in=True)
        mod_p = mod[:Bp].reshape(Bp, 1, 6 * D)
        mod_s = mod[Bp:n_c].reshape(1, Bs, 6 * D)
        hist_t = state_conv[l].transpose(1, 0, 2)

        xp, hp, cp, kvp, kiwip = _mixer_half(
            l, xp, mod_p, P, attend_prompt, lambda l_, zug: _rglru_prompt(zug.reshape(Bp, S, 2 * D), P, l_, tc=256),
            kvp, tt=512, tm=1024)

        def rglru_s(l_, zug, hist_t=hist_t):
            y, h = _rglru_sample(zug, hist_t, state_h[l_], P, l_)
            new_hist = jnp.concatenate([state_conv[l_][:, 1:], zug[:, None, :D]], axis=1)
            return y, h, new_hist

        xs, hs, cs, kvs, kiwis = _mixer_half(l, xs, mod_s, P, attend_sample, rglru_s, kvs, tt=Bs, tm=Bs)

        g2p, g2s = P["norm_g"][l, 1:2], P["norm_g"][l, 1:2]
        if l % 2 == 0:
            h2p = _norm_mod(xp, g2p, mod_p, 4, 3, BF16, tt=512).reshape(Mp, D)
            h2s = _norm_mod(xs, g2s, mod_s, 4, 3, BF16, tt=Bs).reshape(Bs, D)
            xp = _ffn(h2p, ffn_w1, ffn_w3, ffn_w2, l // 2, xp, mod_p, 5, tm=512, tf=512)
            xs = _ffn(h2s, ffn_w1, ffn_w3, ffn_w2, l // 2, xs, mod_s, 5, tm=Bs, tf=512)
        else:
            n_tok = Mp + Bs
            h_all = _norm_mod_rows(xp, g2p, mod_p, 4, 3, n_tok, 0, None, tt=512)
            h_all = _norm_mod_rows(xs, g2s, mod_s, 4, 3, n_tok, Mp, h_all, tt=Bs)
            route = _router(h_all, router_pad[l // 2], tm=_tile(n_tok, (1024, 640, 512, 256, 128)),
                            n_experts=n_experts)
            y2 = _moe_sparse(h_all, route[:, 0:2].astype(I32), route[:, 2:4], moe_w1, moe_w3, moe_w2, l // 2,
                             tm=512, tf=512)
            xp = _moe_combine(xp, mod_p, 5, y2, 0, n_tok, tt=Bs)
            xs = _moe_combine(xs, mod_s, 5, y2, Mp, n_tok, tt=Bs)

        outs_p.append((kiwip[:, :IDX_DIM].reshape(Bp, S, IDX_DIM), hp.reshape(Bp, D), cp))
        outs_s.append((kiwis[:, :IDX_DIM].reshape(Bs, 1, IDX_DIM), hs, cs))
    y_prompt = _final_norm(xp, final_g[None, :], tt=512)
    y_sample = _final_norm(xs, final_g[None, :], tt=Bs).reshape(Bs, 1, D)
    stack = lambda outs, i: jnp.stack([o[i] for o in outs])
    kv_shape_p = (depth, Bp, S, n_kv, HEAD_DIM)
    kv_shape_s = (depth, Bs, 1, n_kv, HEAD_DIM)
    return (y_prompt, y_sample,
            kvp[0].reshape(kv_shape_p), kvp[1].reshape(kv_shape_p),
            stack(outs_p, 0), stack(outs_p, 1), stack(outs_p, 2),
            kvs[0].reshape(kv_shape_s), kvs[1].reshape(kv_shape_s),
            stack(outs_s, 0), stack(outs_s, 1), stack(outs_s, 2))
```

```python
import functools
import math

import jax
import jax.numpy as jnp
from jax import lax
from jax.experimental import pallas as pl
from jax.experimental.pallas import tpu as pltpu

F32 = jnp.float32
BF16 = jnp.bfloat16
I32 = jnp.int32

VMEM_LIMIT_BYTES = 56 * 1024 * 1024
LANES = 128

EPS = 1e-6
LRU_C = 8.0
RNN_BLOCK = 128
CONV_W = 4
HEAD_DIM = 128
KV_GROUP = 2
IDX_DIM = 64
TOPK_MAX = 256
Q_BLOCK = 128
N_BUCKETS = 32
MAX_DISTANCE = 128
PAGE = 128
NEG_INF = float("-inf")
INT_MIN = -(2 ** 31)


def _params(*sem):
    return pltpu.CompilerParams(dimension_semantics=sem, vmem_limit_bytes=VMEM_LIMIT_BYTES)


def _dot(a, b):
    return jnp.dot(a, b, preferred_element_type=F32)


def _dot_nt(a, b):
    return lax.dot_general(a, b, (((1,), (1,)), ((), ())), preferred_element_type=F32)


def _dot_tn(a, b):
    return lax.dot_general(a, b, (((0,), (0,)), ((), ())), preferred_element_type=F32)


def _mm_body(x_ref, w_ref, *refs, silu_in, has_bias, w_nk):
    x = x_ref[...]
    if silu_in:
        x = x.astype(F32)
        x = x * jax.nn.sigmoid(x)
    acc = (_dot_nt if w_nk else _dot)(x.astype(BF16), w_ref[...].astype(BF16))
    outs = refs
    if has_bias:
        acc = acc + refs[0][...]
        outs = refs[1:]
    for o in outs:
        o[...] = acc.astype(o.dtype)


def _w_spec(K, tn, layer, cb0, w_nk):
    if w_nk:
        return pl.BlockSpec((None, tn, K), lambda j, i: (layer, cb0 + j, 0))
    return pl.BlockSpec((None, K, tn), lambda j, i: (layer, 0, cb0 + j))


def _mm(x, w, layer, col0, ncols, out_dtypes, *, tm, tn, bias=None, silu_in=False, w_nk=False):
    M, K = x.shape
    assert M % tm == 0 and ncols % tn == 0 and col0 % tn == 0
    cb0 = col0 // tn
    in_specs = [pl.BlockSpec((tm, K), lambda j, i: (i, 0)), _w_spec(K, tn, layer, cb0, w_nk)]
    args = [x, w]
    if bias is not None:
        in_specs.append(pl.BlockSpec((None, 1, tn), lambda j, i: (layer, 0, cb0 + j)))
        args.append(bias)
    outs = pl.pallas_call(
        functools.partial(_mm_body, silu_in=silu_in, has_bias=bias is not None, w_nk=w_nk),
        grid=(ncols // tn, M // tm),
        in_specs=in_specs,
        out_specs=[pl.BlockSpec((tm, tn), lambda j, i: (i, j)) for _ in out_dtypes],
        out_shape=[jax.ShapeDtypeStruct((M, ncols), dt) for dt in out_dtypes],
        compiler_params=_params("arbitrary", "arbitrary"),
        name="mm",
    )(*args)
    return outs


def _mm_stacked_body(x_ref, w_ref, *refs, w_nk):
    o32_ref, o16_ref = refs[-2:]
    acc = (_dot_nt if w_nk else _dot)(x_ref[...], w_ref[...].astype(BF16))
    o32_ref[...] = acc
    o16_ref[...] = acc.astype(BF16)


def _mm_stacked(x, w, layer, depth, col0, ncols, prev, *, tm, tn, w_nk=False):
    M, K = x.shape
    assert M % tm == 0 and ncols % tn == 0 and col0 % tn == 0
    cb0 = col0 // tn
    in_specs = [pl.BlockSpec((tm, K), lambda j, i: (i, 0)), _w_spec(K, tn, layer, cb0, w_nk)]
    args = [x, w]
    aliases = {}
    if prev is not None:
        in_specs.append(pl.BlockSpec(memory_space=pl.ANY))
        args.append(prev)
        aliases = {2: 0}
    return pl.pallas_call(
        functools.partial(_mm_stacked_body, w_nk=w_nk),
        grid=(ncols // tn, M // tm),
        in_specs=in_specs,
        out_specs=[pl.BlockSpec((None, tm, tn), lambda j, i: (layer, i, j)),
                   pl.BlockSpec((tm, tn), lambda j, i: (i, j))],
        out_shape=[jax.ShapeDtypeStruct((depth, M, ncols), F32), jax.ShapeDtypeStruct((M, ncols), BF16)],
        input_output_aliases=aliases,
        compiler_params=_params("arbitrary", "arbitrary"),
        name="mm_stacked",
    )(*args)


def _rms(x, g):
    return x * lax.rsqrt(jnp.mean(x * x, axis=-1, keepdims=True) + EPS) * g


def _norm_mod_body(x_ref, g_ref, sc_ref, sh_ref, o_ref):
    y = _rms(x_ref[0], g_ref[...])
    o_ref[0] = (y * (1.0 + sc_ref[0]) + sh_ref[0]).astype(o_ref.dtype)


def _norm_mod(x, g, mod, sc_chunk, sh_chunk, out_dtype, *, tt):
    B, T, D = x.shape
    per_tok = mod.shape[1] == T
    tg = tt if per_tok else 1
    mod_spec = lambda chunk: pl.BlockSpec((1, tg, D), lambda b, t: (b, t if per_tok else 0, chunk))
    return pl.pallas_call(
        _norm_mod_body,
        grid=(B, T // tt),
        in_specs=[pl.BlockSpec((1, tt, D), lambda b, t: (b, t, 0)),
                  pl.BlockSpec((1, D), lambda b, t: (0, 0)),
                  mod_spec(sc_chunk), mod_spec(sh_chunk)],
        out_specs=pl.BlockSpec((1, tt, D), lambda b, t: (b, t, 0)),
        out_shape=jax.ShapeDtypeStruct((B, T, D), out_dtype),
        compiler_params=_params("arbitrary", "arbitrary"),
        name="norm_mod",
    )(x, g, mod, mod)


def _norm_mod_rows_body(x_ref, g_ref, sc_ref, sh_ref, *refs):
    o_ref = refs[-1]
    y = _rms(x_ref[0], g_ref[...])
    o_ref[...] = y * (1.0 + sc_ref[0]) + sh_ref[0]


def _norm_mod_rows(x, g, mod, sc_chunk, sh_chunk, n_rows, row0, prev, *, tt):
    B, T, D = x.shape
    per_tok = mod.shape[1] == T
    tg = tt if per_tok else 1
    tpb = T // tt
    assert row0 % tt == 0
    mod_spec = lambda chunk: pl.BlockSpec((1, tg, D), lambda b, t: (b, t if per_tok else 0, chunk))
    in_specs = [pl.BlockSpec((1, tt, D), lambda b, t: (b, t, 0)),
                pl.BlockSpec((1, D), lambda b, t: (0, 0)),
                mod_spec(sc_chunk), mod_spec(sh_chunk)]
    args = [x, g, mod, mod]
    aliases = {}
    if prev is not None:
        in_specs.append(pl.BlockSpec(memory_space=pl.ANY))
        args.append(prev)
        aliases = {4: 0}
    return pl.pallas_call(
        _norm_mod_rows_body,
        grid=(B, tpb),
        in_specs=in_specs,
        out_specs=pl.BlockSpec((tt, D), lambda b, t: (row0 // tt + b * tpb + t, 0)),
        out_shape=jax.ShapeDtypeStruct((n_rows, D), F32),
        input_output_aliases=aliases,
        compiler_params=_params("arbitrary", "arbitrary"),
        name="norm_mod_rows",
    )(*args)


def _final_norm_body(x_ref, g_ref, o_ref):
    o_ref[0] = _rms(x_ref[0], g_ref[...])


def _final_norm(x, g, *, tt):
    B, T, D = x.shape
    return pl.pallas_call(
        _final_norm_body,
        grid=(B, T // tt),
        in_specs=[pl.BlockSpec((1, tt, D), lambda b, t: (b, t, 0)),
                  pl.BlockSpec((1, D), lambda b, t: (0, 0))],
        out_specs=pl.BlockSpec((1, tt, D), lambda b, t: (b, t, 0)),
        out_shape=jax.ShapeDtypeStruct((B, T, D), F32),
        compiler_params=_params("arbitrary", "arbitrary"),
        name="final_norm",
    )(x, g)


def _softplus(x):
    return jnp.maximum(x, 0.0) + jnp.log1p(jnp.exp(-jnp.abs(x)))


def _expm1(x):
    u = jnp.exp(x)
    return jnp.where(u == 1.0, x, jnp.where(u == 0.0, -1.0, (u - 1.0) * x / jnp.log(u)))


def _block_diag_dot(xc, w_ref):
    nb = w_ref.shape[0]
    outs = []
    for n in range(nb):
        xb = xc[:, n * RNN_BLOCK:(n + 1) * RNN_BLOCK].astype(BF16)
        outs.append(_dot(xb, w_ref[n].astype(BF16)))
    return jnp.concatenate(outs, axis=-1)


def _lru_gates(xc, wa_ref, ba, wx_ref, bx, lam):
    r = jax.nn.sigmoid(_block_diag_dot(xc, wa_ref) + ba)
    ig = jax.nn.sigmoid(_block_diag_dot(xc, wx_ref) + bx)
    log_a = (-LRU_C * r) * _softplus(-lam)
    a = jnp.exp(log_a)
    mult = jnp.sqrt(-_expm1(2.0 * log_a))
    return a, mult, ig


HIST = CONV_W - 1
EXT_PAD = 8


def _rglru_prompt_body(u_ref, gt_ref, cw_ref, cb_ref, wa_ref, ba_ref, wx_ref, bx_ref, lam_ref,
                       y_ref, hl_ref, nh_ref, ext_s, a_s, x_s, h_s):
    c = pl.program_id(1)
    tc = u_ref.shape[1]

    @pl.when(c == 0)
    def _():
        ext_s[0:EXT_PAD, :] = jnp.zeros((EXT_PAD, ext_s.shape[1]), F32)
        h_s[...] = jnp.zeros(h_s.shape, F32)

    @pl.when(c > 0)
    def _():
        ext_s[0:EXT_PAD, :] = ext_s[tc:tc + EXT_PAD, :]

    ext_s[EXT_PAD:EXT_PAD + tc, :] = u_ref[0]
    xc = cb_ref[...]
    for j in range(CONV_W):
        off = EXT_PAD - HIST + j
        xc = xc + ext_s[off:off + tc, :] * cw_ref[j:j + 1, :]
    a, mult, ig = _lru_gates(xc, wa_ref, ba_ref[...], wx_ref, bx_ref[...], lam_ref[...])
    row = c * tc + lax.broadcasted_iota(I32, (tc, 1), 0)
    mult = jnp.where(row == 0, 1.0, mult)
    a_s[...] = a
    x_s[...] = mult * ig * xc

    def step(t, h):
        h = a_s[pl.ds(t, 1), :] * h + x_s[pl.ds(t, 1), :]
        x_s[pl.ds(t, 1), :] = h
        return h

    h = lax.fori_loop(0, tc, step, h_s[0:1, :], unroll=8)
    h_s[0:1, :] = h
    y_ref[0] = (x_s[...] * jax.nn.gelu(gt_ref[0])).astype(y_ref.dtype)

    @pl.when(c == pl.num_programs(1) - 1)
    def _():
        hl_ref[0] = h
        nh_ref[0] = ext_s[EXT_PAD + tc - HIST:EXT_PAD + tc, :]


def _rglru_prompt(zug, P, l, *, tc):
    B, T, C2 = zug.shape
    C = C2 // 2
    nb = C // RNN_BLOCK
    vec = lambda: pl.BlockSpec((None, 1, C), lambda b, c: (l, 0, 0))
    blk = lambda: pl.BlockSpec((None, nb, RNN_BLOCK, RNN_BLOCK), lambda b, c: (l, 0, 0, 0))
    return pl.pallas_call(
        _rglru_prompt_body,
        grid=(B, T // tc),
        in_specs=[pl.BlockSpec((1, tc, C), lambda b, c: (b, c, 0)),
                  pl.BlockSpec((1, tc, C), lambda b, c: (b, c, 1)),
                  pl.BlockSpec((None, CONV_W, C), lambda b, c: (l, 0, 0)),
                  vec(), blk(), vec(), blk(), vec(), vec()],
        out_specs=[pl.BlockSpec((1, tc, C), lambda b, c: (b, c, 0)),
                   pl.BlockSpec((1, 1, C), lambda b, c: (b, 0, 0)),
                   pl.BlockSpec((1, HIST, C), lambda b, c: (b, 0, 0))],
        out_shape=[jax.ShapeDtypeStruct((B, T, C), BF16),
                   jax.ShapeDtypeStruct((B, 1, C), F32),
                   jax.ShapeDtypeStruct((B, HIST, C), F32)],
        scratch_shapes=[pltpu.VMEM((tc + EXT_PAD, C), F32), pltpu.VMEM((tc, C), F32),
                        pltpu.VMEM((tc, C), F32), pltpu.VMEM((8, C), F32)],
        compiler_params=_params("arbitrary", "arbitrary"),
        name="rglru_prompt",
    )(zug, zug, P["conv_w"], P["conv_b3"], P["lru_wa"], P["lru_ba3"], P["lru_wx"], P["lru_bx3"],
      P["lru_lambda3"])


def _rglru_sample_body(u_ref, gt_ref, hist_ref, h0_ref, cw_ref, cb_ref, wa_ref, ba_ref, wx_ref, bx_ref,
                       lam_ref, y_ref, h_ref):
    u = u_ref[...]
    xc = cb_ref[...]
    for j in range(HIST):
        xc = xc + hist_ref[j] * cw_ref[j:j + 1, :]
    xc = xc + u * cw_ref[HIST:HIST + 1, :]
    a, mult, ig = _lru_gates(xc, wa_ref, ba_ref[...], wx_ref, bx_ref[...], lam_ref[...])
    h = a * h0_ref[...] + mult * ig * xc
    h_ref[...] = h
    y_ref[...] = (h * jax.nn.gelu(gt_ref[...])).astype(y_ref.dtype)


def _rglru_sample(zug, hist_t, h0, P, l):
    Bs, C2 = zug.shape
    C = C2 // 2
    nb = C // RNN_BLOCK
    vec = lambda: pl.BlockSpec((None, 1, C), lambda i: (l, 0, 0))
    blk = lambda: pl.BlockSpec((None, nb, RNN_BLOCK, RNN_BLOCK), lambda i: (l, 0, 0, 0))
    return pl.pallas_call(
        _rglru_sample_body,
        grid=(1,),
        in_specs=[pl.BlockSpec((Bs, C), lambda i: (0, 0)),
                  pl.BlockSpec((Bs, C), lambda i: (0, 1)),
                  pl.BlockSpec((HIST, Bs, C), lambda i: (0, 0, 0)),
                  pl.BlockSpec((Bs, C), lambda i: (0, 0)),
                  pl.BlockSpec((None, CONV_W, C), lambda i: (l, 0, 0)),
                  vec(), blk(), vec(), blk(), vec(), vec()],
        out_specs=[pl.BlockSpec((Bs, C), lambda i: (0, 0)),
                   pl.BlockSpec((Bs, C), lambda i: (0, 0))],
        out_shape=[jax.ShapeDtypeStruct((Bs, C), BF16), jax.ShapeDtypeStruct((Bs, C), F32)],
        compiler_params=_params("arbitrary"),
        name="rglru_sample",
    )(zug, zug, hist_t, h0, P["conv_w"], P["conv_b3"], P["lru_wa"], P["lru_ba3"], P["lru_wx"],
      P["lru_bx3"], P["lru_lambda3"])


def _rel_bucket(dist):
    n = jnp.maximum(dist, 0)
    max_exact = N_BUCKETS // 2
    nf = jnp.maximum(n, 1).astype(F32)
    large = max_exact + (jnp.log(nf / max_exact) / math.log(MAX_DISTANCE / max_exact)
                         * (N_BUCKETS - max_exact)).astype(I32)
    large = jnp.minimum(large, N_BUCKETS - 1)
    return jnp.where(n < max_exact, n, large)


def _sort_key(x):
    bits = pltpu.bitcast(x, I32)
    return jnp.where(bits < 0, bits ^ jnp.int32(0x7FFFFFFF), bits)


def _kth_largest_key(count_ge, shape, k):
    c = count_ge(jnp.zeros(shape, I32))
    t = jnp.where(c >= k, jnp.int32(0), jnp.int32(INT_MIN))
    for bit in range(30, -1, -1):
        cand = t + jnp.int32(1 << bit)
        c = count_ge(cand)
        t = jnp.where(c >= k, cand, t)
    return t


KB = 2 * Q_BLOCK


def _attn_prompt_body(rb_ref, q_ref, k_ref, v_ref, qi_ref, ki_ref, wit_ref, o_ref,
                      key_s, msk_s, bias_s, qis_s, m_s, l_s, al_s, acc_s, lg_s, p_s, *, n_sel, n_heads, n_idx_heads):
    b = pl.program_id(0)
    i = pl.program_id(1)
    QB = Q_BLOCK
    n_groups = n_heads // KV_GROUP
    GQ = KV_GROUP * QB
    npair = (i + 2) // 2
    s_io = lax.broadcasted_iota(I32, (QB, QB), 0)
    t_io = lax.broadcasted_iota(I32, (QB, QB), 1)

    @pl.when((b == 0) & (i == 0))
    def _():
        for d in range(2):
            bucket = _rel_bucket(t_io - s_io + d * QB)
            for h in range(n_heads):
                acc = jnp.zeros((QB, QB), F32)
                for bb in range(N_BUCKETS):
                    acc = jnp.where(bucket == bb, rb_ref[bb, h], acc)
                bias_s[d, h] = acc
        for h in range(n_heads):
            bias_s[2, h] = jnp.full((QB, QB), rb_ref[N_BUCKETS - 1, h], F32)

    def causal_masked(j):
        off = jnp.where(j < i, QB, jnp.where(j == i, 0, -QB))
        return s_io > t_io + off

    for h in range(n_idx_heads):
        qis_s[h * QB:(h + 1) * QB, :] = qi_ref[0, :, h * IDX_DIM:(h + 1) * IDX_DIM]
    wt = wit_ref[0] * (n_idx_heads * IDX_DIM) ** -0.5

    def score_pair(jj, carry):
        ks = pl.multiple_of(jj * KB, KB)
        d = _dot_nt(ki_ref[0, pl.ds(ks, KB), :], qis_s[...])
        st = jnp.zeros((KB, QB), F32)
        for h in range(n_idx_heads):
            st = st + jnp.maximum(d[:, h * QB:(h + 1) * QB], 0.0) * wt[h:h + 1, :]
        for r in range(2):
            blk = jnp.where(causal_masked(2 * jj + r), NEG_INF, st[r * QB:(r + 1) * QB])
            key_s[pl.ds(ks + r * QB, QB), :] = _sort_key(blk)
        return carry

    lax.fori_loop(0, npair, score_pair, 0)

    def count(pred):
        def body(jj, acc):
            blk = key_s[pl.ds(pl.multiple_of(jj * KB, KB), KB), :]
            return acc + jnp.sum(jnp.where(pred(blk), 1.0, 0.0).reshape(KB // 32, 4, 8, QB), axis=0)
        acc = lax.fori_loop(0, npair, body, jnp.zeros((4, 8, QB), F32))
        return jnp.sum(acc.reshape(32, QB), axis=0, keepdims=True)

    thr = _kth_largest_key(lambda cand: count(lambda blk: blk >= cand), (1, QB), n_sel)
    need = n_sel - count(lambda blk: blk > thr)
    r_io = lax.broadcasted_iota(I32, (KB, KB), 0)
    c_io = lax.broadcasted_iota(I32, (KB, KB), 1)
    ltri = jnp.where(c_io <= r_io, 1.0, 0.0).astype(BF16)

    def mask_pair(jj, carry):
        ks = pl.multiple_of(jj * KB, KB)
        blk = key_s[pl.ds(ks, KB), :]
        eq = blk == thr
        rank = carry + _dot(ltri, jnp.where(eq, 1.0, 0.0).astype(BF16))
        keep = jnp.where(blk > thr, 1.0, jnp.where(eq & (rank <= need), 1.0, 0.0))
        for r in range(2):
            kr = jnp.where(causal_masked(2 * jj + r), 0.0, keep[r * QB:(r + 1) * QB])
            msk_s[pl.ds(ks + r * QB, QB), :] = jnp.where(kr > 0.0, 0.0, NEG_INF)
        return rank[KB - 1:KB, :]

    lax.fori_loop(0, npair, mask_pair, jnp.zeros((1, QB), F32))

    scale = HEAD_DIM ** -0.5
    m_s[...] = jnp.full(m_s.shape, NEG_INF, F32)
    l_s[...] = jnp.zeros(l_s.shape, F32)
    acc_s[...] = jnp.zeros(acc_s.shape, F32)

    def kv_pair(jj, carry):
        ks = pl.multiple_of(jj * KB, KB)
        mk = msk_s[pl.ds(ks, KB), :]
        mk = jnp.concatenate([mk] * KV_GROUP, axis=1)
        dsel = [jnp.clip(i - 2 * jj - r, 0, 2) for r in range(2)]
        for g in range(n_groups):
            heads = [g * KV_GROUP + r for r in range(KV_GROUP)]
            qp = jnp.concatenate([q_ref[0, :, h * HEAD_DIM:(h + 1) * HEAD_DIM] for h in heads], axis=0)
            kb = k_ref[0, pl.ds(ks, KB), g * HEAD_DIM:(g + 1) * HEAD_DIM]
            lg_s[g] = _dot_nt(kb, qp)
        for g in range(n_groups):
            heads = [g * KV_GROUP + r for r in range(KV_GROUP)]
            bias = jnp.concatenate(
                [jnp.concatenate([bias_s[dsel[r], h] for h in heads], axis=1) for r in range(2)], axis=0)
            lt = lg_s[g] * scale + bias + mk
            m_old = m_s[g]
            m_new = jnp.maximum(m_old, jnp.max(jnp.max(lt.reshape(4, KB // 4, GQ), axis=0), axis=0, keepdims=True))
            m_fin = jnp.where(m_new == NEG_INF, 0.0, m_new)
            alpha = jnp.exp(m_old - m_fin)
            p = jnp.exp(lt - m_fin)
            l_s[g] = alpha * l_s[g] + jnp.sum(jnp.sum(p.reshape(4, KB // 4, GQ), axis=0), axis=0, keepdims=True)
            al_s[g] = alpha
            p_s[g] = p.astype(BF16)
            m_s[g] = m_new
        for g in range(n_groups):
            vb = v_ref[0, pl.ds(ks, KB), g * HEAD_DIM:(g + 1) * HEAD_DIM]
            acc_s[g] = al_s[g] * acc_s[g] + _dot_tn(vb, p_s[g])
        return carry

    lax.fori_loop(0, npair, kv_pair, 0)
    for g in range(n_groups):
        ot = acc_s[g] / l_s[g]
        for r in range(KV_GROUP):
            h = g * KV_GROUP + r
            o_ref[0, :, h * HEAD_DIM:(h + 1) * HEAD_DIM] = ot[:, r * QB:(r + 1) * QB].T.astype(o_ref.dtype)


def _attn_prompt(q, k, v, qi, ki, wit, rel_bias):
    B, S, HD = q.shape
    n_heads = HD // HEAD_DIM
    n_groups = n_heads // KV_GROUP
    KD = k.shape[2]
    n_idx_heads = wit.shape[1]
    n_sel = min(TOPK_MAX, S // 4)
    QB = Q_BLOCK
    assert S % KB == 0
    return pl.pallas_call(
        functools.partial(_attn_prompt_body, n_sel=n_sel, n_heads=n_heads, n_idx_heads=n_idx_heads),
        grid=(B, S // QB),
        in_specs=[pl.BlockSpec(memory_space=pltpu.SMEM),
                  pl.BlockSpec((1, QB, HD), lambda b, i: (b, i, 0)),
                  pl.BlockSpec((1, S, KD), lambda b, i: (b, 0, 0)),
                  pl.BlockSpec((1, S, KD), lambda b, i: (b, 0, 0)),
                  pl.BlockSpec((1, QB, qi.shape[2]), lambda b, i: (b, i, 0)),
                  pl.BlockSpec((1, S, IDX_DIM), lambda b, i: (b, 0, 0)),
                  pl.BlockSpec((1, n_idx_heads, QB), lambda b, i: (b, 0, i))],
        out_specs=pl.BlockSpec((1, QB, HD), lambda b, i: (b, i, 0)),
        out_shape=jax.ShapeDtypeStruct((B, S, HD), BF16),
        scratch_shapes=[pltpu.VMEM((S, QB), I32), pltpu.VMEM((S, QB), F32),
                        pltpu.VMEM((3, n_heads, QB, QB), F32),
                        pltpu.VMEM((n_idx_heads * QB, IDX_DIM), BF16),
                        pltpu.VMEM((n_groups, 1, KV_GROUP * QB), F32),
                        pltpu.VMEM((n_groups, 1, KV_GROUP * QB), F32),
                        pltpu.VMEM((n_groups, 1, KV_GROUP * QB), F32),
                        pltpu.VMEM((n_groups, HEAD_DIM, KV_GROUP * QB), F32),
                        pltpu.VMEM((n_groups, KB, KV_GROUP * QB), F32),
                        pltpu.VMEM((n_groups, KB, KV_GROUP * QB), BF16)],
        compiler_params=_params("arbitrary", "arbitrary"),
        name="attn_prompt",
    )(rel_bias, q, k, v, qi, ki, wit)


SCORE_ROWS = 24


def _smp_scores_body(pt_ref, qi_ref, wcol_ref, kin_ref, *refs, n_pages, n_idx_heads):
    pages = refs[:n_pages]
    out_ref = refs[n_pages]
    q16 = qi_ref[0].astype(BF16)
    wcol = wcol_ref[0] * (n_idx_heads * IDX_DIM) ** -0.5
    rows = []
    for p in range(n_pages):
        d = _dot(q16, pages[p][...].astype(BF16))
        rows.append(jnp.sum(jnp.maximum(d, 0.0) * wcol, axis=0, keepdims=True))
    kin = kin_ref[0].astype(BF16).astype(F32)
    dn = jnp.sum(q16.astype(F32) * kin, axis=1, keepdims=True)
    snew = jnp.sum(jnp.maximum(dn, 0.0) * wcol, axis=0, keepdims=True)
    lane = lax.broadcasted_iota(I32, (1, PAGE), 1)
    rows.append(jnp.where(lane == 0, snew, NEG_INF))
    rows.append(jnp.full((SCORE_ROWS - n_pages - 1, PAGE), NEG_INF, F32))
    out_ref[0] = jnp.concatenate(rows, axis=0)


def _smp_scores(page_table, qi3, wcol, kin3, cache_idx_kt, layer):
    Bs, n_pages = page_table.shape
    n_idx_heads = qi3.shape[1]
    page_spec = lambda p: pl.BlockSpec((None, None, IDX_DIM, PAGE), lambda b, pt: (layer, pt[b, p], 0, 0))
    grid_spec = pltpu.PrefetchScalarGridSpec(
        num_scalar_prefetch=1,
        grid=(Bs,),
        in_specs=[pl.BlockSpec((1, n_idx_heads, IDX_DIM), lambda b, pt: (b, 0, 0)),
                  pl.BlockSpec((1, n_idx_heads, 1), lambda b, pt: (b, 0, 0)),
                  pl.BlockSpec((1, 1, IDX_DIM), lambda b, pt: (b, 0, 0))]
        + [page_spec(p) for p in range(n_pages)],
        out_specs=pl.BlockSpec((1, SCORE_ROWS, PAGE), lambda b, pt: (b, 0, 0)),
    )
    return pl.pallas_call(
        functools.partial(_smp_scores_body, n_pages=n_pages, n_idx_heads=n_idx_heads),
        grid_spec=grid_spec,
        out_shape=jax.ShapeDtypeStruct((Bs, SCORE_ROWS, PAGE), F32),
        compiler_params=_params("arbitrary"),
        name="smp_scores",
    )(page_table, qi3, wcol, kin3, *([cache_idx_kt] * n_pages))


def _smp_select_body(sc_ref, o_ref, key_s, *, n_sel):
    Bs, W = sc_ref.shape
    key_s[...] = _sort_key(sc_ref[...])

    def count_ge(cand):
        return jnp.sum(jnp.where(key_s[...] >= cand, 1.0, 0.0), axis=1, keepdims=True)

    thr = _kth_largest_key(count_ge, (Bs, 1), n_sel)
    n_gt = jnp.sum(jnp.where(key_s[...] > thr, 1.0, 0.0), axis=1, keepdims=True)
    need = n_sel - n_gt
    r_io = lax.broadcasted_iota(I32, (LANES, LANES), 0)
    c_io = lax.broadcasted_iota(I32, (LANES, LANES), 1)
    utri = jnp.where(r_io <= c_io, 1.0, 0.0).astype(BF16)
    carry = jnp.zeros((Bs, 1), F32)
    for j in range(W // LANES):
        blk = key_s[:, j * LANES:(j + 1) * LANES]
        eq = blk == thr
        rank = carry + _dot(jnp.where(eq, 1.0, 0.0).astype(BF16), utri)
        keep = jnp.where(blk > thr, 1.0, jnp.where(eq & (rank <= need), 1.0, 0.0))
        o_ref[:, j * LANES:(j + 1) * LANES] = keep
        carry = rank[:, LANES - 1:LANES]


def _smp_select(scores, n_sel):
    Bs, W = scores.shape
    return pl.pallas_call(
        functools.partial(_smp_select_body, n_sel=n_sel),
        grid=(1,),
        in_specs=[pl.BlockSpec((Bs, W), lambda i: (0, 0))],
        out_specs=pl.BlockSpec((Bs, W), lambda i: (0, 0)),
        out_shape=jax.ShapeDtypeStruct((Bs, W), F32),
        scratch_shapes=[pltpu.VMEM((Bs, W), I32)],
        compiler_params=_params("arbitrary"),
        name="smp_select",
    )(scores)


def _smp_attn_body(pt_ref, rbt_ref, q_ref, kn_ref, vn_ref, sel_ref, *refs, n_pages, n_heads, n_kv):
    kpages = refs[:n_pages]
    vpages = refs[n_pages:2 * n_pages]
    o_ref = refs[2 * n_pages]
    lg_s, bias_s, expand_s = refs[2 * n_pages + 1:]
    R = PAGE * n_kv
    row = lax.broadcasted_iota(I32, (1, R), 1)
    head = lax.broadcasted_iota(I32, (n_heads, 1), 0)
    own_group = (row % n_kv) == (head // KV_GROUP)

    @pl.when(pl.program_id(0) == 0)
    def _():
        k_io = lax.broadcasted_iota(I32, (PAGE, R), 0)
        r_io = lax.broadcasted_iota(I32, (PAGE, R), 1)
        expand_s[...] = jnp.where(r_io // n_kv == k_io, 1.0, 0.0).astype(BF16)
        bucket = _rel_bucket(PAGE - row // n_kv)
        acc = jnp.zeros((n_heads, R), F32)
        for bb in range(N_BUCKETS):
            acc = jnp.where(bucket == bb, rbt_ref[:, bb:bb + 1], acc)
        bias_s[...] = acc

    q16 = q_ref[0]
    scale = HEAD_DIM ** -0.5
    far_bias = rbt_ref[:, N_BUCKETS - 1:N_BUCKETS]
    sel = sel_ref[0]
    selx = _dot(sel.astype(BF16), expand_s[...])

    m = jnp.full((n_heads, LANES), NEG_INF, F32)
    for p in range(n_pages):
        lt = _dot_nt(q16, kpages[p][...].astype(BF16)) * scale
        lt = lt + (bias_s[...] if p == n_pages - 1 else far_bias)
        lt = jnp.where(own_group, jnp.where(selx[p:p + 1, :] > 0.5, lt, NEG_INF), NEG_INF)
        lg_s[p] = lt
        for c in range(R // LANES):
            m = jnp.maximum(m, lt[:, c * LANES:(c + 1) * LANES])
    m = jnp.max(m, axis=1, keepdims=True)
    qf = q16.astype(F32)
    ln = jnp.sum(qf * kn_ref[0].astype(BF16).astype(F32), axis=1, keepdims=True) * scale + rbt_ref[:, 0:1]
    ln = jnp.where(sel[n_pages:n_pages + 1, 0:1] > 0.5, ln, NEG_INF)
    m = jnp.maximum(m, ln)

    pn = jnp.exp(ln - m)
    l = pn
    acc = pn.astype(BF16).astype(F32) * vn_ref[0].astype(BF16).astype(F32)
    for p in range(n_pages):
        pr = jnp.exp(lg_s[p] - m)
        l = l + jnp.sum(pr, axis=1, keepdims=True)
        acc = acc + _dot(pr.astype(BF16), vpages[p][...].astype(BF16))
    o_ref[0] = (acc / l).astype(o_ref.dtype)


def _smp_attn(page_table, rbt_pad, q3, kn_rep, vn_rep, sel3, cache_k4, cache_v4, layer, n_kv):
    Bs, n_pages = page_table.shape
    n_heads = q3.shape[1]
    R = PAGE * n_kv
    page_spec = lambda p: pl.BlockSpec((None, None, R, HEAD_DIM), lambda b, pt: (layer, pt[b, p], 0, 0))
    head_spec = lambda: pl.BlockSpec((1, n_heads, HEAD_DIM), lambda b, pt: (b, 0, 0))
    grid_spec = pltpu.PrefetchScalarGridSpec(
        num_scalar_prefetch=1,
        grid=(Bs,),
        in_specs=[pl.BlockSpec((n_heads, LANES), lambda b, pt: (0, 0)),
                  head_spec(), head_spec(), head_spec(),
                  pl.BlockSpec((1, SCORE_ROWS, PAGE), lambda b, pt: (b, 0, 0))]
        + [page_spec(p) for p in range(n_pages)] * 2,
        out_specs=head_spec(),
        scratch_shapes=[pltpu.VMEM((n_pages, n_heads, R), F32), pltpu.VMEM((n_heads, R), F32),
                        pltpu.VMEM((PAGE, R), BF16)],
    )
    return pl.pallas_call(
        functools.partial(_smp_attn_body, n_pages=n_pages, n_heads=n_heads, n_kv=n_kv),
        grid_spec=grid_spec,
        out_shape=jax.ShapeDtypeStruct((Bs, n_heads, HEAD_DIM), BF16),
        compiler_params=_params("arbitrary"),
        name="smp_attn",
    )(page_table, rbt_pad, q3, kn_rep, vn_rep, sel3, *([cache_k4] * n_pages), *([cache_v4] * n_pages))


def _merge_body(yr_ref, ya_ref, wa_ref, wb_ref, ga0_ref, ga1_ref, gb0_ref, gb1_ref, o_ref, *, off):
    ga = jnp.concatenate([ga0_ref[:, off:], ga1_ref[:, :off]], axis=1)
    gb = jnp.concatenate([gb0_ref[:, off:], gb1_ref[:, :off]], axis=1)
    ba = _dot(yr_ref[...], wa_ref[...].astype(BF16))
    bb = _dot(ya_ref[...], wb_ref[...].astype(BF16))
    o_ref[...] = (jax.nn.sigmoid(ga) * ba + jax.nn.sigmoid(gb) * bb).astype(o_ref.dtype)


def _merge(yr, ya, wa, wb, ztail, off, l, *, tm, tn):
    M, K = yr.shape
    D = wa.shape[2]
    nj = D // tn
    assert 0 < off < tn and ztail.shape[1] >= (2 * nj + 1) * tn
    gate = lambda blk: pl.BlockSpec((tm, tn), lambda j, i: (i, blk + j))
    return pl.pallas_call(
        functools.partial(_merge_body, off=off),
        grid=(nj, M // tm),
        in_specs=[pl.BlockSpec((tm, K), lambda j, i: (i, 0)),
                  pl.BlockSpec((tm, K), lambda j, i: (i, 0)),
                  pl.BlockSpec((None, K, tn), lambda j, i: (l, 0, j)),
                  pl.BlockSpec((None, K, tn), lambda j, i: (l, 0, j)),
                  gate(0), gate(1), gate(nj), gate(nj + 1)],
        out_specs=pl.BlockSpec((tm, tn), lambda j, i: (i, j)),
        out_shape=jax.ShapeDtypeStruct((M, D), BF16),
        compiler_params=_params("arbitrary", "arbitrary"),
        name="merge",
    )(yr, ya, wa, wb, ztail, ztail, ztail, ztail)


def _proj_res_body(a_ref, w_ref, x_ref, g_ref, o_ref):
    o_ref[0] = x_ref[0] + g_ref[0] * _dot(a_ref[...], w_ref[...].astype(BF16))


def _proj_res(a, w, l, x, mod, g_chunk, *, tm, tn):
    B, T, D = x.shape
    K = a.shape[1]
    tpb = T // tm
    nj = D // tn
    per_tok = mod.shape[1] == T
    tg = tm if per_tok else 1
    gmap = (lambda j, i: (i // tpb, i % tpb, g_chunk * nj + j)) if per_tok else \
        (lambda j, i: (i // tpb, 0, g_chunk * nj + j))
    return pl.pallas_call(
        _proj_res_body,
        grid=(nj, B * tpb),
        in_specs=[pl.BlockSpec((tm, K), lambda j, i: (i, 0)),
                  pl.BlockSpec((None, K, tn), lambda j, i: (l, 0, j)),
                  pl.BlockSpec((1, tm, tn), lambda j, i: (i // tpb, i % tpb, j)),
                  pl.BlockSpec((1, tg, tn), gmap)],
        out_specs=pl.BlockSpec((1, tm, tn), lambda j, i: (i // tpb, i % tpb, j)),
        out_shape=jax.ShapeDtypeStruct((B, T, D), F32),
        compiler_params=_params("arbitrary", "arbitrary"),
        name="proj_res",
    )(a, w, x, mod)


def _router_body(h_ref, r_ref, o_ref, *, n_experts):
    lg = _dot(h_ref[...].astype(BF16), r_ref[...].astype(BF16))
    lane = lax.broadcasted_iota(I32, lg.shape, 1)
    lg = jnp.where(lane < n_experts, lg, NEG_INF)
    m1 = jnp.max(lg, axis=1, keepdims=True)
    i1 = jnp.min(jnp.where(lg == m1, lane, LANES), axis=1, keepdims=True)
    rest = jnp.where(lane == i1, NEG_INF, lg)
    m2 = jnp.max(rest, axis=1, keepdims=True)
    i2 = jnp.min(jnp.where(rest == m2, lane, LANES), axis=1, keepdims=True)
    e2 = jnp.exp(m2 - m1)
    den = 1.0 + e2
    o_ref[...] = jnp.where(lane == 0, i1.astype(F32),
                           jnp.where(lane == 1, i2.astype(F32),
                                     jnp.where(lane == 2, 1.0 / den, jnp.where(lane == 3, e2 / den, 0.0))))


def _router(h, router_pad, *, tm, n_experts):
    M, D = h.shape
    return pl.pallas_call(
        functools.partial(_router_body, n_experts=n_experts),
        grid=(M // tm,),
        in_specs=[pl.BlockSpec((tm, D), lambda i: (i, 0)),
                  pl.BlockSpec((D, LANES), lambda i: (0, 0))],
        out_specs=pl.BlockSpec((tm, LANES), lambda i: (i, 0)),
        out_shape=jax.ShapeDtypeStruct((M, LANES), F32),
        compiler_params=_params("arbitrary"),
        name="router",
    )(h, router_pad)


def _ffn_body(h_ref, w1_ref, w3_ref, w2_ref, x_ref, g2_ref, o_ref):
    f = pl.program_id(1)

    @pl.when(f == 0)
    def _():
        o_ref[...] = jnp.zeros(o_ref.shape, F32)

    h = h_ref[...]
    a = _dot(h, w1_ref[...].astype(BF16))
    b = _dot(h, w3_ref[...].astype(BF16))
    act = a * jax.nn.sigmoid(a) * b
    o_ref[0] += _dot(act.astype(BF16), w2_ref[...].astype(BF16))

    @pl.when(f == pl.num_programs(1) - 1)
    def _():
        o_ref[0] = x_ref[0] + g2_ref[0] * o_ref[0]


def _ffn(h, w1, w3, w2, layer_idx, x, mod, g_chunk, *, tm, tf):
    B, T, D = x.shape
    tpb = T // tm
    F = w1.shape[2]
    per_tok = mod.shape[1] == T
    tg = tm if per_tok else 1
    gmap = (lambda i, f: (i // tpb, i % tpb, g_chunk)) if per_tok else (lambda i, f: (i // tpb, 0, g_chunk))
    return pl.pallas_call(
        _ffn_body,
        grid=(B * tpb, F // tf),
        in_specs=[pl.BlockSpec((tm, D), lambda i, f: (i, 0)),
                  pl.BlockSpec((None, D, tf), lambda i, f: (layer_idx, 0, f)),
                  pl.BlockSpec((None, D, tf), lambda i, f: (layer_idx, 0, f)),
                  pl.BlockSpec((None, tf, D), lambda i, f: (layer_idx, f, 0)),
                  pl.BlockSpec((1, tm, D), lambda i, f: (i // tpb, i % tpb, 0), pipeline_mode=pl.Buffered(1)),
                  pl.BlockSpec((1, tg, D), gmap)],
        out_specs=pl.BlockSpec((1, tm, D), lambda i, f: (i // tpb, i % tpb, 0)),
        out_shape=jax.ShapeDtypeStruct((B, T, D), F32),
        compiler_params=_params("arbitrary", "arbitrary"),
        name="ffn_dense",
    )(h, w1, w3, w2, x, mod)


def _moe_chunk(tm, n_f):
    return -(-(-(-tm // n_f)) // 8) * 8


def _moe_body(te_ref, dst_ref, nused_ref, h_hbm, w1_ref, w3_ref, w2_ref, rw_ref, out_hbm,
              xg_s, xb_s, acc_s, os_s, sem_in, sem_out, *, n_tok, n_f, n_tiles):
    i = pl.program_id(0)
    f = pl.program_id(1)
    tm = xb_s.shape[0]
    chunk = _moe_chunk(tm, n_f)
    ts = n_f * chunk
    slot = i % 2

    def gather(base, r, s):
        d = dst_ref[base + r]
        tok = jnp.maximum(d - jnp.where(d >= n_tok, n_tok, 0), 0)
        pltpu.make_async_copy(h_hbm.at[pl.ds(tok, 1), :], xg_s.at[s, pl.ds(r, 1), :], sem_in.at[s]).start()

    def scatter(base, r):
        d = dst_ref[base + r]
        row = jnp.where(d >= 0, d, 2 * n_tok + r)
        pltpu.make_async_copy(os_s.at[pl.ds(r, 1), :], out_hbm.at[pl.ds(row, 1), :], sem_out).start()

    def wait_gather(s):
        pltpu.make_async_copy(h_hbm.at[pl.ds(0, ts), :], xg_s.at[s, pl.ds(0, ts), :], sem_in.at[s]).wait()

    def wait_scatter():
        pltpu.make_async_copy(os_s.at[pl.ds(0, ts), :], out_hbm.at[pl.ds(0, ts), :], sem_out).wait()

    @pl.when((i == 0) & (f == 0))
    def _():
        os_s[...] = jnp.zeros(os_s.shape, F32)

        def start(r, c):
            gather(ts, r, 0)
            return c

        lax.fori_loop(0, ts, start, 0)

    @pl.when(f == 0)
    def _():
        wait_gather(slot)
        xb_s[...] = xg_s[slot, 0:tm, :].astype(BF16)
        acc_s[...] = jnp.zeros(acc_s.shape, F32)

    def move_rows():
        for k in range(chunk):
            r = f * chunk + k
            gather((i + 2) * ts, r, 1 - slot)
            scatter(i * ts, r)

    @pl.when(i < nused_ref[0])
    def _():
        move_rows()
        x = xb_s[...]
        a = _dot(x, w1_ref[...].astype(BF16))
        b = _dot(x, w3_ref[...].astype(BF16))
        act = a * jax.nn.sigmoid(a) * b
        acc_s[...] += _dot(act.astype(BF16), w2_ref[...].astype(BF16))

    @pl.when(i >= nused_ref[0])
    def _():
        move_rows()

    @pl.when(f == n_f - 1)
    def _():
        wait_scatter()

        @pl.when(i < nused_ref[0])
        def _():
            os_s[0:tm, :] = acc_s[...] * rw_ref[...]

        @pl.when(i == n_tiles - 1)
        def _():
            def start(r, c):
                scatter((i + 1) * ts, r)
                return c

            lax.fori_loop(0, ts, start, 0)
            wait_scatter()
            wait_gather(1 - slot)


def _moe_sparse(h_all, top_i, top_w, w1, w3, w2, layer_idx, *, tm, tf):
    M, D = h_all.shape
    nE, _, F = w1.shape[1:]
    n_asg = 2 * M
    n_tiles = -(-(n_asg + nE * (tm - 1)) // tm)
    m_pad = n_tiles * tm
    nf = F // tf
    ts = nf * _moe_chunk(tm, nf)
    e_flat = top_i.T.reshape(n_asg)
    onehot = (e_flat[:, None] == jnp.arange(nE, dtype=I32)[None, :]).astype(I32)
    cum = jnp.cumsum(onehot, axis=0)
    counts = cum[-1]
    rank = jnp.take_along_axis(cum, e_flat[:, None], axis=1)[:, 0] - 1
    padded = -(-counts // tm) * tm
    ends = jnp.cumsum(padded)
    starts = ends - padded
    pos = starts[e_flat] + rank
    dst = jnp.full((m_pad,), -1, I32).at[pos].set(jnp.arange(n_asg, dtype=I32))
    dst = jnp.pad(dst.reshape(n_tiles, tm), ((1, 1), (0, ts - tm)), constant_values=-1).reshape(-1)
    roww = jnp.zeros((m_pad,), F32).at[pos].set(top_w.T.reshape(n_asg))
    tile_start = jnp.arange(n_tiles, dtype=I32) * tm
    n_used = (ends[-1] // tm).astype(I32)
    tile_e = jnp.minimum(jnp.searchsorted(ends, tile_start, side="right").astype(I32), nE - 1)
    tile_e = jnp.where(tile_start < ends[-1], tile_e, tile_e[jnp.maximum(n_used - 1, 0)])

    def fidx(i, f, nused):
        return jnp.where(i < nused[0], f, nf - 1)

    grid_spec = pltpu.PrefetchScalarGridSpec(
        num_scalar_prefetch=3,
        grid=(n_tiles, nf),
        in_specs=[pl.BlockSpec(memory_space=pl.ANY),
                  pl.BlockSpec((None, None, D, tf), lambda i, f, te, ds_, nu: (layer_idx, te[i], 0, fidx(i, f, nu))),
                  pl.BlockSpec((None, None, D, tf), lambda i, f, te, ds_, nu: (layer_idx, te[i], 0, fidx(i, f, nu))),
                  pl.BlockSpec((None, None, tf, D), lambda i, f, te, ds_, nu: (layer_idx, te[i], fidx(i, f, nu), 0)),
                  pl.BlockSpec((tm, 1), lambda i, f, te, ds_, nu: (i, 0))],
        out_specs=pl.BlockSpec(memory_space=pl.ANY),
        scratch_shapes=[pltpu.VMEM((2, ts, D), F32), pltpu.VMEM((tm, D), BF16), pltpu.VMEM((tm, D), F32),
                        pltpu.VMEM((ts, D), F32), pltpu.SemaphoreType.DMA((2,)), pltpu.SemaphoreType.DMA(())],
    )
    return pl.pallas_call(
        functools.partial(_moe_body, n_tok=M, n_f=nf, n_tiles=n_tiles),
        grid_spec=grid_spec,
        out_shape=jax.ShapeDtypeStruct((n_asg + ts, D), F32),
        compiler_params=_params("arbitrary", "arbitrary"),
        name="moe_sparse",
    )(tile_e, dst, n_used.reshape(1), h_all, w1, w3, w2, roww.reshape(m_pad, 1))


def _moe_combine_body(x_ref, g2_ref, a_ref, b_ref, o_ref):
    o_ref[0] = x_ref[0] + g2_ref[0] * (a_ref[...] + b_ref[...])


def _moe_combine(x, mod, g_chunk, y2, row0, n_tok, *, tt):
    B, T, D = x.shape
    per_tok = mod.shape[1] == T
    tg = tt if per_tok else 1
    tpb = T // tt
    assert row0 % tt == 0 and n_tok % tt == 0
    rb0, rb1 = row0 // tt, (n_tok + row0) // tt
    return pl.pallas_call(
        _moe_combine_body,
        grid=(B, tpb),
        in_specs=[pl.BlockSpec((1, tt, D), lambda b, t: (b, t, 0)),
                  pl.BlockSpec((1, tg, D), lambda b, t: (b, t if per_tok else 0, g_chunk)),
                  pl.BlockSpec((tt, D), lambda b, t: (rb0 + b * tpb + t, 0)),
                  pl.BlockSpec((tt, D), lambda b, t: (rb1 + b * tpb + t, 0))],
        out_specs=pl.BlockSpec((1, tt, D), lambda b, t: (b, t, 0)),
        out_shape=jax.ShapeDtypeStruct((B, T, D), F32),
        compiler_params=_params("arbitrary", "arbitrary"),
        name="moe_combine",
    )(x, mod, y2, y2)


def _tile(n, prefs):
    for p in prefs:
        if n % p == 0:
            return p
    return n


def _mixer_half(l, x, mod, P, attend, rglru, kv_prev, *, tt, tm):
    B, T, D = x.shape
    M = B * T
    h = _norm_mod(x, P["norm_g"][l, 0:1], mod, 1, 0, BF16, tt=tt).reshape(M, D)
    w_in = P["w_in_t"]
    (zug,) = _mm(h, w_in, l, 0, 2 * D, [F32], tm=tm, tn=1024, w_nk=True)
    (q,) = _mm(h, w_in, l, 2 * D, D, [BF16], tm=tm, tn=1024, w_nk=True)
    KD = P["n_kv_cols"] // 2
    depth = w_in.shape[0]
    tnk = _tile(KD, (1024,))
    k_all, k16 = _mm_stacked(h, w_in, l, depth, P["col_kv"], KD, kv_prev and kv_prev[0], tm=tm, tn=tnk, w_nk=True)
    v_all, v16 = _mm_stacked(h, w_in, l, depth, P["col_kv"] + KD, KD, kv_prev and kv_prev[1], tm=tm, tn=tnk,
                             w_nk=True)
    (qi,) = _mm(h, w_in, l, P["col_qi"], P["n_qi_cols"], [BF16], tm=tm, tn=1024, w_nk=True)
    tnt = 512
    n_tail = -(-(P["n_kiwi"] + 2 * D) // tnt) * tnt
    (ztail,) = _mm(h, w_in, l, P["col_ki"], n_tail, [F32], tm=tm, tn=tnt, w_nk=True)
    kiwi = ztail[:, :LANES]
    y_rnn, h_last, new_hist = rglru(l, zug)
    y_att = attend(l, q, (k_all, v_all), (k16, v16), qi, kiwi)
    merged = _merge(y_rnn.reshape(M, D), y_att.reshape(M, D), P["w_branch_a"], P["w_branch_b"], ztail,
                    P["n_kiwi"], l, tm=_tile(M, (512,)), tn=tnt)
    x = _proj_res(merged, P["w_out"], l, x, mod, 2, tm=tm, tn=1024)
    return x, h_last, new_hist, (k_all, v_all), kiwi


def kernel(x_prompt, x_sample, c_prompt, c_sample, cache_k, cache_v, cache_idx_k, state_h, state_conv,
           page_table, w_ada, b_ada, norm_g, w_in, conv_w, conv_b, lru_wa, lru_ba, lru_wx, lru_bx,
           lru_lambda, w_branch_a, w_branch_b, w_out, rel_bias, ffn_w1, ffn_w3, ffn_w2,
           moe_router, moe_w1, moe_w3, moe_w2, final_g):
    Bp, S, D = x_prompt.shape
    Bs = x_sample.shape[0]
    Mp = Bp * S
    depth = w_in.shape[0]
    n_heads = rel_bias.shape[1]
    n_phys, _, n_kv = cache_k.shape[1:4]
    KD = n_kv * HEAD_DIM
    n_idx_heads = (w_in.shape[2] - (5 * D + 2 * KD + IDX_DIM)) // (IDX_DIM + 1)
    n_pages = page_table.shape[1]
    past = n_pages * PAGE
    col_qi = 3 * D + 2 * KD
    col_ki = col_qi + n_idx_heads * IDX_DIM
    n_kiwi = IDX_DIM + n_idx_heads
    n_experts = moe_router.shape[2]

    P = dict(
        col_kv=3 * D, n_kv_cols=2 * KD, col_qi=col_qi, n_qi_cols=n_idx_heads * IDX_DIM,
        col_ki=col_ki, n_kiwi=n_kiwi, norm_g=norm_g, w_in_t=jnp.swapaxes(w_in, 1, 2), conv_w=conv_w, lru_wa=lru_wa, lru_wx=lru_wx,
        conv_b3=conv_b[:, None, :], lru_ba3=lru_ba[:, None, :], lru_bx3=lru_bx[:, None, :],
        lru_lambda3=lru_lambda[:, None, :],
        w_branch_a=w_branch_a, w_branch_b=w_branch_b, w_out=w_out,
    )
    router_pad = jnp.pad(moe_router, ((0, 0), (0, 0), (0, LANES - n_experts)))
    rbt_pad = jnp.pad(rel_bias.T, ((0, 0), (0, LANES - rel_bias.shape[0])))
    cache_idx_kt = jnp.swapaxes(cache_idx_k, 2, 3)
    cache_k4 = cache_k.reshape(depth, n_phys, PAGE * n_kv, HEAD_DIM)
    cache_v4 = cache_v.reshape(depth, n_phys, PAGE * n_kv, HEAD_DIM)

    n_c = Bp + Bs
    n_c_pad = -(-n_c // 8) * 8
    c_all = jnp.concatenate([c_prompt, c_sample, jnp.zeros((n_c_pad - n_c, D), F32)], axis=0)
    b_ada3 = b_ada[:, None, :]

    def attend_prompt(l, q, kv32, kv16, qi, kiwi):
        ki = kiwi[:, :IDX_DIM].astype(BF16).reshape(Bp, S, IDX_DIM)
        wit = kiwi[:, IDX_DIM:n_kiwi].reshape(Bp, S, n_idx_heads).transpose(0, 2, 1)
        return _attn_prompt(q.reshape(Bp, S, D), kv16[0].reshape(Bp, S, KD), kv16[1].reshape(Bp, S, KD),
                            qi.reshape(Bp, S, -1), ki, wit, rel_bias)

    def attend_sample(l, q, kv32, kv16, qi, kiwi):
        qi3 = qi.reshape(Bs, n_idx_heads, IDX_DIM)
        wcol = kiwi[:, IDX_DIM:n_kiwi].reshape(Bs, n_idx_heads, 1)
        kin3 = kiwi[:, :IDX_DIM].reshape(Bs, 1, IDX_DIM)
        scores = _smp_scores(page_table, qi3, wcol, kin3, cache_idx_kt, l)
        n_sel = min(TOPK_MAX, (past + 1) // 4)
        sel3 = _smp_select(scores.reshape(Bs, SCORE_ROWS * PAGE), n_sel).reshape(Bs, SCORE_ROWS, PAGE)
        q3 = q.reshape(Bs, n_heads, HEAD_DIM)
        kn_rep = jnp.repeat(kv32[0][l].reshape(Bs, n_kv, HEAD_DIM), KV_GROUP, axis=1)
        vn_rep = jnp.repeat(kv32[1][l].reshape(Bs, n_kv, HEAD_DIM), KV_GROUP, axis=1)
        return _smp_attn(page_table, rbt_pad, q3, kn_rep, vn_rep, sel3, cache_k4, cache_v4, l, n_kv)

    xp, xs = x_prompt, x_sample.reshape(1, Bs, D)
    outs_p, outs_s = [], []
    kvp = kvs = None
    for l in range(depth):
        (mod,) = _mm(c_all, w_ada, l, 0, 6 * D, [F32], tm=n_c_pad, tn=1024, bias=b_ada3, silu_in=True)
        mod_p = mod[:Bp].reshape(Bp, 1, 6 * D)
        mod_s = mod[Bp:n_c].reshape(1, Bs, 6 * D)
        hist_t = state_conv[l].transpose(1, 0, 2)

        xp, hp, cp, kvp, kiwip = _mixer_half(
            l, xp, mod_p, P, attend_prompt, lambda l_, zug: _rglru_prompt(zug.reshape(Bp, S, 2 * D), P, l_, tc=256),
            kvp, tt=512, tm=1024)

        def rglru_s(l_, zug, hist_t=hist_t):
            y, h = _rglru_sample(zug, hist_t, state_h[l_], P, l_)
            new_hist = jnp.concatenate([state_conv[l_][:, 1:], zug[:, None, :D]], axis=1)
            return y, h, new_hist

        xs, hs, cs, kvs, kiwis = _mixer_half(l, xs, mod_s, P, attend_sample, rglru_s, kvs, tt=Bs, tm=Bs)

        g2p, g2s = P["norm_g"][l, 1:2], P["norm_g"][l, 1:2]
        if l % 2 == 0:
            h2p = _norm_mod(xp, g2p, mod_p, 4, 3, BF16, tt=512).reshape(Mp, D)
            h2s = _norm_mod(xs, g2s, mod_s, 4, 3, BF16, tt=Bs).reshape(Bs, D)
            xp = _ffn(h2p, ffn_w1, ffn_w3, ffn_w2, l // 2, xp, mod_p, 5, tm=1024, tf=256)
            xs = _ffn(h2s, ffn_w1, ffn_w3, ffn_w2, l // 2, xs, mod_s, 5, tm=Bs, tf=512)
        else:
            n_tok = Mp + Bs
            h_all = _norm_mod_rows(xp, g2p, mod_p, 4, 3, n_tok, 0, None, tt=512)
            h_all = _norm_mod_rows(xs, g2s, mod_s, 4, 3, n_tok, Mp, h_all, tt=Bs)
            route = _router(h_all, router_pad[l // 2], tm=_tile(n_tok, (1024, 640, 512, 256, 128)),
                            n_experts=n_experts)
            y2 = _moe_sparse(h_all, route[:, 0:2].astype(I32), route[:, 2:4], moe_w1, moe_w3, moe_w2, l // 2,
                             tm=512, tf=512)
            xp = _moe_combine(xp, mod_p, 5, y2, 0, n_tok, tt=Bs)
            xs = _moe_combine(xs, mod_s, 5, y2, Mp, n_tok, tt=Bs)

        outs_p.append((kiwip[:, :IDX_DIM].reshape(Bp, S, IDX_DIM), hp.reshape(Bp, D), cp))
        outs_s.append((kiwis[:, :IDX_DIM].reshape(Bs, 1, IDX_DIM), hs, cs))
    y_prompt = _final_norm(xp, final_g[None, :], tt=512)
    y_sample = _final_norm(xs, final_g[None, :], tt=Bs).reshape(Bs, 1, D)
    stack = lambda outs, i: jnp.stack([o[i] for o in outs])
    kv_shape_p = (depth, Bp, S, n_kv, HEAD_DIM)
    kv_shape_s = (depth, Bs, 1, n_kv, HEAD_DIM)
    return (y_prompt, y_sample,
            kvp[0].reshape(kv_shape_p), kvp[1].reshape(kv_shape_p),
            stack(outs_p, 0), stack(outs_p, 1), stack(outs_p, 2),
            kvs[0].reshape(kv_shape_s), kvs[1].reshape(kv_shape_s),
            stack(outs_s, 0), stack(outs_s, 1), stack(outs_s, 2))
```

```python
import functools
import math

import jax
import jax.numpy as jnp
from jax import lax
from jax.experimental import pallas as pl
from jax.experimental.pallas import tpu as pltpu

F32 = jnp.float32
BF16 = jnp.bfloat16
I32 = jnp.int32

VMEM_LIMIT_BYTES = 56 * 1024 * 1024
LANES = 128

EPS = 1e-6
LRU_C = 8.0
RNN_BLOCK = 128
CONV_W = 4
HEAD_DIM = 128
KV_GROUP = 2
IDX_DIM = 64
TOPK_MAX = 256
Q_BLOCK = 128
N_BUCKETS = 32
MAX_DISTANCE = 128
PAGE = 128
NEG_INF = float("-inf")
INT_MIN = -(2 ** 31)


def _params(*sem):
    return pltpu.CompilerParams(dimension_semantics=sem, vmem_limit_bytes=VMEM_LIMIT_BYTES)


def _dot(a, b):
    return jnp.dot(a, b, preferred_element_type=F32)


def _dot_nt(a, b):
    return lax.dot_general(a, b, (((1,), (1,)), ((), ())), preferred_element_type=F32)


def _dot_tn(a, b):
    return lax.dot_general(a, b, (((0,), (0,)), ((), ())), preferred_element_type=F32)


def _mm_body(x_ref, w_ref, *refs, silu_in, has_bias, w_nk):
    x = x_ref[...]
    if silu_in:
        x = x.astype(F32)
        x = x * jax.nn.sigmoid(x)
    acc = (_dot_nt if w_nk else _dot)(x.astype(BF16), w_ref[...].astype(BF16))
    outs = refs
    if has_bias:
        acc = acc + refs[0][...]
        outs = refs[1:]
    for o in outs:
        o[...] = acc.astype(o.dtype)


def _w_spec(K, tn, layer, cb0, w_nk):
    if w_nk:
        return pl.BlockSpec((None, tn, K), lambda j, i: (layer, cb0 + j, 0))
    return pl.BlockSpec((None, K, tn), lambda j, i: (layer, 0, cb0 + j))


def _mm(x, w, layer, col0, ncols, out_dtypes, *, tm, tn, bias=None, silu_in=False, w_nk=False):
    M, K = x.shape
    assert M % tm == 0 and ncols % tn == 0 and col0 % tn == 0
    cb0 = col0 // tn
    in_specs = [pl.BlockSpec((tm, K), lambda j, i: (i, 0)), _w_spec(K, tn, layer, cb0, w_nk)]
    args = [x, w]
    if bias is not None:
        in_specs.append(pl.BlockSpec((None, 1, tn), lambda j, i: (layer, 0, cb0 + j)))
        args.append(bias)
    outs = pl.pallas_call(
        functools.partial(_mm_body, silu_in=silu_in, has_bias=bias is not None, w_nk=w_nk),
        grid=(ncols // tn, M // tm),
        in_specs=in_specs,
        out_specs=[pl.BlockSpec((tm, tn), lambda j, i: (i, j)) for _ in out_dtypes],
        out_shape=[jax.ShapeDtypeStruct((M, ncols), dt) for dt in out_dtypes],
        compiler_params=_params("arbitrary", "arbitrary"),
        name="mm",
    )(*args)
    return outs


def _mm_stacked_body(x_ref, w_ref, *refs, w_nk):
    o32_ref, o16_ref = refs[-2:]
    acc = (_dot_nt if w_nk else _dot)(x_ref[...], w_ref[...].astype(BF16))
    o32_ref[...] = acc
    o16_ref[...] = acc.astype(BF16)


def _mm_stacked(x, w, layer, depth, col0, ncols, prev, *, tm, tn, w_nk=False):
    M, K = x.shape
    assert M % tm == 0 and ncols % tn == 0 and col0 % tn == 0
    cb0 = col0 // tn
    in_specs = [pl.BlockSpec((tm, K), lambda j, i: (i, 0)), _w_spec(K, tn, layer, cb0, w_nk)]
    args = [x, w]
    aliases = {}
    if prev is not None:
        in_specs.append(pl.BlockSpec(memory_space=pl.ANY))
        args.append(prev)
        aliases = {2: 0}
    return pl.pallas_call(
        functools.partial(_mm_stacked_body, w_nk=w_nk),
        grid=(ncols // tn, M // tm),
        in_specs=in_specs,
        out_specs=[pl.BlockSpec((None, tm, tn), lambda j, i: (layer, i, j)),
                   pl.BlockSpec((tm, tn), lambda j, i: (i, j))],
        out_shape=[jax.ShapeDtypeStruct((depth, M, ncols), F32), jax.ShapeDtypeStruct((M, ncols), BF16)],
        input_output_aliases=aliases,
        compiler_params=_params("arbitrary", "arbitrary"),
        name="mm_stacked",
    )(*args)


def _rms(x, g):
    return x * lax.rsqrt(jnp.mean(x * x, axis=-1, keepdims=True) + EPS) * g


def _norm_mod_body(x_ref, g_ref, sc_ref, sh_ref, o_ref):
    y = _rms(x_ref[0], g_ref[...])
    o_ref[0] = (y * (1.0 + sc_ref[0]) + sh_ref[0]).astype(o_ref.dtype)


def _norm_mod(x, g, mod, sc_chunk, sh_chunk, out_dtype, *, tt):
    B, T, D = x.shape
    per_tok = mod.shape[1] == T
    tg = tt if per_tok else 1
    mod_spec = lambda chunk: pl.BlockSpec((1, tg, D), lambda b, t: (b, t if per_tok else 0, chunk))
    return pl.pallas_call(
        _norm_mod_body,
        grid=(B, T // tt),
        in_specs=[pl.BlockSpec((1, tt, D), lambda b, t: (b, t, 0)),
                  pl.BlockSpec((1, D), lambda b, t: (0, 0)),
                  mod_spec(sc_chunk), mod_spec(sh_chunk)],
        out_specs=pl.BlockSpec((1, tt, D), lambda b, t: (b, t, 0)),
        out_shape=jax.ShapeDtypeStruct((B, T, D), out_dtype),
        compiler_params=_params("arbitrary", "arbitrary"),
        name="norm_mod",
    )(x, g, mod, mod)


def _norm_mod_rows_body(x_ref, g_ref, sc_ref, sh_ref, *refs):
    o_ref = refs[-1]
    y = _rms(x_ref[0], g_ref[...])
    o_ref[...] = y * (1.0 + sc_ref[0]) + sh_ref[0]


def _norm_mod_rows(x, g, mod, sc_chunk, sh_chunk, n_rows, row0, prev, *, tt):
    B, T, D = x.shape
    per_tok = mod.shape[1] == T
    tg = tt if per_tok else 1
    tpb = T // tt
    assert row0 % tt == 0
    mod_spec = lambda chunk: pl.BlockSpec((1, tg, D), lambda b, t: (b, t if per_tok else 0, chunk))
    in_specs = [pl.BlockSpec((1, tt, D), lambda b, t: (b, t, 0)),
                pl.BlockSpec((1, D), lambda b, t: (0, 0)),
                mod_spec(sc_chunk), mod_spec(sh_chunk)]
    args = [x, g, mod, mod]
    aliases = {}
    if prev is not None:
        in_specs.append(pl.BlockSpec(memory_space=pl.ANY))
        args.append(prev)
        aliases = {4: 0}
    return pl.pallas_call(
        _norm_mod_rows_body,
        grid=(B, tpb),
        in_specs=in_specs,
        out_specs=pl.BlockSpec((tt, D), lambda b, t: (row0 // tt + b * tpb + t, 0)),
        out_shape=jax.ShapeDtypeStruct((n_rows, D), F32),
        input_output_aliases=aliases,
        compiler_params=_params("arbitrary", "arbitrary"),
        name="norm_mod_rows",
    )(*args)


def _final_norm_body(x_ref, g_ref, o_ref):
    o_ref[0] = _rms(x_ref[0], g_ref[...])


def _final_norm(x, g, *, tt):
    B, T, D = x.shape
    return pl.pallas_call(
        _final_norm_body,
        grid=(B, T // tt),
        in_specs=[pl.BlockSpec((1, tt, D), lambda b, t: (b, t, 0)),
                  pl.BlockSpec((1, D), lambda b, t: (0, 0))],
        out_specs=pl.BlockSpec((1, tt, D), lambda b, t: (b, t, 0)),
        out_shape=jax.ShapeDtypeStruct((B, T, D), F32),
        compiler_params=_params("arbitrary", "arbitrary"),
        name="final_norm",
    )(x, g)


def _softplus(x):
    return jnp.maximum(x, 0.0) + jnp.log1p(jnp.exp(-jnp.abs(x)))


def _expm1(x):
    u = jnp.exp(x)
    return jnp.where(u == 1.0, x, jnp.where(u == 0.0, -1.0, (u - 1.0) * x / jnp.log(u)))


def _block_diag_dot(xc, w_ref):
    nb = w_ref.shape[0]
    outs = []
    for n in range(nb):
        xb = xc[:, n * RNN_BLOCK:(n + 1) * RNN_BLOCK].astype(BF16)
        outs.append(_dot(xb, w_ref[n].astype(BF16)))
    return jnp.concatenate(outs, axis=-1)


def _lru_gates(xc, wa_ref, ba, wx_ref, bx, lam):
    r = jax.nn.sigmoid(_block_diag_dot(xc, wa_ref) + ba)
    ig = jax.nn.sigmoid(_block_diag_dot(xc, wx_ref) + bx)
    log_a = (-LRU_C * r) * _softplus(-lam)
    a = jnp.exp(log_a)
    mult = jnp.sqrt(-_expm1(2.0 * log_a))
    return a, mult, ig


HIST = CONV_W - 1
EXT_PAD = 8


def _rglru_prompt_body(u_ref, gt_ref, cw_ref, cb_ref, wa_ref, ba_ref, wx_ref, bx_ref, lam_ref,
                       y_ref, hl_ref, nh_ref, ext_s, a_s, x_s, h_s):
    c = pl.program_id(1)
    tc = u_ref.shape[1]

    @pl.when(c == 0)
    def _():
        ext_s[0:EXT_PAD, :] = jnp.zeros((EXT_PAD, ext_s.shape[1]), F32)
        h_s[...] = jnp.zeros(h_s.shape, F32)

    @pl.when(c > 0)
    def _():
        ext_s[0:EXT_PAD, :] = ext_s[tc:tc + EXT_PAD, :]

    ext_s[EXT_PAD:EXT_PAD + tc, :] = u_ref[0]
    xc = cb_ref[...]
    for j in range(CONV_W):
        off = EXT_PAD - HIST + j
        xc = xc + ext_s[off:off + tc, :] * cw_ref[j:j + 1, :]
    a, mult, ig = _lru_gates(xc, wa_ref, ba_ref[...], wx_ref, bx_ref[...], lam_ref[...])
    row = c * tc + lax.broadcasted_iota(I32, (tc, 1), 0)
    mult = jnp.where(row == 0, 1.0, mult)
    a_s[...] = a
    x_s[...] = mult * ig * xc

    def step(t, h):
        h = a_s[pl.ds(t, 1), :] * h + x_s[pl.ds(t, 1), :]
        x_s[pl.ds(t, 1), :] = h
        return h

    h = lax.fori_loop(0, tc, step, h_s[0:1, :], unroll=8)
    h_s[0:1, :] = h
    y_ref[0] = (x_s[...] * jax.nn.gelu(gt_ref[0])).astype(y_ref.dtype)

    @pl.when(c == pl.num_programs(1) - 1)
    def _():
        hl_ref[0] = h
        nh_ref[0] = ext_s[EXT_PAD + tc - HIST:EXT_PAD + tc, :]


def _rglru_prompt(zug, P, l, *, tc):
    B, T, C2 = zug.shape
    C = C2 // 2
    nb = C // RNN_BLOCK
    vec = lambda: pl.BlockSpec((None, 1, C), lambda b, c: (l, 0, 0))
    blk = lambda: pl.BlockSpec((None, nb, RNN_BLOCK, RNN_BLOCK), lambda b, c: (l, 0, 0, 0))
    return pl.pallas_call(
        _rglru_prompt_body,
        grid=(B, T // tc),
        in_specs=[pl.BlockSpec((1, tc, C), lambda b, c: (b, c, 0)),
                  pl.BlockSpec((1, tc, C), lambda b, c: (b, c, 1)),
                  pl.BlockSpec((None, CONV_W, C), lambda b, c: (l, 0, 0)),
                  vec(), blk(), vec(), blk(), vec(), vec()],
        out_specs=[pl.BlockSpec((1, tc, C), lambda b, c: (b, c, 0)),
                   pl.BlockSpec((1, 1, C), lambda b, c: (b, 0, 0)),
                   pl.BlockSpec((1, HIST, C), lambda b, c: (b, 0, 0))],
        out_shape=[jax.ShapeDtypeStruct((B, T, C), BF16),
                   jax.ShapeDtypeStruct((B, 1, C), F32),
                   jax.ShapeDtypeStruct((B, HIST, C), F32)],
        scratch_shapes=[pltpu.VMEM((tc + EXT_PAD, C), F32), pltpu.VMEM((tc, C), F32),
                        pltpu.VMEM((tc, C), F32), pltpu.VMEM((8, C), F32)],
        compiler_params=_params("arbitrary", "arbitrary"),
        name="rglru_prompt",
    )(zug, zug, P["conv_w"], P["conv_b3"], P["lru_wa"], P["lru_ba3"], P["lru_wx"], P["lru_bx3"],
      P["lru_lambda3"])


def _rglru_sample_body(u_ref, gt_ref, hist_ref, h0_ref, cw_ref, cb_ref, wa_ref, ba_ref, wx_ref, bx_ref,
                       lam_ref, y_ref, h_ref):
    u = u_ref[...]
    xc = cb_ref[...]
    for j in range(HIST):
        xc = xc + hist_ref[j] * cw_ref[j:j + 1, :]
    xc = xc + u * cw_ref[HIST:HIST + 1, :]
    a, mult, ig = _lru_gates(xc, wa_ref, ba_ref[...], wx_ref, bx_ref[...], lam_ref[...])
    h = a * h0_ref[...] + mult * ig * xc
    h_ref[...] = h
    y_ref[...] = (h * jax.nn.gelu(gt_ref[...])).astype(y_ref.dtype)


def _rglru_sample(zug, hist_t, h0, P, l):
    Bs, C2 = zug.shape
    C = C2 // 2
    nb = C // RNN_BLOCK
    vec = lambda: pl.BlockSpec((None, 1, C), lambda i: (l, 0, 0))
    blk = lambda: pl.BlockSpec((None, nb, RNN_BLOCK, RNN_BLOCK), lambda i: (l, 0, 0, 0))
    return pl.pallas_call(
        _rglru_sample_body,
        grid=(1,),
        in_specs=[pl.BlockSpec((Bs, C), lambda i: (0, 0)),
                  pl.BlockSpec((Bs, C), lambda i: (0, 1)),
                  pl.BlockSpec((HIST, Bs, C), lambda i: (0, 0, 0)),
                  pl.BlockSpec((Bs, C), lambda i: (0, 0)),
                  pl.BlockSpec((None, CONV_W, C), lambda i: (l, 0, 0)),
                  vec(), blk(), vec(), blk(), vec(), vec()],
        out_specs=[pl.BlockSpec((Bs, C), lambda i: (0, 0)),
                   pl.BlockSpec((Bs, C), lambda i: (0, 0))],
        out_shape=[jax.ShapeDtypeStruct((Bs, C), BF16), jax.ShapeDtypeStruct((Bs, C), F32)],
        compiler_params=_params("arbitrary"),
        name="rglru_sample",
    )(zug, zug, hist_t, h0, P["conv_w"], P["conv_b3"], P["lru_wa"], P["lru_ba3"], P["lru_wx"],
      P["lru_bx3"], P["lru_lambda3"])


def _rel_bucket(dist):
    n = jnp.maximum(dist, 0)
    max_exact = N_BUCKETS // 2
    nf = jnp.maximum(n, 1).astype(F32)
    large = max_exact + (jnp.log(nf / max_exact) / math.log(MAX_DISTANCE / max_exact)
                         * (N_BUCKETS - max_exact)).astype(I32)
    large = jnp.minimum(large, N_BUCKETS - 1)
    return jnp.where(n < max_exact, n, large)


def _sort_key(x):
    bits = pltpu.bitcast(x, I32)
    return jnp.where(bits < 0, bits ^ jnp.int32(0x7FFFFFFF), bits)


def _kth_largest_key(count_ge, shape, k):
    c = count_ge(jnp.zeros(shape, I32))
    t = jnp.where(c >= k, jnp.int32(0), jnp.int32(INT_MIN))
    for bit in range(30, -1, -1):
        cand = t + jnp.int32(1 << bit)
        c = count_ge(cand)
        t = jnp.where(c >= k, cand, t)
    return t


KB = 2 * Q_BLOCK


def _attn_prompt_body(rb_ref, q_ref, k_ref, v_ref, qi_ref, ki_ref, wit_ref, o_ref,
                      key_s, msk_s, bias_s, qis_s, m_s, l_s, al_s, acc_s, lg_s, p_s, *, n_sel, n_heads, n_idx_heads):
    b = pl.program_id(0)
    i = pl.program_id(1)
    QB = Q_BLOCK
    n_groups = n_heads // KV_GROUP
    GQ = KV_GROUP * QB
    npair = (i + 2) // 2
    s_io = lax.broadcasted_iota(I32, (QB, QB), 0)
    t_io = lax.broadcasted_iota(I32, (QB, QB), 1)

    @pl.when((b == 0) & (i == 0))
    def _():
        for d in range(2):
            bucket = _rel_bucket(t_io - s_io + d * QB)
            for h in range(n_heads):
                acc = jnp.zeros((QB, QB), F32)
                for bb in range(N_BUCKETS):
                    acc = jnp.where(bucket == bb, rb_ref[bb, h], acc)
                bias_s[d, h] = acc
        for h in range(n_heads):
            bias_s[2, h] = jnp.full((QB, QB), rb_ref[N_BUCKETS - 1, h], F32)

    def causal_masked(j):
        off = jnp.where(j < i, QB, jnp.where(j == i, 0, -QB))
        return s_io > t_io + off

    for h in range(n_idx_heads):
        qis_s[h * QB:(h + 1) * QB, :] = qi_ref[0, :, h * IDX_DIM:(h + 1) * IDX_DIM]
    wt = wit_ref[0] * (n_idx_heads * IDX_DIM) ** -0.5

    def score_pair(jj, carry):
        ks = pl.multiple_of(jj * KB, KB)
        d = _dot_nt(ki_ref[0, pl.ds(ks, KB), :], qis_s[...])
        st = jnp.zeros((KB, QB), F32)
        for h in range(n_idx_heads):
            st = st + jnp.maximum(d[:, h * QB:(h + 1) * QB], 0.0) * wt[h:h + 1, :]
        for r in range(2):
            blk = jnp.where(causal_masked(2 * jj + r), NEG_INF, st[r * QB:(r + 1) * QB])
            key_s[pl.ds(ks + r * QB, QB), :] = _sort_key(blk)
        return carry

    lax.fori_loop(0, npair, score_pair, 0)

    def count(pred):
        def body(jj, acc):
            blk = key_s[pl.ds(pl.multiple_of(jj * KB, KB), KB), :]
            return acc + jnp.sum(jnp.where(pred(blk), 1.0, 0.0).reshape(KB // 32, 4, 8, QB), axis=0)
        acc = lax.fori_loop(0, npair, body, jnp.zeros((4, 8, QB), F32))
        return jnp.sum(acc.reshape(32, QB), axis=0, keepdims=True)

    thr = _kth_largest_key(lambda cand: count(lambda blk: blk >= cand), (1, QB), n_sel)
    need = n_sel - count(lambda blk: blk > thr)
    r_io = lax.broadcasted_iota(I32, (KB, KB), 0)
    c_io = lax.broadcasted_iota(I32, (KB, KB), 1)
    ltri = jnp.where(c_io <= r_io, 1.0, 0.0).astype(BF16)

    def mask_pair(jj, carry):
        ks = pl.multiple_of(jj * KB, KB)
        blk = key_s[pl.ds(ks, KB), :]
        eq = blk == thr
        rank = carry + _dot(ltri, jnp.where(eq, 1.0, 0.0).astype(BF16))
        keep = jnp.where(blk > thr, 1.0, jnp.where(eq & (rank <= need), 1.0, 0.0))
        for r in range(2):
            kr = jnp.where(causal_masked(2 * jj + r), 0.0, keep[r * QB:(r + 1) * QB])
            msk_s[pl.ds(ks + r * QB, QB), :] = jnp.where(kr > 0.0, 0.0, NEG_INF)
        return rank[KB - 1:KB, :]

    lax.fori_loop(0, npair, mask_pair, jnp.zeros((1, QB), F32))

    scale = HEAD_DIM ** -0.5
    m_s[...] = jnp.full(m_s.shape, NEG_INF, F32)
    l_s[...] = jnp.zeros(l_s.shape, F32)
    acc_s[...] = jnp.zeros(acc_s.shape, F32)

    def kv_pair(jj, carry):
        ks = pl.multiple_of(jj * KB, KB)
        mk = msk_s[pl.ds(ks, KB), :]
        mk = jnp.concatenate([mk] * KV_GROUP, axis=1)
        dsel = [jnp.clip(i - 2 * jj - r, 0, 2) for r in range(2)]
        for g in range(n_groups):
            heads = [g * KV_GROUP + r for r in range(KV_GROUP)]
            qp = jnp.concatenate([q_ref[0, :, h * HEAD_DIM:(h + 1) * HEAD_DIM] for h in heads], axis=0)
            kb = k_ref[0, pl.ds(ks, KB), g * HEAD_DIM:(g + 1) * HEAD_DIM]
            lg_s[g] = _dot_nt(kb, qp)
        for g in range(n_groups):
            heads = [g * KV_GROUP + r for r in range(KV_GROUP)]
            bias = jnp.concatenate(
                [jnp.concatenate([bias_s[dsel[r], h] for h in heads], axis=1) for r in range(2)], axis=0)
            lt = lg_s[g] * scale + bias + mk
            m_old = m_s[g]
            m_new = jnp.maximum(m_old, jnp.max(jnp.max(lt.reshape(4, KB // 4, GQ), axis=0), axis=0, keepdims=True))
            m_fin = jnp.where(m_new == NEG_INF, 0.0, m_new)
            alpha = jnp.exp(m_old - m_fin)
            p = jnp.exp(lt - m_fin)
            l_s[g] = alpha * l_s[g] + jnp.sum(jnp.sum(p.reshape(4, KB // 4, GQ), axis=0), axis=0, keepdims=True)
            al_s[g] = alpha
            p_s[g] = p.astype(BF16)
            m_s[g] = m_new
        for g in range(n_groups):
            vb = v_ref[0, pl.ds(ks, KB), g * HEAD_DIM:(g + 1) * HEAD_DIM]
            acc_s[g] = al_s[g] * acc_s[g] + _dot_tn(vb, p_s[g])
        return carry

    lax.fori_loop(0, npair, kv_pair, 0)
    for g in range(n_groups):
        ot = acc_s[g] / l_s[g]
        for r in range(KV_GROUP):
            h = g * KV_GROUP + r
            o_ref[0, :, h * HEAD_DIM:(h + 1) * HEAD_DIM] = ot[:, r * QB:(r + 1) * QB].T.astype(o_ref.dtype)


def _attn_prompt(q, k, v, qi, ki, wit, rel_bias):
    B, S, HD = q.shape
    n_heads = HD // HEAD_DIM
    n_groups = n_heads // KV_GROUP
    KD = k.shape[2]
    n_idx_heads = wit.shape[1]
    n_sel = min(TOPK_MAX, S // 4)
    QB = Q_BLOCK
    assert S % KB == 0
    return pl.pallas_call(
        functools.partial(_attn_prompt_body, n_sel=n_sel, n_heads=n_heads, n_idx_heads=n_idx_heads),
        grid=(B, S // QB),
        in_specs=[pl.BlockSpec(memory_space=pltpu.SMEM),
                  pl.BlockSpec((1, QB, HD), lambda b, i: (b, i, 0)),
                  pl.BlockSpec((1, S, KD), lambda b, i: (b, 0, 0)),
                  pl.BlockSpec((1, S, KD), lambda b, i: (b, 0, 0)),
                  pl.BlockSpec((1, QB, qi.shape[2]), lambda b, i: (b, i, 0)),
                  pl.BlockSpec((1, S, IDX_DIM), lambda b, i: (b, 0, 0)),
                  pl.BlockSpec((1, n_idx_heads, QB), lambda b, i: (b, 0, i))],
        out_specs=pl.BlockSpec((1, QB, HD), lambda b, i: (b, i, 0)),
        out_shape=jax.ShapeDtypeStruct((B, S, HD), BF16),
        scratch_shapes=[pltpu.VMEM((S, QB), I32), pltpu.VMEM((S, QB), F32),
                        pltpu.VMEM((3, n_heads, QB, QB), F32),
                        pltpu.VMEM((n_idx_heads * QB, IDX_DIM), BF16),
                        pltpu.VMEM((n_groups, 1, KV_GROUP * QB), F32),
                        pltpu.VMEM((n_groups, 1, KV_GROUP * QB), F32),
                        pltpu.VMEM((n_groups, 1, KV_GROUP * QB), F32),
                        pltpu.VMEM((n_groups, HEAD_DIM, KV_GROUP * QB), F32),
                        pltpu.VMEM((n_groups, KB, KV_GROUP * QB), F32),
                        pltpu.VMEM((n_groups, KB, KV_GROUP * QB), BF16)],
        compiler_params=_params("arbitrary", "arbitrary"),
        name="attn_prompt",
    )(rel_bias, q, k, v, qi, ki, wit)


SCORE_ROWS = 24


def _smp_scores_body(pt_ref, qi_ref, wcol_ref, kin_ref, *refs, n_pages, n_idx_heads):
    pages = refs[:n_pages]
    out_ref = refs[n_pages]
    q16 = qi_ref[0].astype(BF16)
    wcol = wcol_ref[0] * (n_idx_heads * IDX_DIM) ** -0.5
    rows = []
    for p in range(n_pages):
        d = _dot(q16, pages[p][...].astype(BF16))
        rows.append(jnp.sum(jnp.maximum(d, 0.0) * wcol, axis=0, keepdims=True))
    kin = kin_ref[0].astype(BF16).astype(F32)
    dn = jnp.sum(q16.astype(F32) * kin, axis=1, keepdims=True)
    snew = jnp.sum(jnp.maximum(dn, 0.0) * wcol, axis=0, keepdims=True)
    lane = lax.broadcasted_iota(I32, (1, PAGE), 1)
    rows.append(jnp.where(lane == 0, snew, NEG_INF))
    rows.append(jnp.full((SCORE_ROWS - n_pages - 1, PAGE), NEG_INF, F32))
    out_ref[0] = jnp.concatenate(rows, axis=0)


def _smp_scores(page_table, qi3, wcol, kin3, cache_idx_kt, layer):
    Bs, n_pages = page_table.shape
    n_idx_heads = qi3.shape[1]
    page_spec = lambda p: pl.BlockSpec((None, None, IDX_DIM, PAGE), lambda b, pt: (layer, pt[b, p], 0, 0))
    grid_spec = pltpu.PrefetchScalarGridSpec(
        num_scalar_prefetch=1,
        grid=(Bs,),
        in_specs=[pl.BlockSpec((1, n_idx_heads, IDX_DIM), lambda b, pt: (b, 0, 0)),
                  pl.BlockSpec((1, n_idx_heads, 1), lambda b, pt: (b, 0, 0)),
                  pl.BlockSpec((1, 1, IDX_DIM), lambda b, pt: (b, 0, 0))]
        + [page_spec(p) for p in range(n_pages)],
        out_specs=pl.BlockSpec((1, SCORE_ROWS, PAGE), lambda b, pt: (b, 0, 0)),
    )
    return pl.pallas_call(
        functools.partial(_smp_scores_body, n_pages=n_pages, n_idx_heads=n_idx_heads),
        grid_spec=grid_spec,
        out_shape=jax.ShapeDtypeStruct((Bs, SCORE_ROWS, PAGE), F32),
        compiler_params=_params("arbitrary"),
        name="smp_scores",
    )(page_table, qi3, wcol, kin3, *([cache_idx_kt] * n_pages))


def _smp_select_body(sc_ref, o_ref, key_s, *, n_sel):
    Bs, W = sc_ref.shape
    key_s[...] = _sort_key(sc_ref[...])

    def count_ge(cand):
        return jnp.sum(jnp.where(key_s[...] >= cand, 1.0, 0.0), axis=1, keepdims=True)

    thr = _kth_largest_key(count_ge, (Bs, 1), n_sel)
    n_gt = jnp.sum(jnp.where(key_s[...] > thr, 1.0, 0.0), axis=1, keepdims=True)
    need = n_sel - n_gt
    r_io = lax.broadcasted_iota(I32, (LANES, LANES), 0)
    c_io = lax.broadcasted_iota(I32, (LANES, LANES), 1)
    utri = jnp.where(r_io <= c_io, 1.0, 0.0).astype(BF16)
    carry = jnp.zeros((Bs, 1), F32)
    for j in range(W // LANES):
        blk = key_s[:, j * LANES:(j + 1) * LANES]
        eq = blk == thr
        rank = carry + _dot(jnp.where(eq, 1.0, 0.0).astype(BF16), utri)
        keep = jnp.where(blk > thr, 1.0, jnp.where(eq & (rank <= need), 1.0, 0.0))
        o_ref[:, j * LANES:(j + 1) * LANES] = keep
        carry = rank[:, LANES - 1:LANES]


def _smp_select(scores, n_sel):
    Bs, W = scores.shape
    return pl.pallas_call(
        functools.partial(_smp_select_body, n_sel=n_sel),
        grid=(1,),
        in_specs=[pl.BlockSpec((Bs, W), lambda i: (0, 0))],
        out_specs=pl.BlockSpec((Bs, W), lambda i: (0, 0)),
        out_shape=jax.ShapeDtypeStruct((Bs, W), F32),
        scratch_shapes=[pltpu.VMEM((Bs, W), I32)],
        compiler_params=_params("arbitrary"),
        name="smp_select",
    )(scores)


def _smp_attn_body(pt_ref, rbt_ref, q_ref, kn_ref, vn_ref, sel_ref, *refs, n_pages, n_heads, n_kv):
    kpages = refs[:n_pages]
    vpages = refs[n_pages:2 * n_pages]
    o_ref = refs[2 * n_pages]
    lg_s, bias_s, expand_s = refs[2 * n_pages + 1:]
    R = PAGE * n_kv
    row = lax.broadcasted_iota(I32, (1, R), 1)
    head = lax.broadcasted_iota(I32, (n_heads, 1), 0)
    own_group = (row % n_kv) == (head // KV_GROUP)

    @pl.when(pl.program_id(0) == 0)
    def _():
        k_io = lax.broadcasted_iota(I32, (PAGE, R), 0)
        r_io = lax.broadcasted_iota(I32, (PAGE, R), 1)
        expand_s[...] = jnp.where(r_io // n_kv == k_io, 1.0, 0.0).astype(BF16)
        bucket = _rel_bucket(PAGE - row // n_kv)
        acc = jnp.zeros((n_heads, R), F32)
        for bb in range(N_BUCKETS):
            acc = jnp.where(bucket == bb, rbt_ref[:, bb:bb + 1], acc)
        bias_s[...] = acc

    q16 = q_ref[0]
    scale = HEAD_DIM ** -0.5
    far_bias = rbt_ref[:, N_BUCKETS - 1:N_BUCKETS]
    sel = sel_ref[0]
    selx = _dot(sel.astype(BF16), expand_s[...])

    m = jnp.full((n_heads, LANES), NEG_INF, F32)
    for p in range(n_pages):
        lt = _dot_nt(q16, kpages[p][...].astype(BF16)) * scale
        lt = lt + (bias_s[...] if p == n_pages - 1 else far_bias)
        lt = jnp.where(own_group, jnp.where(selx[p:p + 1, :] > 0.5, lt, NEG_INF), NEG_INF)
        lg_s[p] = lt
        for c in range(R // LANES):
            m = jnp.maximum(m, lt[:, c * LANES:(c + 1) * LANES])
    m = jnp.max(m, axis=1, keepdims=True)
    qf = q16.astype(F32)
    ln = jnp.sum(qf * kn_ref[0].astype(BF16).astype(F32), axis=1, keepdims=True) * scale + rbt_ref[:, 0:1]
    ln = jnp.where(sel[n_pages:n_pages + 1, 0:1] > 0.5, ln, NEG_INF)
    m = jnp.maximum(m, ln)

    pn = jnp.exp(ln - m)
    l = pn
    acc = pn.astype(BF16).astype(F32) * vn_ref[0].astype(BF16).astype(F32)
    for p in range(n_pages):
        pr = jnp.exp(lg_s[p] - m)
        l = l + jnp.sum(pr, axis=1, keepdims=True)
        acc = acc + _dot(pr.astype(BF16), vpages[p][...].astype(BF16))
    o_ref[0] = (acc / l).astype(o_ref.dtype)


def _smp_attn(page_table, rbt_pad, q3, kn_rep, vn_rep, sel3, cache_k4, cache_v4, layer, n_kv):
    Bs, n_pages = page_table.shape
    n_heads = q3.shape[1]
    R = PAGE * n_kv
    page_spec = lambda p: pl.BlockSpec((None, None, R, HEAD_DIM), lambda b, pt: (layer, pt[b, p], 0, 0))
    head_spec = lambda: pl.BlockSpec((1, n_heads, HEAD_DIM), lambda b, pt: (b, 0, 0))
    grid_spec = pltpu.PrefetchScalarGridSpec(
        num_scalar_prefetch=1,
        grid=(Bs,),
        in_specs=[pl.BlockSpec((n_heads, LANES), lambda b, pt: (0, 0)),
                  head_spec(), head_spec(), head_spec(),
                  pl.BlockSpec((1, SCORE_ROWS, PAGE), lambda b, pt: (b, 0, 0))]
        + [page_spec(p) for p in range(n_pages)] * 2,
        out_specs=head_spec(),
        scratch_shapes=[pltpu.VMEM((n_pages, n_heads, R), F32), pltpu.VMEM((n_heads, R), F32),
                        pltpu.VMEM((PAGE, R), BF16)],
    )
    return pl.pallas_call(
        functools.partial(_smp_attn_body, n_pages=n_pages, n_heads=n_heads, n_kv=n_kv),
        grid_spec=grid_spec,
        out_shape=jax.ShapeDtypeStruct((Bs, n_heads, HEAD_DIM), BF16),
        compiler_params=_params("arbitrary"),
        name="smp_attn",
    )(page_table, rbt_pad, q3, kn_rep, vn_rep, sel3, *([cache_k4] * n_pages), *([cache_v4] * n_pages))


def _merge_body(yr_ref, ya_ref, wa_ref, wb_ref, ga0_ref, ga1_ref, gb0_ref, gb1_ref, o_ref, *, off):
    ga = jnp.concatenate([ga0_ref[:, off:], ga1_ref[:, :off]], axis=1)
    gb = jnp.concatenate([gb0_ref[:, off:], gb1_ref[:, :off]], axis=1)
    ba = _dot(yr_ref[...], wa_ref[...].astype(BF16))
    bb = _dot(ya_ref[...], wb_ref[...].astype(BF16))
    o_ref[...] = (jax.nn.sigmoid(ga) * ba + jax.nn.sigmoid(gb) * bb).astype(o_ref.dtype)


def _merge(yr, ya, wa, wb, ztail, off, l, *, tm, tn):
    M, K = yr.shape
    D = wa.shape[2]
    nj = D // tn
    assert 0 < off < tn and ztail.shape[1] >= (2 * nj + 1) * tn
    gate = lambda blk: pl.BlockSpec((tm, tn), lambda j, i: (i, blk + j))
    return pl.pallas_call(
        functools.partial(_merge_body, off=off),
        grid=(nj, M // tm),
        in_specs=[pl.BlockSpec((tm, K), lambda j, i: (i, 0)),
                  pl.BlockSpec((tm, K), lambda j, i: (i, 0)),
                  pl.BlockSpec((None, K, tn), lambda j, i: (l, 0, j)),
                  pl.BlockSpec((None, K, tn), lambda j, i: (l, 0, j)),
                  gate(0), gate(1), gate(nj), gate(nj + 1)],
        out_specs=pl.BlockSpec((tm, tn), lambda j, i: (i, j)),
        out_shape=jax.ShapeDtypeStruct((M, D), BF16),
        compiler_params=_params("arbitrary", "arbitrary"),
        name="merge",
    )(yr, ya, wa, wb, ztail, ztail, ztail, ztail)


def _proj_res_body(a_ref, w_ref, x_ref, g_ref, o_ref):
    o_ref[0] = x_ref[0] + g_ref[0] * _dot(a_ref[...], w_ref[...].astype(BF16))


def _proj_res(a, w, l, x, mod, g_chunk, *, tm, tn):
    B, T, D = x.shape
    K = a.shape[1]
    tpb = T // tm
    nj = D // tn
    per_tok = mod.shape[1] == T
    tg = tm if per_tok else 1
    gmap = (lambda j, i: (i // tpb, i % tpb, g_chunk * nj + j)) if per_tok else \
        (lambda j, i: (i // tpb, 0, g_chunk * nj + j))
    return pl.pallas_call(
        _proj_res_body,
        grid=(nj, B * tpb),
        in_specs=[pl.BlockSpec((tm, K), lambda j, i: (i, 0)),
                  pl.BlockSpec((None, K, tn), lambda j, i: (l, 0, j)),
                  pl.BlockSpec((1, tm, tn), lambda j, i: (i // tpb, i % tpb, j)),
                  pl.BlockSpec((1, tg, tn), gmap)],
        out_specs=pl.BlockSpec((1, tm, tn), lambda j, i: (i // tpb, i % tpb, j)),
        out_shape=jax.ShapeDtypeStruct((B, T, D), F32),
        compiler_params=_params("arbitrary", "arbitrary"),
        name="proj_res",
    )(a, w, x, mod)


def _router_body(h_ref, r_ref, o_ref, *, n_experts):
    lg = _dot(h_ref[...].astype(BF16), r_ref[...].astype(BF16))
    lane = lax.broadcasted_iota(I32, lg.shape, 1)
    lg = jnp.where(lane < n_experts, lg, NEG_INF)
    m1 = jnp.max(lg, axis=1, keepdims=True)
    i1 = jnp.min(jnp.where(lg == m1, lane, LANES), axis=1, keepdims=True)
    rest = jnp.where(lane == i1, NEG_INF, lg)
    m2 = jnp.max(rest, axis=1, keepdims=True)
    i2 = jnp.min(jnp.where(rest == m2, lane, LANES), axis=1, keepdims=True)
    e2 = jnp.exp(m2 - m1)
    den = 1.0 + e2
    o_ref[...] = jnp.where(lane == 0, i1.astype(F32),
                           jnp.where(lane == 1, i2.astype(F32),
                                     jnp.where(lane == 2, 1.0 / den, jnp.where(lane == 3, e2 / den, 0.0))))


def _router(h, router_pad, *, tm, n_experts):
    M, D = h.shape
    return pl.pallas_call(
        functools.partial(_router_body, n_experts=n_experts),
        grid=(M // tm,),
        in_specs=[pl.BlockSpec((tm, D), lambda i: (i, 0)),
                  pl.BlockSpec((D, LANES), lambda i: (0, 0))],
        out_specs=pl.BlockSpec((tm, LANES), lambda i: (i, 0)),
        out_shape=jax.ShapeDtypeStruct((M, LANES), F32),
        compiler_params=_params("arbitrary"),
        name="router",
    )(h, router_pad)


def _ffn_body(h_ref, w1_ref, w3_ref, w2_ref, x_ref, g2_ref, o_ref):
    f = pl.program_id(1)

    @pl.when(f == 0)
    def _():
        o_ref[...] = jnp.zeros(o_ref.shape, F32)

    h = h_ref[...]
    a = _dot(h, w1_ref[...].astype(BF16))
    b = _dot(h, w3_ref[...].astype(BF16))
    act = a * jax.nn.sigmoid(a) * b
    o_ref[0] += _dot(act.astype(BF16), w2_ref[...].astype(BF16))

    @pl.when(f == pl.num_programs(1) - 1)
    def _():
        o_ref[0] = x_ref[0] + g2_ref[0] * o_ref[0]


def _ffn(h, w1, w3, w2, layer_idx, x, mod, g_chunk, *, tm, tf):
    B, T, D = x.shape
    tpb = T // tm
    F = w1.shape[2]
    per_tok = mod.shape[1] == T
    tg = tm if per_tok else 1
    gmap = (lambda i, f: (i // tpb, i % tpb, g_chunk)) if per_tok else (lambda i, f: (i // tpb, 0, g_chunk))
    return pl.pallas_call(
        _ffn_body,
        grid=(B * tpb, F // tf),
        in_specs=[pl.BlockSpec((tm, D), lambda i, f: (i, 0)),
                  pl.BlockSpec((None, D, tf), lambda i, f: (layer_idx, 0, f)),
                  pl.BlockSpec((None, D, tf), lambda i, f: (layer_idx, 0, f)),
                  pl.BlockSpec((None, tf, D), lambda i, f: (layer_idx, f, 0)),
                  pl.BlockSpec((1, tm, D), lambda i, f: (i // tpb, i % tpb, 0), pipeline_mode=pl.Buffered(1)),
                  pl.BlockSpec((1, tg, D), gmap)],
        out_specs=pl.BlockSpec((1, tm, D), lambda i, f: (i // tpb, i % tpb, 0)),
        out_shape=jax.ShapeDtypeStruct((B, T, D), F32),
        compiler_params=_params("arbitrary", "arbitrary"),
        name="ffn_dense",
    )(h, w1, w3, w2, x, mod)


def _moe_chunk(tm, n_f):
    return -(-(-(-tm // n_f)) // 8) * 8


def _moe_body(te_ref, dst_ref, nval_ref, nused_ref, h_hbm, w1_ref, w3_ref, w2_ref, rw_ref, out_hbm,
              xg_s, xb_s, acc_s, os_s, sem_in, sem_out, *, n_tok, n_f, n_tiles, sub):
    i = pl.program_id(0)
    f = pl.program_id(1)
    tm = xb_s.shape[0]
    chunk = _moe_chunk(tm, n_f)
    ts = n_f * chunk

    def gather(base, r):
        d = dst_ref[base + r]
        tok = jnp.maximum(d - jnp.where(d >= n_tok, n_tok, 0), 0)
        pltpu.make_async_copy(h_hbm.at[pl.ds(tok, 1), :], xg_s.at[pl.ds(r, 1), :], sem_in).start()

    def scatter(base, r):
        d = dst_ref[base + r]
        row = jnp.where(d >= 0, d, 2 * n_tok + r)
        pltpu.make_async_copy(os_s.at[pl.ds(r, 1), :], out_hbm.at[pl.ds(row, 1), :], sem_out).start()

    def wait_gather():
        pltpu.make_async_copy(h_hbm.at[pl.ds(0, ts), :], xg_s, sem_in).wait()

    def wait_scatter():
        pltpu.make_async_copy(os_s, out_hbm.at[pl.ds(0, ts), :], sem_out).wait()

    @pl.when((i == 0) & (f == 0))
    def _():
        os_s[...] = jnp.zeros(os_s.shape, F32)

        def start(r, c):
            gather(ts, r)
            return c

        lax.fori_loop(0, ts, start, 0)

    @pl.when(f == 0)
    def _():
        wait_gather()
        xb_s[...] = xg_s[0:tm, :].astype(BF16)
        acc_s[...] = jnp.zeros(acc_s.shape, F32)

    def move_rows():
        for k in range(chunk):
            r = f * chunk + k
            gather((i + 2) * ts, r)
            scatter(i * ts, r)

    def expert_rows(lo):
        x = xb_s[lo:lo + sub, :]
        a = _dot(x, w1_ref[...].astype(BF16))
        b = _dot(x, w3_ref[...].astype(BF16))
        act = a * jax.nn.sigmoid(a) * b
        acc_s[lo:lo + sub, :] += _dot(act.astype(BF16), w2_ref[...].astype(BF16))

    @pl.when(i < nused_ref[0])
    def _():
        move_rows()
        expert_rows(0)
        for lo in range(sub, tm, sub):
            @pl.when(nval_ref[i] > lo)
            def _():
                expert_rows(lo)

    @pl.when(i >= nused_ref[0])
    def _():
        move_rows()

    @pl.when(f == n_f - 1)
    def _():
        wait_scatter()

        @pl.when(i < nused_ref[0])
        def _():
            os_s[0:tm, :] = acc_s[...] * rw_ref[...]

        @pl.when(i == n_tiles - 1)
        def _():
            def start(r, c):
                scatter((i + 1) * ts, r)
                return c

            lax.fori_loop(0, ts, start, 0)
            wait_scatter()
            wait_gather()


def _moe_sparse(h_all, top_i, top_w, w1, w3, w2, layer_idx, *, tm, sub, tf):
    M, D = h_all.shape
    nE, _, F = w1.shape[1:]
    n_asg = 2 * M
    n_tiles = -(-(n_asg + nE * (tm - 1)) // tm)
    m_pad = n_tiles * tm
    nf = F // tf
    ts = nf * _moe_chunk(tm, nf)
    e_flat = top_i.T.reshape(n_asg)
    onehot = (e_flat[:, None] == jnp.arange(nE, dtype=I32)[None, :]).astype(I32)
    cum = jnp.cumsum(onehot, axis=0)
    counts = cum[-1]
    rank = jnp.take_along_axis(cum, e_flat[:, None], axis=1)[:, 0] - 1
    padded = -(-counts // tm) * tm
    ends = jnp.cumsum(padded)
    starts = ends - padded
    pos = starts[e_flat] + rank
    dst = jnp.full((m_pad,), -1, I32).at[pos].set(jnp.arange(n_asg, dtype=I32))
    dst = jnp.pad(dst.reshape(n_tiles, tm), ((1, 1), (0, ts - tm)), constant_values=-1).reshape(-1)
    roww = jnp.zeros((m_pad,), F32).at[pos].set(top_w.T.reshape(n_asg))
    tile_start = jnp.arange(n_tiles, dtype=I32) * tm
    n_used = (ends[-1] // tm).astype(I32)
    tile_e = jnp.minimum(jnp.searchsorted(ends, tile_start, side="right").astype(I32), nE - 1)
    tile_e = jnp.where(tile_start < ends[-1], tile_e, tile_e[jnp.maximum(n_used - 1, 0)])
    nval = jnp.clip(counts[tile_e] - (tile_start - starts[tile_e]), 0, tm).astype(I32)

    def fidx(i, f, nused):
        return jnp.where(i < nused[0], f, nf - 1)

    grid_spec = pltpu.PrefetchScalarGridSpec(
        num_scalar_prefetch=4,
        grid=(n_tiles, nf),
        in_specs=[pl.BlockSpec(memory_space=pl.ANY),
                  pl.BlockSpec((None, None, D, tf), lambda i, f, te, ds_, nv, nu: (layer_idx, te[i], 0, fidx(i, f, nu))),
                  pl.BlockSpec((None, None, D, tf), lambda i, f, te, ds_, nv, nu: (layer_idx, te[i], 0, fidx(i, f, nu))),
                  pl.BlockSpec((None, None, tf, D), lambda i, f, te, ds_, nv, nu: (layer_idx, te[i], fidx(i, f, nu), 0)),
                  pl.BlockSpec((tm, 1), lambda i, f, te, ds_, nv, nu: (i, 0))],
        out_specs=pl.BlockSpec(memory_space=pl.ANY),
        scratch_shapes=[pltpu.VMEM((ts, D), F32), pltpu.VMEM((tm, D), BF16), pltpu.VMEM((tm, D), F32),
                        pltpu.VMEM((ts, D), F32), pltpu.SemaphoreType.DMA(()), pltpu.SemaphoreType.DMA(())],
    )
    return pl.pallas_call(
        functools.partial(_moe_body, n_tok=M, n_f=nf, n_tiles=n_tiles, sub=sub),
        grid_spec=grid_spec,
        out_shape=jax.ShapeDtypeStruct((n_asg + ts, D), F32),
        compiler_params=_params("arbitrary", "arbitrary"),
        name="moe_sparse",
    )(tile_e, dst, nval, n_used.reshape(1), h_all, w1, w3, w2, roww.reshape(m_pad, 1))


def _moe_combine_body(x_ref, g2_ref, a_ref, b_ref, o_ref):
    o_ref[0] = x_ref[0] + g2_ref[0] * (a_ref[...] + b_ref[...])


def _moe_combine(x, mod, g_chunk, y2, row0, n_tok, *, tt):
    B, T, D = x.shape
    per_tok = mod.shape[1] == T
    tg = tt if per_tok else 1
    tpb = T // tt
    assert row0 % tt == 0 and n_tok % tt == 0
    rb0, rb1 = row0 // tt, (n_tok + row0) // tt
    return pl.pallas_call(
        _moe_combine_body,
        grid=(B, tpb),
        in_specs=[pl.BlockSpec((1, tt, D), lambda b, t: (b, t, 0)),
                  pl.BlockSpec((1, tg, D), lambda b, t: (b, t if per_tok else 0, g_chunk)),
                  pl.BlockSpec((tt, D), lambda b, t: (rb0 + b * tpb + t, 0)),
                  pl.BlockSpec((tt, D), lambda b, t: (rb1 + b * tpb + t, 0))],
        out_specs=pl.BlockSpec((1, tt, D), lambda b, t: (b, t, 0)),
        out_shape=jax.ShapeDtypeStruct((B, T, D), F32),
        compiler_params=_params("arbitrary", "arbitrary"),
        name="moe_combine",
    )(x, mod, y2, y2)


def _tile(n, prefs):
    for p in prefs:
        if n % p == 0:
            return p
    return n


def _mixer_half(l, x, mod, P, attend, rglru, kv_prev, *, tt, tm):
    B, T, D = x.shape
    M = B * T
    h = _norm_mod(x, P["norm_g"][l, 0:1], mod, 1, 0, BF16, tt=tt).reshape(M, D)
    w_in = P["w_in_t"]
    (zug,) = _mm(h, w_in, l, 0, 2 * D, [F32], tm=tm, tn=1024, w_nk=True)
    (q,) = _mm(h, w_in, l, 2 * D, D, [BF16], tm=tm, tn=1024, w_nk=True)
    KD = P["n_kv_cols"] // 2
    depth = w_in.shape[0]
    tnk = _tile(KD, (1024,))
    k_all, k16 = _mm_stacked(h, w_in, l, depth, P["col_kv"], KD, kv_prev and kv_prev[0], tm=tm, tn=tnk, w_nk=True)
    v_all, v16 = _mm_stacked(h, w_in, l, depth, P["col_kv"] + KD, KD, kv_prev and kv_prev[1], tm=tm, tn=tnk,
                             w_nk=True)
    (qi,) = _mm(h, w_in, l, P["col_qi"], P["n_qi_cols"], [BF16], tm=tm, tn=1024, w_nk=True)
    tnt = 512
    n_tail = -(-(P["n_kiwi"] + 2 * D) // tnt) * tnt
    (ztail,) = _mm(h, w_in, l, P["col_ki"], n_tail, [F32], tm=tm, tn=tnt, w_nk=True)
    kiwi = ztail[:, :LANES]
    y_rnn, h_last, new_hist = rglru(l, zug)
    y_att = attend(l, q, (k_all, v_all), (k16, v16), qi, kiwi)
    merged = _merge(y_rnn.reshape(M, D), y_att.reshape(M, D), P["w_branch_a"], P["w_branch_b"], ztail,
                    P["n_kiwi"], l, tm=_tile(M, (512,)), tn=tnt)
    x = _proj_res(merged, P["w_out"], l, x, mod, 2, tm=tm, tn=1024)
    return x, h_last, new_hist, (k_all, v_all), kiwi


def kernel(x_prompt, x_sample, c_prompt, c_sample, cache_k, cache_v, cache_idx_k, state_h, state_conv,
           page_table, w_ada, b_ada, norm_g, w_in, conv_w, conv_b, lru_wa, lru_ba, lru_wx, lru_bx,
           lru_lambda, w_branch_a, w_branch_b, w_out, rel_bias, ffn_w1, ffn_w3, ffn_w2,
           moe_router, moe_w1, moe_w3, moe_w2, final_g):
    Bp, S, D = x_prompt.shape
    Bs = x_sample.shape[0]
    Mp = Bp * S
    depth = w_in.shape[0]
    n_heads = rel_bias.shape[1]
    n_phys, _, n_kv = cache_k.shape[1:4]
    KD = n_kv * HEAD_DIM
    n_idx_heads = (w_in.shape[2] - (5 * D + 2 * KD + IDX_DIM)) // (IDX_DIM + 1)
    n_pages = page_table.shape[1]
    past = n_pages * PAGE
    col_qi = 3 * D + 2 * KD
    col_ki = col_qi + n_idx_heads * IDX_DIM
    n_kiwi = IDX_DIM + n_idx_heads
    n_experts = moe_router.shape[2]

    P = dict(
        col_kv=3 * D, n_kv_cols=2 * KD, col_qi=col_qi, n_qi_cols=n_idx_heads * IDX_DIM,
        col_ki=col_ki, n_kiwi=n_kiwi, norm_g=norm_g, w_in_t=jnp.swapaxes(w_in, 1, 2), conv_w=conv_w, lru_wa=lru_wa, lru_wx=lru_wx,
        conv_b3=conv_b[:, None, :], lru_ba3=lru_ba[:, None, :], lru_bx3=lru_bx[:, None, :],
        lru_lambda3=lru_lambda[:, None, :],
        w_branch_a=w_branch_a, w_branch_b=w_branch_b, w_out=w_out,
    )
    router_pad = jnp.pad(moe_router, ((0, 0), (0, 0), (0, LANES - n_experts)))
    rbt_pad = jnp.pad(rel_bias.T, ((0, 0), (0, LANES - rel_bias.shape[0])))
    cache_idx_kt = jnp.swapaxes(cache_idx_k, 2, 3)
    cache_k4 = cache_k.reshape(depth, n_phys, PAGE * n_kv, HEAD_DIM)
    cache_v4 = cache_v.reshape(depth, n_phys, PAGE * n_kv, HEAD_DIM)

    n_c = Bp + Bs
    n_c_pad = -(-n_c // 8) * 8
    c_all = jnp.concatenate([c_prompt, c_sample, jnp.zeros((n_c_pad - n_c, D), F32)], axis=0)
    b_ada3 = b_ada[:, None, :]

    def attend_prompt(l, q, kv32, kv16, qi, kiwi):
        ki = kiwi[:, :IDX_DIM].astype(BF16).reshape(Bp, S, IDX_DIM)
        wit = kiwi[:, IDX_DIM:n_kiwi].reshape(Bp, S, n_idx_heads).transpose(0, 2, 1)
        return _attn_prompt(q.reshape(Bp, S, D), kv16[0].reshape(Bp, S, KD), kv16[1].reshape(Bp, S, KD),
                            qi.reshape(Bp, S, -1), ki, wit, rel_bias)

    def attend_sample(l, q, kv32, kv16, qi, kiwi):
        qi3 = qi.reshape(Bs, n_idx_heads, IDX_DIM)
        wcol = kiwi[:, IDX_DIM:n_kiwi].reshape(Bs, n_idx_heads, 1)
        kin3 = kiwi[:, :IDX_DIM].reshape(Bs, 1, IDX_DIM)
        scores = _smp_scores(page_table, qi3, wcol, kin3, cache_idx_kt, l)
        n_sel = min(TOPK_MAX, (past + 1) // 4)
        sel3 = _smp_select(scores.reshape(Bs, SCORE_ROWS * PAGE), n_sel).reshape(Bs, SCORE_ROWS, PAGE)
        q3 = q.reshape(Bs, n_heads, HEAD_DIM)
        kn_rep = jnp.repeat(kv32[0][l].reshape(Bs, n_kv, HEAD_DIM), KV_GROUP, axis=1)
        vn_rep = jnp.repeat(kv32[1][l].reshape(Bs, n_kv, HEAD_DIM), KV_GROUP, axis=1)
        return _smp_attn(page_table, rbt_pad, q3, kn_rep, vn_rep, sel3, cache_k4, cache_v4, l, n_kv)

    xp, xs = x_prompt, x_sample.reshape(1, Bs, D)
    outs_p, outs_s = [], []
    kvp = kvs = None
    for l in range(depth):
        (mod,) = _mm(c_all, w_ada, l, 0, 6 * D, [F32], tm=n_c_pad, tn=1024, bias=b_ada3, silu_in=True)
        mod_p = mod[:Bp].reshape(Bp, 1, 6 * D)
        mod_s = mod[Bp:n_c].reshape(1, Bs, 6 * D)
        hist_t = state_conv[l].transpose(1, 0, 2)

        xp, hp, cp, kvp, kiwip = _mixer_half(
            l, xp, mod_p, P, attend_prompt, lambda l_, zug: _rglru_prompt(zug.reshape(Bp, S, 2 * D), P, l_, tc=256),
            kvp, tt=512, tm=1024)

        def rglru_s(l_, zug, hist_t=hist_t):
            y, h = _rglru_sample(zug, hist_t, state_h[l_], P, l_)
            new_hist = jnp.concatenate([state_conv[l_][:, 1:], zug[:, None, :D]], axis=1)
            return y, h, new_hist

        xs, hs, cs, kvs, kiwis = _mixer_half(l, xs, mod_s, P, attend_sample, rglru_s, kvs, tt=Bs, tm=Bs)

        g2p, g2s = P["norm_g"][l, 1:2], P["norm_g"][l, 1:2]
        if l % 2 == 0:
            h2p = _norm_mod(xp, g2p, mod_p, 4, 3, BF16, tt=512).reshape(Mp, D)
            h2s = _norm_mod(xs, g2s, mod_s, 4, 3, BF16, tt=Bs).reshape(Bs, D)
            xp = _ffn(h2p, ffn_w1, ffn_w3, ffn_w2, l // 2, xp, mod_p, 5, tm=1024, tf=256)
            xs = _ffn(h2s, ffn_w1, ffn_w3, ffn_w2, l // 2, xs, mod_s, 5, tm=Bs, tf=512)
        else:
            n_tok = Mp + Bs
            h_all = _norm_mod_rows(xp, g2p, mod_p, 4, 3, n_tok, 0, None, tt=512)
            h_all = _norm_mod_rows(xs, g2s, mod_s, 4, 3, n_tok, Mp, h_all, tt=Bs)
            route = _router(h_all, router_pad[l // 2], tm=_tile(n_tok, (1024, 640, 512, 256, 128)),
                            n_experts=n_experts)
            y2 = _moe_sparse(h_all, route[:, 0:2].astype(I32), route[:, 2:4], moe_w1, moe_w3, moe_w2, l // 2,
                             tm=1024, sub=512, tf=256)
            xp = _moe_combine(xp, mod_p, 5, y2, 0, n_tok, tt=Bs)
            xs = _moe_combine(xs, mod_s, 5, y2, Mp, n_tok, tt=Bs)

        outs_p.append((kiwip[:, :IDX_DIM].reshape(Bp, S, IDX_DIM), hp.reshape(Bp, D), cp))
        outs_s.append((kiwis[:, :IDX_DIM].reshape(Bs, 1, IDX_DIM), hs, cs))
    y_prompt = _final_norm(xp, final_g[None, :], tt=512)
    y_sample = _final_norm(xs, final_g[None, :], tt=Bs).reshape(Bs, 1, D)
    stack = lambda outs, i: jnp.stack([o[i] for o in outs])
    kv_shape_p = (depth, Bp, S, n_kv, HEAD_DIM)
    kv_shape_s = (depth, Bs, 1, n_kv, HEAD_DIM)
    return (y_prompt, y_sample,
            kvp[0].reshape(kv_shape_p), kvp[1].reshape(kv_shape_p),
            stack(outs_p, 0), stack(outs_p, 1), stack(outs_p, 2),
            kvs[0].reshape(kv_shape_s), kvs[1].reshape(kv_shape_s),
            stack(outs_s, 0), stack(outs_s, 1), stack(outs_s, 2))
```

```python
import functools
import math

import jax
import jax.numpy as jnp
from jax import lax
from jax.experimental import pallas as pl
from jax.experimental.pallas import tpu as pltpu

F32 = jnp.float32
BF16 = jnp.bfloat16
I32 = jnp.int32

VMEM_LIMIT_BYTES = 56 * 1024 * 1024
LANES = 128

EPS = 1e-6
LRU_C = 8.0
RNN_BLOCK = 128
CONV_W = 4
HEAD_DIM = 128
KV_GROUP = 2
IDX_DIM = 64
TOPK_MAX = 256
Q_BLOCK = 128
N_BUCKETS = 32
MAX_DISTANCE = 128
PAGE = 128
NEG_INF = float("-inf")
INT_MIN = -(2 ** 31)


def _params(*sem):
    return pltpu.CompilerParams(dimension_semantics=sem, vmem_limit_bytes=VMEM_LIMIT_BYTES)


def _dot(a, b):
    return jnp.dot(a, b, preferred_element_type=F32)


def _dot_nt(a, b):
    return lax.dot_general(a, b, (((1,), (1,)), ((), ())), preferred_element_type=F32)


def _dot_tn(a, b):
    return lax.dot_general(a, b, (((0,), (0,)), ((), ())), preferred_element_type=F32)


def _mm_body(x_ref, w_ref, *refs, silu_in, has_bias, w_nk):
    x = x_ref[...]
    if silu_in:
        x = x.astype(F32)
        x = x * jax.nn.sigmoid(x)
    acc = _dot_nt(x.astype(BF16), w_ref[0].astype(BF16)) if w_nk else _dot(x.astype(BF16), w_ref[...].astype(BF16))
    outs = refs
    if has_bias:
        acc = acc + refs[0][...]
        outs = refs[1:]
    for o in outs:
        o[...] = acc.astype(o.dtype)


def _w_spec(K, tn, layer, col0, w_nk):
    if w_nk:
        assert col0 % 8 == 0
        return pl.BlockSpec((pl.Element(1), pl.Element(tn), pl.Element(K)),
                            lambda j, i: (layer, pl.multiple_of(col0 + j * tn, 8), 0))
    assert col0 % tn == 0
    return pl.BlockSpec((None, K, tn), lambda j, i: (layer, 0, col0 // tn + j))


def _mm(x, w, layer, col0, ncols, out_dtypes, *, tm, tn, bias=None, silu_in=False, w_nk=False):
    M, K = x.shape
    assert M % tm == 0 and ncols % tn == 0
    in_specs = [pl.BlockSpec((tm, K), lambda j, i: (i, 0)), _w_spec(K, tn, layer, col0, w_nk)]
    args = [x, w]
    if bias is not None:
        in_specs.append(pl.BlockSpec((None, 1, tn), lambda j, i: (layer, 0, col0 // tn + j)))
        args.append(bias)
    outs = pl.pallas_call(
        functools.partial(_mm_body, silu_in=silu_in, has_bias=bias is not None, w_nk=w_nk),
        grid=(ncols // tn, M // tm),
        in_specs=in_specs,
        out_specs=[pl.BlockSpec((tm, tn), lambda j, i: (i, j)) for _ in out_dtypes],
        out_shape=[jax.ShapeDtypeStruct((M, ncols), dt) for dt in out_dtypes],
        compiler_params=_params("arbitrary", "arbitrary"),
        name="mm",
    )(*args)
    return outs


def _mm_stacked_body(x_ref, w_ref, *refs, w_nk):
    o32_ref, o16_ref = refs[-2:]
    acc = _dot_nt(x_ref[...], w_ref[0].astype(BF16)) if w_nk else _dot(x_ref[...], w_ref[...].astype(BF16))
    o32_ref[...] = acc
    o16_ref[...] = acc.astype(BF16)


def _mm_stacked(x, w, layer, depth, col0, ncols, prev, *, tm, tn, w_nk=False):
    M, K = x.shape
    assert M % tm == 0 and ncols % tn == 0
    in_specs = [pl.BlockSpec((tm, K), lambda j, i: (i, 0)), _w_spec(K, tn, layer, col0, w_nk)]
    args = [x, w]
    aliases = {}
    if prev is not None:
        in_specs.append(pl.BlockSpec(memory_space=pl.ANY))
        args.append(prev)
        aliases = {2: 0}
    return pl.pallas_call(
        functools.partial(_mm_stacked_body, w_nk=w_nk),
        grid=(ncols // tn, M // tm),
        in_specs=in_specs,
        out_specs=[pl.BlockSpec((None, tm, tn), lambda j, i: (layer, i, j)),
                   pl.BlockSpec((tm, tn), lambda j, i: (i, j))],
        out_shape=[jax.ShapeDtypeStruct((depth, M, ncols), F32), jax.ShapeDtypeStruct((M, ncols), BF16)],
        input_output_aliases=aliases,
        compiler_params=_params("arbitrary", "arbitrary"),
        name="mm_stacked",
    )(*args)


def _rms(x, g):
    return x * lax.rsqrt(jnp.mean(x * x, axis=-1, keepdims=True) + EPS) * g


def _norm_mod_body(x_ref, g_ref, sc_ref, sh_ref, o_ref):
    y = _rms(x_ref[0], g_ref[...])
    o_ref[0] = (y * (1.0 + sc_ref[0]) + sh_ref[0]).astype(o_ref.dtype)


def _norm_mod(x, g, mod, sc_chunk, sh_chunk, out_dtype, *, tt):
    B, T, D = x.shape
    per_tok = mod.shape[1] == T
    tg = tt if per_tok else 1
    mod_spec = lambda chunk: pl.BlockSpec((1, tg, D), lambda b, t: (b, t if per_tok else 0, chunk))
    return pl.pallas_call(
        _norm_mod_body,
        grid=(B, T // tt),
        in_specs=[pl.BlockSpec((1, tt, D), lambda b, t: (b, t, 0)),
                  pl.BlockSpec((1, D), lambda b, t: (0, 0)),
                  mod_spec(sc_chunk), mod_spec(sh_chunk)],
        out_specs=pl.BlockSpec((1, tt, D), lambda b, t: (b, t, 0)),
        out_shape=jax.ShapeDtypeStruct((B, T, D), out_dtype),
        compiler_params=_params("arbitrary", "arbitrary"),
        name="norm_mod",
    )(x, g, mod, mod)


def _norm_mod_rows_body(x_ref, g_ref, sc_ref, sh_ref, *refs):
    o_ref = refs[-1]
    y = _rms(x_ref[0], g_ref[...])
    o_ref[...] = y * (1.0 + sc_ref[0]) + sh_ref[0]


def _norm_mod_rows(x, g, mod, sc_chunk, sh_chunk, n_rows, row0, prev, *, tt):
    B, T, D = x.shape
    per_tok = mod.shape[1] == T
    tg = tt if per_tok else 1
    tpb = T // tt
    assert row0 % tt == 0
    mod_spec = lambda chunk: pl.BlockSpec((1, tg, D), lambda b, t: (b, t if per_tok else 0, chunk))
    in_specs = [pl.BlockSpec((1, tt, D), lambda b, t: (b, t, 0)),
                pl.BlockSpec((1, D), lambda b, t: (0, 0)),
                mod_spec(sc_chunk), mod_spec(sh_chunk)]
    args = [x, g, mod, mod]
    aliases = {}
    if prev is not None:
        in_specs.append(pl.BlockSpec(memory_space=pl.ANY))
        args.append(prev)
        aliases = {4: 0}
    return pl.pallas_call(
        _norm_mod_rows_body,
        grid=(B, tpb),
        in_specs=in_specs,
        out_specs=pl.BlockSpec((tt, D), lambda b, t: (row0 // tt + b * tpb + t, 0)),
        out_shape=jax.ShapeDtypeStruct((n_rows, D), F32),
        input_output_aliases=aliases,
        compiler_params=_params("arbitrary", "arbitrary"),
        name="norm_mod_rows",
    )(*args)


def _final_norm_body(x_ref, g_ref, o_ref):
    o_ref[0] = _rms(x_ref[0], g_ref[...])


def _final_norm(x, g, *, tt):
    B, T, D = x.shape
    return pl.pallas_call(
        _final_norm_body,
        grid=(B, T // tt),
        in_specs=[pl.BlockSpec((1, tt, D), lambda b, t: (b, t, 0)),
                  pl.BlockSpec((1, D), lambda b, t: (0, 0))],
        out_specs=pl.BlockSpec((1, tt, D), lambda b, t: (b, t, 0)),
        out_shape=jax.ShapeDtypeStruct((B, T, D), F32),
        compiler_params=_params("arbitrary", "arbitrary"),
        name="final_norm",
    )(x, g)


def _softplus(x):
    return jnp.maximum(x, 0.0) + jnp.log1p(jnp.exp(-jnp.abs(x)))


def _expm1(x):
    u = jnp.exp(x)
    return jnp.where(u == 1.0, x, jnp.where(u == 0.0, -1.0, (u - 1.0) * x / jnp.log(u)))


def _block_diag_dot(xc, w_ref):
    nb = w_ref.shape[0]
    outs = []
    for n in range(nb):
        xb = xc[:, n * RNN_BLOCK:(n + 1) * RNN_BLOCK].astype(BF16)
        outs.append(_dot(xb, w_ref[n].astype(BF16)))
    return jnp.concatenate(outs, axis=-1)


def _lru_gates(xc, wa_ref, ba, wx_ref, bx, lam):
    r = jax.nn.sigmoid(_block_diag_dot(xc, wa_ref) + ba)
    ig = jax.nn.sigmoid(_block_diag_dot(xc, wx_ref) + bx)
    log_a = (-LRU_C * r) * _softplus(-lam)
    a = jnp.exp(log_a)
    mult = jnp.sqrt(-_expm1(2.0 * log_a))
    return a, mult, ig


HIST = CONV_W - 1
EXT_PAD = 8


def _rglru_prompt_body(u_ref, gt_ref, cw_ref, cb_ref, wa_ref, ba_ref, wx_ref, bx_ref, lam_ref,
                       y_ref, hl_ref, nh_ref, ext_s, a_s, x_s, h_s):
    c = pl.program_id(1)
    tc = u_ref.shape[1]

    @pl.when(c == 0)
    def _():
        ext_s[0:EXT_PAD, :] = jnp.zeros((EXT_PAD, ext_s.shape[1]), F32)
        h_s[...] = jnp.zeros(h_s.shape, F32)

    @pl.when(c > 0)
    def _():
        ext_s[0:EXT_PAD, :] = ext_s[tc:tc + EXT_PAD, :]

    ext_s[EXT_PAD:EXT_PAD + tc, :] = u_ref[0]
    xc = cb_ref[...]
    for j in range(CONV_W):
        off = EXT_PAD - HIST + j
        xc = xc + ext_s[off:off + tc, :] * cw_ref[j:j + 1, :]
    a, mult, ig = _lru_gates(xc, wa_ref, ba_ref[...], wx_ref, bx_ref[...], lam_ref[...])
    row = c * tc + lax.broadcasted_iota(I32, (tc, 1), 0)
    mult = jnp.where(row == 0, 1.0, mult)
    a_s[...] = a
    x_s[...] = mult * ig * xc

    def step(t, h):
        h = a_s[pl.ds(t, 1), :] * h + x_s[pl.ds(t, 1), :]
        x_s[pl.ds(t, 1), :] = h
        return h

    h = lax.fori_loop(0, tc, step, h_s[0:1, :], unroll=8)
    h_s[0:1, :] = h
    y_ref[0] = (x_s[...] * jax.nn.gelu(gt_ref[0])).astype(y_ref.dtype)

    @pl.when(c == pl.num_programs(1) - 1)
    def _():
        hl_ref[0] = h
        nh_ref[0] = ext_s[EXT_PAD + tc - HIST:EXT_PAD + tc, :]


def _rglru_prompt(zug, P, l, *, tc):
    B, T, C2 = zug.shape
    C = C2 // 2
    nb = C // RNN_BLOCK
    vec = lambda: pl.BlockSpec((None, 1, C), lambda b, c: (l, 0, 0))
    blk = lambda: pl.BlockSpec((None, nb, RNN_BLOCK, RNN_BLOCK), lambda b, c: (l, 0, 0, 0))
    return pl.pallas_call(
        _rglru_prompt_body,
        grid=(B, T // tc),
        in_specs=[pl.BlockSpec((1, tc, C), lambda b, c: (b, c, 0)),
                  pl.BlockSpec((1, tc, C), lambda b, c: (b, c, 1)),
                  pl.BlockSpec((None, CONV_W, C), lambda b, c: (l, 0, 0)),
                  vec(), blk(), vec(), blk(), vec(), vec()],
        out_specs=[pl.BlockSpec((1, tc, C), lambda b, c: (b, c, 0)),
                   pl.BlockSpec((1, 1, C), lambda b, c: (b, 0, 0)),
                   pl.BlockSpec((1, HIST, C), lambda b, c: (b, 0, 0))],
        out_shape=[jax.ShapeDtypeStruct((B, T, C), BF16),
                   jax.ShapeDtypeStruct((B, 1, C), F32),
                   jax.ShapeDtypeStruct((B, HIST, C), F32)],
        scratch_shapes=[pltpu.VMEM((tc + EXT_PAD, C), F32), pltpu.VMEM((tc, C), F32),
                        pltpu.VMEM((tc, C), F32), pltpu.VMEM((8, C), F32)],
        compiler_params=_params("arbitrary", "arbitrary"),
        name="rglru_prompt",
    )(zug, zug, P["conv_w"], P["conv_b3"], P["lru_wa"], P["lru_ba3"], P["lru_wx"], P["lru_bx3"],
      P["lru_lambda3"])


def _rglru_sample_body(u_ref, gt_ref, hist_ref, h0_ref, cw_ref, cb_ref, wa_ref, ba_ref, wx_ref, bx_ref,
                       lam_ref, y_ref, h_ref):
    u = u_ref[...]
    xc = cb_ref[...]
    for j in range(HIST):
        xc = xc + hist_ref[j] * cw_ref[j:j + 1, :]
    xc = xc + u * cw_ref[HIST:HIST + 1, :]
    a, mult, ig = _lru_gates(xc, wa_ref, ba_ref[...], wx_ref, bx_ref[...], lam_ref[...])
    h = a * h0_ref[...] + mult * ig * xc
    h_ref[...] = h
    y_ref[...] = (h * jax.nn.gelu(gt_ref[...])).astype(y_ref.dtype)


def _rglru_sample(zug, hist_t, h0, P, l):
    Bs, C2 = zug.shape
    C = C2 // 2
    nb = C // RNN_BLOCK
    vec = lambda: pl.BlockSpec((None, 1, C), lambda i: (l, 0, 0))
    blk = lambda: pl.BlockSpec((None, nb, RNN_BLOCK, RNN_BLOCK), lambda i: (l, 0, 0, 0))
    return pl.pallas_call(
        _rglru_sample_body,
        grid=(1,),
        in_specs=[pl.BlockSpec((Bs, C), lambda i: (0, 0)),
                  pl.BlockSpec((Bs, C), lambda i: (0, 1)),
                  pl.BlockSpec((HIST, Bs, C), lambda i: (0, 0, 0)),
                  pl.BlockSpec((Bs, C), lambda i: (0, 0)),
                  pl.BlockSpec((None, CONV_W, C), lambda i: (l, 0, 0)),
                  vec(), blk(), vec(), blk(), vec(), vec()],
        out_specs=[pl.BlockSpec((Bs, C), lambda i: (0, 0)),
                   pl.BlockSpec((Bs, C), lambda i: (0, 0))],
        out_shape=[jax.ShapeDtypeStruct((Bs, C), BF16), jax.ShapeDtypeStruct((Bs, C), F32)],
        compiler_params=_params("arbitrary"),
        name="rglru_sample",
    )(zug, zug, hist_t, h0, P["conv_w"], P["conv_b3"], P["lru_wa"], P["lru_ba3"], P["lru_wx"],
      P["lru_bx3"], P["lru_lambda3"])


def _rel_bucket(dist):
    n = jnp.maximum(dist, 0)
    max_exact = N_BUCKETS // 2
    nf = jnp.maximum(n, 1).astype(F32)
    large = max_exact + (jnp.log(nf / max_exact) / math.log(MAX_DISTANCE / max_exact)
                         * (N_BUCKETS - max_exact)).astype(I32)
    large = jnp.minimum(large, N_BUCKETS - 1)
    return jnp.where(n < max_exact, n, large)


def _sort_key(x):
    bits = pltpu.bitcast(x, I32)
    return jnp.where(bits < 0, bits ^ jnp.int32(0x7FFFFFFF), bits)


def _kth_largest_key(count_ge, shape, k):
    c = count_ge(jnp.zeros(shape, I32))
    t = jnp.where(c >= k, jnp.int32(0), jnp.int32(INT_MIN))
    for bit in range(30, -1, -1):
        cand = t + jnp.int32(1 << bit)
        c = count_ge(cand)
        t = jnp.where(c >= k, cand, t)
    return t


KB = 2 * Q_BLOCK


def _attn_prompt_body(rb_ref, q_ref, k_ref, v_ref, qi_ref, ki_ref, wit_ref, o_ref,
                      key_s, msk_s, bias_s, qis_s, m_s, l_s, al_s, acc_s, lg_s, p_s, *, n_sel, n_heads, n_idx_heads):
    b = pl.program_id(0)
    i = pl.program_id(1)
    QB = Q_BLOCK
    n_groups = n_heads // KV_GROUP
    GQ = KV_GROUP * QB
    npair = (i + 2) // 2
    s_io = lax.broadcasted_iota(I32, (QB, QB), 0)
    t_io = lax.broadcasted_iota(I32, (QB, QB), 1)

    @pl.when((b == 0) & (i == 0))
    def _():
        for d in range(2):
            bucket = _rel_bucket(t_io - s_io + d * QB)
            for h in range(n_heads):
                acc = jnp.zeros((QB, QB), F32)
                for bb in range(N_BUCKETS):
                    acc = jnp.where(bucket == bb, rb_ref[bb, h], acc)
                bias_s[d, h] = acc
        for h in range(n_heads):
            bias_s[2, h] = jnp.full((QB, QB), rb_ref[N_BUCKETS - 1, h], F32)

    def causal_masked(j):
        off = jnp.where(j < i, QB, jnp.where(j == i, 0, -QB))
        return s_io > t_io + off

    for h in range(n_idx_heads):
        qis_s[h * QB:(h + 1) * QB, :] = qi_ref[0, :, h * IDX_DIM:(h + 1) * IDX_DIM]
    wt = wit_ref[0] * (n_idx_heads * IDX_DIM) ** -0.5

    def score_pair(jj, carry):
        ks = pl.multiple_of(jj * KB, KB)
        d = _dot_nt(ki_ref[0, pl.ds(ks, KB), :], qis_s[...])
        st = jnp.zeros((KB, QB), F32)
        for h in range(n_idx_heads):
            st = st + jnp.maximum(d[:, h * QB:(h + 1) * QB], 0.0) * wt[h:h + 1, :]
        for r in range(2):
            blk = jnp.where(causal_masked(2 * jj + r), NEG_INF, st[r * QB:(r + 1) * QB])
            key_s[pl.ds(ks + r * QB, QB), :] = _sort_key(blk)
        return carry

    lax.fori_loop(0, npair, score_pair, 0)

    def count(pred):
        def body(jj, acc):
            blk = key_s[pl.ds(pl.multiple_of(jj * KB, KB), KB), :]
            return acc + jnp.sum(jnp.where(pred(blk), 1.0, 0.0).reshape(KB // 32, 4, 8, QB), axis=0)
        acc = lax.fori_loop(0, npair, body, jnp.zeros((4, 8, QB), F32))
        return jnp.sum(acc.reshape(32, QB), axis=0, keepdims=True)

    thr = _kth_largest_key(lambda cand: count(lambda blk: blk >= cand), (1, QB), n_sel)
    need = n_sel - count(lambda blk: blk > thr)
    r_io = lax.broadcasted_iota(I32, (KB, KB), 0)
    c_io = lax.broadcasted_iota(I32, (KB, KB), 1)
    ltri = jnp.where(c_io <= r_io, 1.0, 0.0).astype(BF16)

    def mask_pair(jj, carry):
        ks = pl.multiple_of(jj * KB, KB)
        blk = key_s[pl.ds(ks, KB), :]
        eq = blk == thr
        rank = carry + _dot(ltri, jnp.where(eq, 1.0, 0.0).astype(BF16))
        keep = jnp.where(blk > thr, 1.0, jnp.where(eq & (rank <= need), 1.0, 0.0))
        for r in range(2):
            kr = jnp.where(causal_masked(2 * jj + r), 0.0, keep[r * QB:(r + 1) * QB])
            msk_s[pl.ds(ks + r * QB, QB), :] = jnp.where(kr > 0.0, 0.0, NEG_INF)
        return rank[KB - 1:KB, :]

    lax.fori_loop(0, npair, mask_pair, jnp.zeros((1, QB), F32))

    scale = HEAD_DIM ** -0.5
    m_s[...] = jnp.full(m_s.shape, NEG_INF, F32)
    l_s[...] = jnp.zeros(l_s.shape, F32)
    acc_s[...] = jnp.zeros(acc_s.shape, F32)

    def kv_pair(jj, carry):
        ks = pl.multiple_of(jj * KB, KB)
        mk = msk_s[pl.ds(ks, KB), :]
        mk = jnp.concatenate([mk] * KV_GROUP, axis=1)
        dsel = [jnp.clip(i - 2 * jj - r, 0, 2) for r in range(2)]
        for g in range(n_groups):
            heads = [g * KV_GROUP + r for r in range(KV_GROUP)]
            qp = jnp.concatenate([q_ref[0, :, h * HEAD_DIM:(h + 1) * HEAD_DIM] for h in heads], axis=0)
            kb = k_ref[0, pl.ds(ks, KB), g * HEAD_DIM:(g + 1) * HEAD_DIM]
            lg_s[g] = _dot_nt(kb, qp)
        for g in range(n_groups):
            heads = [g * KV_GROUP + r for r in range(KV_GROUP)]
            bias = jnp.concatenate(
                [jnp.concatenate([bias_s[dsel[r], h] for h in heads], axis=1) for r in range(2)], axis=0)
            lt = lg_s[g] * scale + bias + mk
            m_old = m_s[g]
            m_new = jnp.maximum(m_old, jnp.max(jnp.max(lt.reshape(4, KB // 4, GQ), axis=0), axis=0, keepdims=True))
            m_fin = jnp.where(m_new == NEG_INF, 0.0, m_new)
            alpha = jnp.exp(m_old - m_fin)
            p = jnp.exp(lt - m_fin)
            l_s[g] = alpha * l_s[g] + jnp.sum(jnp.sum(p.reshape(4, KB // 4, GQ), axis=0), axis=0, keepdims=True)
            al_s[g] = alpha
            p_s[g] = p.astype(BF16)
            m_s[g] = m_new
        for g in range(n_groups):
            vb = v_ref[0, pl.ds(ks, KB), g * HEAD_DIM:(g + 1) * HEAD_DIM]
            acc_s[g] = al_s[g] * acc_s[g] + _dot_tn(vb, p_s[g])
        return carry

    lax.fori_loop(0, npair, kv_pair, 0)
    for g in range(n_groups):
        ot = acc_s[g] / l_s[g]
        for r in range(KV_GROUP):
            h = g * KV_GROUP + r
            o_ref[0, :, h * HEAD_DIM:(h + 1) * HEAD_DIM] = ot[:, r * QB:(r + 1) * QB].T.astype(o_ref.dtype)


def _attn_prompt(q, k, v, qi, ki, wit, rel_bias):
    B, S, HD = q.shape
    n_heads = HD // HEAD_DIM
    n_groups = n_heads // KV_GROUP
    KD = k.shape[2]
    n_idx_heads = wit.shape[1]
    n_sel = min(TOPK_MAX, S // 4)
    QB = Q_BLOCK
    assert S % KB == 0
    return pl.pallas_call(
        functools.partial(_attn_prompt_body, n_sel=n_sel, n_heads=n_heads, n_idx_heads=n_idx_heads),
        grid=(B, S // QB),
        in_specs=[pl.BlockSpec(memory_space=pltpu.SMEM),
                  pl.BlockSpec((1, QB, HD), lambda b, i: (b, i, 0)),
                  pl.BlockSpec((1, S, KD), lambda b, i: (b, 0, 0)),
                  pl.BlockSpec((1, S, KD), lambda b, i: (b, 0, 0)),
                  pl.BlockSpec((1, QB, qi.shape[2]), lambda b, i: (b, i, 0)),
                  pl.BlockSpec((1, S, IDX_DIM), lambda b, i: (b, 0, 0)),
                  pl.BlockSpec((1, n_idx_heads, QB), lambda b, i: (b, 0, i))],
        out_specs=pl.BlockSpec((1, QB, HD), lambda b, i: (b, i, 0)),
        out_shape=jax.ShapeDtypeStruct((B, S, HD), BF16),
        scratch_shapes=[pltpu.VMEM((S, QB), I32), pltpu.VMEM((S, QB), F32),
                        pltpu.VMEM((3, n_heads, QB, QB), F32),
                        pltpu.VMEM((n_idx_heads * QB, IDX_DIM), BF16),
                        pltpu.VMEM((n_groups, 1, KV_GROUP * QB), F32),
                        pltpu.VMEM((n_groups, 1, KV_GROUP * QB), F32),
                        pltpu.VMEM((n_groups, 1, KV_GROUP * QB), F32),
                        pltpu.VMEM((n_groups, HEAD_DIM, KV_GROUP * QB), F32),
                        pltpu.VMEM((n_groups, KB, KV_GROUP * QB), F32),
                        pltpu.VMEM((n_groups, KB, KV_GROUP * QB), BF16)],
        compiler_params=_params("arbitrary", "arbitrary"),
        name="attn_prompt",
    )(rel_bias, q, k, v, qi, ki, wit)


SCORE_ROWS = 24


def _smp_scores_body(pt_ref, qi_ref, wcol_ref, kin_ref, *refs, n_pages, n_idx_heads):
    pages = refs[:n_pages]
    out_ref = refs[n_pages]
    q16 = qi_ref[0].astype(BF16)
    wcol = wcol_ref[0] * (n_idx_heads * IDX_DIM) ** -0.5
    rows = []
    for p in range(n_pages):
        d = _dot(q16, pages[p][...].astype(BF16))
        rows.append(jnp.sum(jnp.maximum(d, 0.0) * wcol, axis=0, keepdims=True))
    kin = kin_ref[0].astype(BF16).astype(F32)
    dn = jnp.sum(q16.astype(F32) * kin, axis=1, keepdims=True)
    snew = jnp.sum(jnp.maximum(dn, 0.0) * wcol, axis=0, keepdims=True)
    lane = lax.broadcasted_iota(I32, (1, PAGE), 1)
    rows.append(jnp.where(lane == 0, snew, NEG_INF))
    rows.append(jnp.full((SCORE_ROWS - n_pages - 1, PAGE), NEG_INF, F32))
    out_ref[0] = jnp.concatenate(rows, axis=0)


def _smp_scores(page_table, qi3, wcol, kin3, cache_idx_kt, layer):
    Bs, n_pages = page_table.shape
    n_idx_heads = qi3.shape[1]
    page_spec = lambda p: pl.BlockSpec((None, None, IDX_DIM, PAGE), lambda b, pt: (layer, pt[b, p], 0, 0))
    grid_spec = pltpu.PrefetchScalarGridSpec(
        num_scalar_prefetch=1,
        grid=(Bs,),
        in_specs=[pl.BlockSpec((1, n_idx_heads, IDX_DIM), lambda b, pt: (b, 0, 0)),
                  pl.BlockSpec((1, n_idx_heads, 1), lambda b, pt: (b, 0, 0)),
                  pl.BlockSpec((1, 1, IDX_DIM), lambda b, pt: (b, 0, 0))]
        + [page_spec(p) for p in range(n_pages)],
        out_specs=pl.BlockSpec((1, SCORE_ROWS, PAGE), lambda b, pt: (b, 0, 0)),
    )
    return pl.pallas_call(
        functools.partial(_smp_scores_body, n_pages=n_pages, n_idx_heads=n_idx_heads),
        grid_spec=grid_spec,
        out_shape=jax.ShapeDtypeStruct((Bs, SCORE_ROWS, PAGE), F32),
        compiler_params=_params("arbitrary"),
        name="smp_scores",
    )(page_table, qi3, wcol, kin3, *([cache_idx_kt] * n_pages))


def _smp_select_body(sc_ref, o_ref, key_s, *, n_sel):
    Bs, W = sc_ref.shape
    key_s[...] = _sort_key(sc_ref[...])

    def count_ge(cand):
        return jnp.sum(jnp.where(key_s[...] >= cand, 1.0, 0.0), axis=1, keepdims=True)

    thr = _kth_largest_key(count_ge, (Bs, 1), n_sel)
    n_gt = jnp.sum(jnp.where(key_s[...] > thr, 1.0, 0.0), axis=1, keepdims=True)
    need = n_sel - n_gt
    r_io = lax.broadcasted_iota(I32, (LANES, LANES), 0)
    c_io = lax.broadcasted_iota(I32, (LANES, LANES), 1)
    utri = jnp.where(r_io <= c_io, 1.0, 0.0).astype(BF16)
    carry = jnp.zeros((Bs, 1), F32)
    for j in range(W // LANES):
        blk = key_s[:, j * LANES:(j + 1) * LANES]
        eq = blk == thr
        rank = carry + _dot(jnp.where(eq, 1.0, 0.0).astype(BF16), utri)
        keep = jnp.where(blk > thr, 1.0, jnp.where(eq & (rank <= need), 1.0, 0.0))
        o_ref[:, j * LANES:(j + 1) * LANES] = keep
        carry = rank[:, LANES - 1:LANES]


def _smp_select(scores, n_sel):
    Bs, W = scores.shape
    return pl.pallas_call(
        functools.partial(_smp_select_body, n_sel=n_sel),
        grid=(1,),
        in_specs=[pl.BlockSpec((Bs, W), lambda i: (0, 0))],
        out_specs=pl.BlockSpec((Bs, W), lambda i: (0, 0)),
        out_shape=jax.ShapeDtypeStruct((Bs, W), F32),
        scratch_shapes=[pltpu.VMEM((Bs, W), I32)],
        compiler_params=_params("arbitrary"),
        name="smp_select",
    )(scores)


def _smp_attn_body(pt_ref, rbt_ref, q_ref, kn_ref, vn_ref, sel_ref, *refs, n_pages, n_heads, n_kv):
    kpages = refs[:n_pages]
    vpages = refs[n_pages:2 * n_pages]
    o_ref = refs[2 * n_pages]
    lg_s, bias_s, expand_s = refs[2 * n_pages + 1:]
    R = PAGE * n_kv
    row = lax.broadcasted_iota(I32, (1, R), 1)
    head = lax.broadcasted_iota(I32, (n_heads, 1), 0)
    own_group = (row % n_kv) == (head // KV_GROUP)

    @pl.when(pl.program_id(0) == 0)
    def _():
        k_io = lax.broadcasted_iota(I32, (PAGE, R), 0)
        r_io = lax.broadcasted_iota(I32, (PAGE, R), 1)
        expand_s[...] = jnp.where(r_io // n_kv == k_io, 1.0, 0.0).astype(BF16)
        bucket = _rel_bucket(PAGE - row // n_kv)
        acc = jnp.zeros((n_heads, R), F32)
        for bb in range(N_BUCKETS):
            acc = jnp.where(bucket == bb, rbt_ref[:, bb:bb + 1], acc)
        bias_s[...] = acc

    q16 = q_ref[0]
    scale = HEAD_DIM ** -0.5
    far_bias = rbt_ref[:, N_BUCKETS - 1:N_BUCKETS]
    sel = sel_ref[0]
    selx = _dot(sel.astype(BF16), expand_s[...])

    m = jnp.full((n_heads, LANES), NEG_INF, F32)
    for p in range(n_pages):
        lt = _dot_nt(q16, kpages[p][...].astype(BF16)) * scale
        lt = lt + (bias_s[...] if p == n_pages - 1 else far_bias)
        lt = jnp.where(own_group, jnp.where(selx[p:p + 1, :] > 0.5, lt, NEG_INF), NEG_INF)
        lg_s[p] = lt
        for c in range(R // LANES):
            m = jnp.maximum(m, lt[:, c * LANES:(c + 1) * LANES])
    m = jnp.max(m, axis=1, keepdims=True)
    qf = q16.astype(F32)
    ln = jnp.sum(qf * kn_ref[0].astype(BF16).astype(F32), axis=1, keepdims=True) * scale + rbt_ref[:, 0:1]
    ln = jnp.where(sel[n_pages:n_pages + 1, 0:1] > 0.5, ln, NEG_INF)
    m = jnp.maximum(m, ln)

    pn = jnp.exp(ln - m)
    l = pn
    acc = pn.astype(BF16).astype(F32) * vn_ref[0].astype(BF16).astype(F32)
    for p in range(n_pages):
        pr = jnp.exp(lg_s[p] - m)
        l = l + jnp.sum(pr, axis=1, keepdims=True)
        acc = acc + _dot(pr.astype(BF16), vpages[p][...].astype(BF16))
    o_ref[0] = (acc / l).astype(o_ref.dtype)


def _smp_attn(page_table, rbt_pad, q3, kn_rep, vn_rep, sel3, cache_k4, cache_v4, layer, n_kv):
    Bs, n_pages = page_table.shape
    n_heads = q3.shape[1]
    R = PAGE * n_kv
    page_spec = lambda p: pl.BlockSpec((None, None, R, HEAD_DIM), lambda b, pt: (layer, pt[b, p], 0, 0))
    head_spec = lambda: pl.BlockSpec((1, n_heads, HEAD_DIM), lambda b, pt: (b, 0, 0))
    grid_spec = pltpu.PrefetchScalarGridSpec(
        num_scalar_prefetch=1,
        grid=(Bs,),
        in_specs=[pl.BlockSpec((n_heads, LANES), lambda b, pt: (0, 0)),
                  head_spec(), head_spec(), head_spec(),
                  pl.BlockSpec((1, SCORE_ROWS, PAGE), lambda b, pt: (b, 0, 0))]
        + [page_spec(p) for p in range(n_pages)] * 2,
        out_specs=head_spec(),
        scratch_shapes=[pltpu.VMEM((n_pages, n_heads, R), F32), pltpu.VMEM((n_heads, R), F32),
                        pltpu.VMEM((PAGE, R), BF16)],
    )
    return pl.pallas_call(
        functools.partial(_smp_attn_body, n_pages=n_pages, n_heads=n_heads, n_kv=n_kv),
        grid_spec=grid_spec,
        out_shape=jax.ShapeDtypeStruct((Bs, n_heads, HEAD_DIM), BF16),
        compiler_params=_params("arbitrary"),
        name="smp_attn",
    )(page_table, rbt_pad, q3, kn_rep, vn_rep, sel3, *([cache_k4] * n_pages), *([cache_v4] * n_pages))


def _merge_body(yr_ref, ya_ref, wa_ref, wb_ref, ga_ref, gb_ref, o_ref):
    ba = _dot(yr_ref[...], wa_ref[...].astype(BF16))
    bb = _dot(ya_ref[...], wb_ref[...].astype(BF16))
    o_ref[...] = (jax.nn.sigmoid(ga_ref[...]) * ba + jax.nn.sigmoid(gb_ref[...]) * bb).astype(o_ref.dtype)


def _merge(yr, ya, wa, wb, gab, l, *, tm, tn):
    M, K = yr.shape
    D = wa.shape[2]
    nj = D // tn
    return pl.pallas_call(
        _merge_body,
        grid=(nj, M // tm),
        in_specs=[pl.BlockSpec((tm, K), lambda j, i: (i, 0)),
                  pl.BlockSpec((tm, K), lambda j, i: (i, 0)),
                  pl.BlockSpec((None, K, tn), lambda j, i: (l, 0, j)),
                  pl.BlockSpec((None, K, tn), lambda j, i: (l, 0, j)),
                  pl.BlockSpec((tm, tn), lambda j, i: (i, j)),
                  pl.BlockSpec((tm, tn), lambda j, i: (i, nj + j))],
        out_specs=pl.BlockSpec((tm, tn), lambda j, i: (i, j)),
        out_shape=jax.ShapeDtypeStruct((M, D), BF16),
        compiler_params=_params("arbitrary", "arbitrary"),
        name="merge",
    )(yr, ya, wa, wb, gab, gab)


def _proj_res_body(a_ref, w_ref, x_ref, g_ref, o_ref):
    o_ref[0] = x_ref[0] + g_ref[0] * _dot(a_ref[...], w_ref[...].astype(BF16))


def _proj_res(a, w, l, x, mod, g_chunk, *, tm, tn):
    B, T, D = x.shape
    K = a.shape[1]
    tpb = T // tm
    nj = D // tn
    per_tok = mod.shape[1] == T
    tg = tm if per_tok else 1
    gmap = (lambda j, i: (i // tpb, i % tpb, g_chunk * nj + j)) if per_tok else \
        (lambda j, i: (i // tpb, 0, g_chunk * nj + j))
    return pl.pallas_call(
        _proj_res_body,
        grid=(nj, B * tpb),
        in_specs=[pl.BlockSpec((tm, K), lambda j, i: (i, 0)),
                  pl.BlockSpec((None, K, tn), lambda j, i: (l, 0, j)),
                  pl.BlockSpec((1, tm, tn), lambda j, i: (i // tpb, i % tpb, j)),
                  pl.BlockSpec((1, tg, tn), gmap)],
        out_specs=pl.BlockSpec((1, tm, tn), lambda j, i: (i // tpb, i % tpb, j)),
        out_shape=jax.ShapeDtypeStruct((B, T, D), F32),
        compiler_params=_params("arbitrary", "arbitrary"),
        name="proj_res",
    )(a, w, x, mod)


def _router_body(h_ref, r_ref, o_ref, *, n_experts):
    lg = _dot(h_ref[...].astype(BF16), r_ref[...].astype(BF16))
    lane = lax.broadcasted_iota(I32, lg.shape, 1)
    lg = jnp.where(lane < n_experts, lg, NEG_INF)
    m1 = jnp.max(lg, axis=1, keepdims=True)
    i1 = jnp.min(jnp.where(lg == m1, lane, LANES), axis=1, keepdims=True)
    rest = jnp.where(lane == i1, NEG_INF, lg)
    m2 = jnp.max(rest, axis=1, keepdims=True)
    i2 = jnp.min(jnp.where(rest == m2, lane, LANES), axis=1, keepdims=True)
    e2 = jnp.exp(m2 - m1)
    den = 1.0 + e2
    o_ref[...] = jnp.where(lane == 0, i1.astype(F32),
                           jnp.where(lane == 1, i2.astype(F32),
                                     jnp.where(lane == 2, 1.0 / den, jnp.where(lane == 3, e2 / den, 0.0))))


def _router(h, router_pad, *, tm, n_experts):
    M, D = h.shape
    return pl.pallas_call(
        functools.partial(_router_body, n_experts=n_experts),
        grid=(M // tm,),
        in_specs=[pl.BlockSpec((tm, D), lambda i: (i, 0)),
                  pl.BlockSpec((D, LANES), lambda i: (0, 0))],
        out_specs=pl.BlockSpec((tm, LANES), lambda i: (i, 0)),
        out_shape=jax.ShapeDtypeStruct((M, LANES), F32),
        compiler_params=_params("arbitrary"),
        name="router",
    )(h, router_pad)


def _ffn_body(h_ref, w1_ref, w3_ref, w2_ref, x_ref, g2_ref, o_ref):
    f = pl.program_id(1)

    @pl.when(f == 0)
    def _():
        o_ref[...] = jnp.zeros(o_ref.shape, F32)

    h = h_ref[...]
    a = _dot(h, w1_ref[...].astype(BF16))
    b = _dot(h, w3_ref[...].astype(BF16))
    act = a * jax.nn.sigmoid(a) * b
    o_ref[0] += _dot(act.astype(BF16), w2_ref[...].astype(BF16))

    @pl.when(f == pl.num_programs(1) - 1)
    def _():
        o_ref[0] = x_ref[0] + g2_ref[0] * o_ref[0]


def _ffn(h, w1, w3, w2, layer_idx, x, mod, g_chunk, *, tm, tf):
    B, T, D = x.shape
    tpb = T // tm
    F = w1.shape[2]
    per_tok = mod.shape[1] == T
    tg = tm if per_tok else 1
    gmap = (lambda i, f: (i // tpb, i % tpb, g_chunk)) if per_tok else (lambda i, f: (i // tpb, 0, g_chunk))
    return pl.pallas_call(
        _ffn_body,
        grid=(B * tpb, F // tf),
        in_specs=[pl.BlockSpec((tm, D), lambda i, f: (i, 0)),
                  pl.BlockSpec((None, D, tf), lambda i, f: (layer_idx, 0, f)),
                  pl.BlockSpec((None, D, tf), lambda i, f: (layer_idx, 0, f)),
                  pl.BlockSpec((None, tf, D), lambda i, f: (layer_idx, f, 0)),
                  pl.BlockSpec((1, tm, D), lambda i, f: (i // tpb, i % tpb, 0), pipeline_mode=pl.Buffered(1)),
                  pl.BlockSpec((1, tg, D), gmap)],
        out_specs=pl.BlockSpec((1, tm, D), lambda i, f: (i // tpb, i % tpb, 0)),
        out_shape=jax.ShapeDtypeStruct((B, T, D), F32),
        compiler_params=_params("arbitrary", "arbitrary"),
        name="ffn_dense",
    )(h, w1, w3, w2, x, mod)


def _moe_chunk(tm, n_f):
    return -(-(-(-tm // n_f)) // 8) * 8


def _moe_body(te_ref, dst_ref, nval_ref, nused_ref, h_hbm, w1_ref, w3_ref, w2_ref, rw_ref, out_hbm,
              xg_s, xb_s, acc_s, os_s, sem_in, sem_out, *, n_tok, n_f, n_tiles, sub):
    i = pl.program_id(0)
    f = pl.program_id(1)
    tm = xb_s.shape[0]
    chunk = _moe_chunk(tm, n_f)
    ts = n_f * chunk

    def gather(base, r):
        d = dst_ref[base + r]
        tok = jnp.maximum(d - jnp.where(d >= n_tok, n_tok, 0), 0)
        pltpu.make_async_copy(h_hbm.at[pl.ds(tok, 1), :], xg_s.at[pl.ds(r, 1), :], sem_in).start()

    def scatter(base, r):
        d = dst_ref[base + r]
        row = jnp.where(d >= 0, d, 2 * n_tok + r)
        pltpu.make_async_copy(os_s.at[pl.ds(r, 1), :], out_hbm.at[pl.ds(row, 1), :], sem_out).start()

    def wait_gather():
        pltpu.make_async_copy(h_hbm.at[pl.ds(0, ts), :], xg_s, sem_in).wait()

    def wait_scatter():
        pltpu.make_async_copy(os_s, out_hbm.at[pl.ds(0, ts), :], sem_out).wait()

    @pl.when((i == 0) & (f == 0))
    def _():
        os_s[...] = jnp.zeros(os_s.shape, F32)

        def start(r, c):
            gather(ts, r)
            return c

        lax.fori_loop(0, ts, start, 0)

    @pl.when(f == 0)
    def _():
        wait_gather()
        xb_s[...] = xg_s[0:tm, :].astype(BF16)
        acc_s[...] = jnp.zeros(acc_s.shape, F32)

    def move_rows():
        for k in range(chunk):
            r = f * chunk + k
            gather((i + 2) * ts, r)
            scatter(i * ts, r)

    def expert_rows(lo):
        x = xb_s[lo:lo + sub, :]
        a = _dot(x, w1_ref[...].astype(BF16))
        b = _dot(x, w3_ref[...].astype(BF16))
        act = a * jax.nn.sigmoid(a) * b
        acc_s[lo:lo + sub, :] += _dot(act.astype(BF16), w2_ref[...].astype(BF16))

    @pl.when(i < nused_ref[0])
    def _():
        move_rows()
        expert_rows(0)
        for lo in range(sub, tm, sub):
            @pl.when(nval_ref[i] > lo)
            def _():
                expert_rows(lo)

    @pl.when(i >= nused_ref[0])
    def _():
        move_rows()

    @pl.when(f == n_f - 1)
    def _():
        wait_scatter()

        @pl.when(i < nused_ref[0])
        def _():
            os_s[0:tm, :] = acc_s[...] * rw_ref[...]

        @pl.when(i == n_tiles - 1)
        def _():
            def start(r, c):
                scatter((i + 1) * ts, r)
                return c

            lax.fori_loop(0, ts, start, 0)
            wait_scatter()
            wait_gather()


def _moe_sparse(h_all, top_i, top_w, w1, w3, w2, layer_idx, *, tm, sub, tf):
    M, D = h_all.shape
    nE, _, F = w1.shape[1:]
    n_asg = 2 * M
    n_tiles = -(-(n_asg + nE * (tm - 1)) // tm)
    m_pad = n_tiles * tm
    nf = F // tf
    ts = nf * _moe_chunk(tm, nf)
    e_flat = top_i.T.reshape(n_asg)
    onehot = (e_flat[:, None] == jnp.arange(nE, dtype=I32)[None, :]).astype(I32)
    cum = jnp.cumsum(onehot, axis=0)
    counts = cum[-1]
    rank = jnp.take_along_axis(cum, e_flat[:, None], axis=1)[:, 0] - 1
    padded = -(-counts // tm) * tm
    ends = jnp.cumsum(padded)
    starts = ends - padded
    pos = starts[e_flat] + rank
    dst = jnp.full((m_pad,), -1, I32).at[pos].set(jnp.arange(n_asg, dtype=I32))
    dst = jnp.pad(dst.reshape(n_tiles, tm), ((1, 1), (0, ts - tm)), constant_values=-1).reshape(-1)
    roww = jnp.zeros((m_pad,), F32).at[pos].set(top_w.T.reshape(n_asg))
    tile_start = jnp.arange(n_tiles, dtype=I32) * tm
    n_used = (ends[-1] // tm).astype(I32)
    tile_e = jnp.minimum(jnp.searchsorted(ends, tile_start, side="right").astype(I32), nE - 1)
    tile_e = jnp.where(tile_start < ends[-1], tile_e, tile_e[jnp.maximum(n_used - 1, 0)])
    nval = jnp.clip(counts[tile_e] - (tile_start - starts[tile_e]), 0, tm).astype(I32)

    def fidx(i, f, nused):
        return jnp.where(i < nused[0], f, nf - 1)

    grid_spec = pltpu.PrefetchScalarGridSpec(
        num_scalar_prefetch=4,
        grid=(n_tiles, nf),
        in_specs=[pl.BlockSpec(memory_space=pl.ANY),
                  pl.BlockSpec((None, None, D, tf), lambda i, f, te, ds_, nv, nu: (layer_idx, te[i], 0, fidx(i, f, nu))),
                  pl.BlockSpec((None, None, D, tf), lambda i, f, te, ds_, nv, nu: (layer_idx, te[i], 0, fidx(i, f, nu))),
                  pl.BlockSpec((None, None, tf, D), lambda i, f, te, ds_, nv, nu: (layer_idx, te[i], fidx(i, f, nu), 0)),
                  pl.BlockSpec((tm, 1), lambda i, f, te, ds_, nv, nu: (i, 0))],
        out_specs=pl.BlockSpec(memory_space=pl.ANY),
        scratch_shapes=[pltpu.VMEM((ts, D), F32), pltpu.VMEM((tm, D), BF16), pltpu.VMEM((tm, D), F32),
                        pltpu.VMEM((ts, D), F32), pltpu.SemaphoreType.DMA(()), pltpu.SemaphoreType.DMA(())],
    )
    return pl.pallas_call(
        functools.partial(_moe_body, n_tok=M, n_f=nf, n_tiles=n_tiles, sub=sub),
        grid_spec=grid_spec,
        out_shape=jax.ShapeDtypeStruct((n_asg + ts, D), F32),
        compiler_params=_params("arbitrary", "arbitrary"),
        name="moe_sparse",
    )(tile_e, dst, nval, n_used.reshape(1), h_all, w1, w3, w2, roww.reshape(m_pad, 1))


def _moe_combine_body(x_ref, g2_ref, a_ref, b_ref, o_ref):
    o_ref[0] = x_ref[0] + g2_ref[0] * (a_ref[...] + b_ref[...])


def _moe_combine(x, mod, g_chunk, y2, row0, n_tok, *, tt):
    B, T, D = x.shape
    per_tok = mod.shape[1] == T
    tg = tt if per_tok else 1
    tpb = T // tt
    assert row0 % tt == 0 and n_tok % tt == 0
    rb0, rb1 = row0 // tt, (n_tok + row0) // tt
    return pl.pallas_call(
        _moe_combine_body,
        grid=(B, tpb),
        in_specs=[pl.BlockSpec((1, tt, D), lambda b, t: (b, t, 0)),
                  pl.BlockSpec((1, tg, D), lambda b, t: (b, t if per_tok else 0, g_chunk)),
                  pl.BlockSpec((tt, D), lambda b, t: (rb0 + b * tpb + t, 0)),
                  pl.BlockSpec((tt, D), lambda b, t: (rb1 + b * tpb + t, 0))],
        out_specs=pl.BlockSpec((1, tt, D), lambda b, t: (b, t, 0)),
        out_shape=jax.ShapeDtypeStruct((B, T, D), F32),
        compiler_params=_params("arbitrary", "arbitrary"),
        name="moe_combine",
    )(x, mod, y2, y2)


def _tile(n, prefs):
    for p in prefs:
        if n % p == 0:
            return p
    return n


def _mixer_half(l, x, mod, P, attend, rglru, kv_prev, *, tt, tm):
    B, T, D = x.shape
    M = B * T
    h = _norm_mod(x, P["norm_g"][l, 0:1], mod, 1, 0, BF16, tt=tt).reshape(M, D)
    w_in = P["w_in_t"]
    (zug,) = _mm(h, w_in, l, 0, 2 * D, [F32], tm=tm, tn=1024, w_nk=True)
    (q,) = _mm(h, w_in, l, 2 * D, D, [BF16], tm=tm, tn=1024, w_nk=True)
    KD = P["n_kv_cols"] // 2
    depth = w_in.shape[0]
    tnk = _tile(KD, (1024,))
    k_all, k16 = _mm_stacked(h, w_in, l, depth, P["col_kv"], KD, kv_prev and kv_prev[0], tm=tm, tn=tnk, w_nk=True)
    v_all, v16 = _mm_stacked(h, w_in, l, depth, P["col_kv"] + KD, KD, kv_prev and kv_prev[1], tm=tm, tn=tnk,
                             w_nk=True)
    (qi,) = _mm(h, w_in, l, P["col_qi"], P["n_qi_cols"], [BF16], tm=tm, tn=1024, w_nk=True)
    (kiwi,) = _mm(h, w_in, l, P["col_ki"], LANES, [F32], tm=tm, tn=LANES, w_nk=True)
    (gab,) = _mm(h, w_in, l, P["col_ki"] + P["n_kiwi"], 2 * D, [F32], tm=tm, tn=1024, w_nk=True)
    y_rnn, h_last, new_hist = rglru(l, zug)
    y_att = attend(l, q, (k_all, v_all), (k16, v16), qi, kiwi)
    merged = _merge(y_rnn.reshape(M, D), y_att.reshape(M, D), P["w_branch_a"], P["w_branch_b"], gab, l,
                    tm=tm, tn=512)
    x = _proj_res(merged, P["w_out"], l, x, mod, 2, tm=tm, tn=1024)
    return x, h_last, new_hist, (k_all, v_all), kiwi


def kernel(x_prompt, x_sample, c_prompt, c_sample, cache_k, cache_v, cache_idx_k, state_h, state_conv,
           page_table, w_ada, b_ada, norm_g, w_in, conv_w, conv_b, lru_wa, lru_ba, lru_wx, lru_bx,
           lru_lambda, w_branch_a, w_branch_b, w_out, rel_bias, ffn_w1, ffn_w3, ffn_w2,
           moe_router, moe_w1, moe_w3, moe_w2, final_g):
    Bp, S, D = x_prompt.shape
    Bs = x_sample.shape[0]
    Mp = Bp * S
    depth = w_in.shape[0]
    n_heads = rel_bias.shape[1]
    n_phys, _, n_kv = cache_k.shape[1:4]
    KD = n_kv * HEAD_DIM
    n_idx_heads = (w_in.shape[2] - (5 * D + 2 * KD + IDX_DIM)) // (IDX_DIM + 1)
    n_pages = page_table.shape[1]
    past = n_pages * PAGE
    col_qi = 3 * D + 2 * KD
    col_ki = col_qi + n_idx_heads * IDX_DIM
    n_kiwi = IDX_DIM + n_idx_heads
    n_experts = moe_router.shape[2]

    P = dict(
        col_kv=3 * D, n_kv_cols=2 * KD, col_qi=col_qi, n_qi_cols=n_idx_heads * IDX_DIM,
        col_ki=col_ki, n_kiwi=n_kiwi, norm_g=norm_g, w_in_t=jnp.swapaxes(w_in, 1, 2), conv_w=conv_w, lru_wa=lru_wa, lru_wx=lru_wx,
        conv_b3=conv_b[:, None, :], lru_ba3=lru_ba[:, None, :], lru_bx3=lru_bx[:, None, :],
        lru_lambda3=lru_lambda[:, None, :],
        w_branch_a=w_branch_a, w_branch_b=w_branch_b, w_out=w_out,
    )
    router_pad = jnp.pad(moe_router, ((0, 0), (0, 0), (0, LANES - n_experts)))
    rbt_pad = jnp.pad(rel_bias.T, ((0, 0), (0, LANES - rel_bias.shape[0])))
    cache_idx_kt = jnp.swapaxes(cache_idx_k, 2, 3)
    cache_k4 = cache_k.reshape(depth, n_phys, PAGE * n_kv, HEAD_DIM)
    cache_v4 = cache_v.reshape(depth, n_phys, PAGE * n_kv, HEAD_DIM)

    n_c = Bp + Bs
    n_c_pad = -(-n_c // 8) * 8
    c_all = jnp.concatenate([c_prompt, c_sample, jnp.zeros((n_c_pad - n_c, D), F32)], axis=0)
    b_ada3 = b_ada[:, None, :]

    def attend_prompt(l, q, kv32, kv16, qi, kiwi):
        ki = kiwi[:, :IDX_DIM].astype(BF16).reshape(Bp, S, IDX_DIM)
        wit = kiwi[:, IDX_DIM:n_kiwi].reshape(Bp, S, n_idx_heads).transpose(0, 2, 1)
        return _attn_prompt(q.reshape(Bp, S, D), kv16[0].reshape(Bp, S, KD), kv16[1].reshape(Bp, S, KD),
                            qi.reshape(Bp, S, -1), ki, wit, rel_bias)

    def attend_sample(l, q, kv32, kv16, qi, kiwi):
        qi3 = qi.reshape(Bs, n_idx_heads, IDX_DIM)
        wcol = kiwi[:, IDX_DIM:n_kiwi].reshape(Bs, n_idx_heads, 1)
        kin3 = kiwi[:, :IDX_DIM].reshape(Bs, 1, IDX_DIM)
        scores = _smp_scores(page_table, qi3, wcol, kin3, cache_idx_kt, l)
        n_sel = min(TOPK_MAX, (past + 1) // 4)
        sel3 = _smp_select(scores.reshape(Bs, SCORE_ROWS * PAGE), n_sel).reshape(Bs, SCORE_ROWS, PAGE)
        q3 = q.reshape(Bs, n_heads, HEAD_DIM)
        kn_rep = jnp.repeat(kv32[0][l].reshape(Bs, n_kv, HEAD_DIM), KV_GROUP, axis=1)
        vn_rep = jnp.repeat(kv32[1][l].reshape(Bs, n_kv, HEAD_DIM), KV_GROUP, axis=1)
        return _smp_attn(page_table, rbt_pad, q3, kn_rep, vn_rep, sel3, cache_k4, cache_v4, l, n_kv)

    xp, xs = x_prompt, x_sample.reshape(1, Bs, D)
    outs_p, outs_s = [], []
    kvp = kvs = None
    for l in range(depth):
        (mod,) = _mm(c_all, w_ada, l, 0, 6 * D, [F32], tm=n_c_pad, tn=1024, bias=b_ada3, silu_in=True)
        mod_p = mod[:Bp].reshape(Bp, 1, 6 * D)
        mod_s = mod[Bp:n_c].reshape(1, Bs, 6 * D)
        hist_t = state_conv[l].transpose(1, 0, 2)

        xp, hp, cp, kvp, kiwip = _mixer_half(
            l, xp, mod_p, P, attend_prompt, lambda l_, zug: _rglru_prompt(zug.reshape(Bp, S, 2 * D), P, l_, tc=256),
            kvp, tt=512, tm=1024)

        def rglru_s(l_, zug, hist_t=hist_t):
            y, h = _rglru_sample(zug, hist_t, state_h[l_], P, l_)
            new_hist = jnp.concatenate([state_conv[l_][:, 1:], zug[:, None, :D]], axis=1)
            return y, h, new_hist

        xs, hs, cs, kvs, kiwis = _mixer_half(l, xs, mod_s, P, attend_sample, rglru_s, kvs, tt=Bs, tm=Bs)

        g2p, g2s = P["norm_g"][l, 1:2], P["norm_g"][l, 1:2]
        if l % 2 == 0:
            h2p = _norm_mod(xp, g2p, mod_p, 4, 3, BF16, tt=512).reshape(Mp, D)
            h2s = _norm_mod(xs, g2s, mod_s, 4, 3, BF16, tt=Bs).reshape(Bs, D)
            xp = _ffn(h2p, ffn_w1, ffn_w3, ffn_w2, l // 2, xp, mod_p, 5, tm=1024, tf=256)
            xs = _ffn(h2s, ffn_w1, ffn_w3, ffn_w2, l // 2, xs, mod_s, 5, tm=Bs, tf=512)
        else:
            n_tok = Mp + Bs
            h_all = _norm_mod_rows(xp, g2p, mod_p, 4, 3, n_tok, 0, None, tt=512)
            h_all = _norm_mod_rows(xs, g2s, mod_s, 4, 3, n_tok, Mp, h_all, tt=Bs)
            route = _router(h_all, router_pad[l // 2], tm=_tile(n_tok, (1024, 640, 512, 256, 128)),
                            n_experts=n_experts)
            y2 = _moe_sparse(h_all, route[:, 0:2].astype(I32), route[:, 2:4], moe_w1, moe_w3, moe_w2, l // 2,
                             tm=1024, sub=512, tf=256)
            xp = _moe_combine(xp, mod_p, 5, y2, 0, n_tok, tt=Bs)
            xs = _moe_combine(xs, mod_s, 5, y2, Mp, n_tok, tt=Bs)

        outs_p.append((kiwip[:, :IDX_DIM].reshape(Bp, S, IDX_DIM), hp.reshape(Bp, D), cp))
        outs_s.append((kiwis[:, :IDX_DIM].reshape(Bs, 1, IDX_DIM), hs, cs))
    y_prompt = _final_norm(xp, final_g[None, :], tt=512)
    y_sample = _final_norm(xs, final_g[None, :], tt=Bs).reshape(Bs, 1, D)
    stack = lambda outs, i: jnp.stack([o[i] for o in outs])
    kv_shape_p = (depth, Bp, S, n_kv, HEAD_DIM)
    kv_shape_s = (depth, Bs, 1, n_kv, HEAD_DIM)
    return (y_prompt, y_sample,
            kvp[0].reshape(kv_shape_p), kvp[1].reshape(kv_shape_p),
            stack(outs_p, 0), stack(outs_p, 1), stack(outs_p, 2),
            kvs[0].reshape(kv_shape_s), kvs[1].reshape(kv_shape_s),
            stack(outs_s, 0), stack(outs_s, 1), stack(outs_s, 2))
```

```python
import functools
import math

import jax
import jax.numpy as jnp
from jax import lax
from jax.experimental import pallas as pl
from jax.experimental.pallas import tpu as pltpu

F32 = jnp.float32
BF16 = jnp.bfloat16
I32 = jnp.int32

VMEM_LIMIT_BYTES = 56 * 1024 * 1024
LANES = 128

EPS = 1e-6
LRU_C = 8.0
RNN_BLOCK = 128
CONV_W = 4
HEAD_DIM = 128
KV_GROUP = 2
IDX_DIM = 64
TOPK_MAX = 256
Q_BLOCK = 128
N_BUCKETS = 32
MAX_DISTANCE = 128
PAGE = 128
NEG_INF = float("-inf")
INT_MIN = -(2 ** 31)


def _params(*sem):
    return pltpu.CompilerParams(dimension_semantics=sem, vmem_limit_bytes=VMEM_LIMIT_BYTES)


def _dot(a, b):
    return jnp.dot(a, b, preferred_element_type=F32)


def _dot_nt(a, b):
    return lax.dot_general(a, b, (((1,), (1,)), ((), ())), preferred_element_type=F32)


def _dot_tn(a, b):
    return lax.dot_general(a, b, (((0,), (0,)), ((), ())), preferred_element_type=F32)


def _mm_body(x_ref, w_ref, *refs, silu_in, has_bias, w_nk):
    x = x_ref[...]
    if silu_in:
        x = x.astype(F32)
        x = x * jax.nn.sigmoid(x)
    acc = _dot_nt(x.astype(BF16), w_ref[0].astype(BF16)) if w_nk else _dot(x.astype(BF16), w_ref[...].astype(BF16))
    outs = refs
    if has_bias:
        acc = acc + refs[0][...]
        outs = refs[1:]
    for o in outs:
        o[...] = acc.astype(o.dtype)


def _w_spec(K, tn, layer, col0, w_nk):
    if w_nk:
        assert col0 % 8 == 0
        return pl.BlockSpec((pl.Element(1), pl.Element(tn), pl.Element(K)),
                            lambda j, i: (layer, pl.multiple_of(col0 + j * tn, 8), 0))
    assert col0 % tn == 0
    return pl.BlockSpec((None, K, tn), lambda j, i: (layer, 0, col0 // tn + j))


def _mm(x, w, layer, col0, ncols, out_dtypes, *, tm, tn, bias=None, silu_in=False, w_nk=False):
    M, K = x.shape
    assert M % tm == 0 and ncols % tn == 0
    in_specs = [pl.BlockSpec((tm, K), lambda j, i: (i, 0)), _w_spec(K, tn, layer, col0, w_nk)]
    args = [x, w]
    if bias is not None:
        in_specs.append(pl.BlockSpec((None, 1, tn), lambda j, i: (layer, 0, col0 // tn + j)))
        args.append(bias)
    outs = pl.pallas_call(
        functools.partial(_mm_body, silu_in=silu_in, has_bias=bias is not None, w_nk=w_nk),
        grid=(ncols // tn, M // tm),
        in_specs=in_specs,
        out_specs=[pl.BlockSpec((tm, tn), lambda j, i: (i, j)) for _ in out_dtypes],
        out_shape=[jax.ShapeDtypeStruct((M, ncols), dt) for dt in out_dtypes],
        compiler_params=_params("arbitrary", "arbitrary"),
        name="mm",
    )(*args)
    return outs


def _mm_stacked_body(x_ref, w_ref, *refs, w_nk):
    o32_ref, o16_ref = refs[-2:]
    acc = _dot_nt(x_ref[...], w_ref[0].astype(BF16)) if w_nk else _dot(x_ref[...], w_ref[...].astype(BF16))
    o32_ref[...] = acc
    o16_ref[...] = acc.astype(BF16)


def _mm_stacked(x, w, layer, depth, col0, ncols, prev, *, tm, tn, w_nk=False):
    M, K = x.shape
    assert M % tm == 0 and ncols % tn == 0
    in_specs = [pl.BlockSpec((tm, K), lambda j, i: (i, 0)), _w_spec(K, tn, layer, col0, w_nk)]
    args = [x, w]
    aliases = {}
    if prev is not None:
        in_specs.append(pl.BlockSpec(memory_space=pl.ANY))
        args.append(prev)
        aliases = {2: 0}
    return pl.pallas_call(
        functools.partial(_mm_stacked_body, w_nk=w_nk),
        grid=(ncols // tn, M // tm),
        in_specs=in_specs,
        out_specs=[pl.BlockSpec((None, tm, tn), lambda j, i: (layer, i, j)),
                   pl.BlockSpec((tm, tn), lambda j, i: (i, j))],
        out_shape=[jax.ShapeDtypeStruct((depth, M, ncols), F32), jax.ShapeDtypeStruct((M, ncols), BF16)],
        input_output_aliases=aliases,
        compiler_params=_params("arbitrary", "arbitrary"),
        name="mm_stacked",
    )(*args)


def _rms(x, g):
    return x * lax.rsqrt(jnp.mean(x * x, axis=-1, keepdims=True) + EPS) * g


def _norm_mod_body(x_ref, g_ref, sc_ref, sh_ref, o_ref):
    y = _rms(x_ref[0], g_ref[...])
    o_ref[0] = (y * (1.0 + sc_ref[0]) + sh_ref[0]).astype(o_ref.dtype)


def _norm_mod(x, g, mod, sc_chunk, sh_chunk, out_dtype, *, tt):
    B, T, D = x.shape
    per_tok = mod.shape[1] == T
    tg = tt if per_tok else 1
    mod_spec = lambda chunk: pl.BlockSpec((1, tg, D), lambda b, t: (b, t if per_tok else 0, chunk))
    return pl.pallas_call(
        _norm_mod_body,
        grid=(B, T // tt),
        in_specs=[pl.BlockSpec((1, tt, D), lambda b, t: (b, t, 0)),
                  pl.BlockSpec((1, D), lambda b, t: (0, 0)),
                  mod_spec(sc_chunk), mod_spec(sh_chunk)],
        out_specs=pl.BlockSpec((1, tt, D), lambda b, t: (b, t, 0)),
        out_shape=jax.ShapeDtypeStruct((B, T, D), out_dtype),
        compiler_params=_params("arbitrary", "arbitrary"),
        name="norm_mod",
    )(x, g, mod, mod)


def _norm_mod_rows_body(x_ref, g_ref, sc_ref, sh_ref, *refs):
    o_ref = refs[-1]
    y = _rms(x_ref[0], g_ref[...])
    o_ref[...] = y * (1.0 + sc_ref[0]) + sh_ref[0]


def _norm_mod_rows(x, g, mod, sc_chunk, sh_chunk, n_rows, row0, prev, *, tt):
    B, T, D = x.shape
    per_tok = mod.shape[1] == T
    tg = tt if per_tok else 1
    tpb = T // tt
    assert row0 % tt == 0
    mod_spec = lambda chunk: pl.BlockSpec((1, tg, D), lambda b, t: (b, t if per_tok else 0, chunk))
    in_specs = [pl.BlockSpec((1, tt, D), lambda b, t: (b, t, 0)),
                pl.BlockSpec((1, D), lambda b, t: (0, 0)),
                mod_spec(sc_chunk), mod_spec(sh_chunk)]
    args = [x, g, mod, mod]
    aliases = {}
    if prev is not None:
        in_specs.append(pl.BlockSpec(memory_space=pl.ANY))
        args.append(prev)
        aliases = {4: 0}
    return pl.pallas_call(
        _norm_mod_rows_body,
        grid=(B, tpb),
        in_specs=in_specs,
        out_specs=pl.BlockSpec((tt, D), lambda b, t: (row0 // tt + b * tpb + t, 0)),
        out_shape=jax.ShapeDtypeStruct((n_rows, D), F32),
        input_output_aliases=aliases,
        compiler_params=_params("arbitrary", "arbitrary"),
        name="norm_mod_rows",
    )(*args)


def _final_norm_body(x_ref, g_ref, o_ref):
    o_ref[0] = _rms(x_ref[0], g_ref[...])


def _final_norm(x, g, *, tt):
    B, T, D = x.shape
    return pl.pallas_call(
        _final_norm_body,
        grid=(B, T // tt),
        in_specs=[pl.BlockSpec((1, tt, D), lambda b, t: (b, t, 0)),
                  pl.BlockSpec((1, D), lambda b, t: (0, 0))],
        out_specs=pl.BlockSpec((1, tt, D), lambda b, t: (b, t, 0)),
        out_shape=jax.ShapeDtypeStruct((B, T, D), F32),
        compiler_params=_params("arbitrary", "arbitrary"),
        name="final_norm",
    )(x, g)


def _softplus(x):
    return jnp.maximum(x, 0.0) + jnp.log1p(jnp.exp(-jnp.abs(x)))


def _expm1(x):
    u = jnp.exp(x)
    return jnp.where(u == 1.0, x, jnp.where(u == 0.0, -1.0, (u - 1.0) * x / jnp.log(u)))


def _block_diag_dot(xc, w_ref):
    nb = w_ref.shape[0]
    outs = []
    for n in range(nb):
        xb = xc[:, n * RNN_BLOCK:(n + 1) * RNN_BLOCK].astype(BF16)
        outs.append(_dot(xb, w_ref[n].astype(BF16)))
    return jnp.concatenate(outs, axis=-1)


def _lru_gates(xc, wa_ref, ba, wx_ref, bx, lam):
    r = jax.nn.sigmoid(_block_diag_dot(xc, wa_ref) + ba)
    ig = jax.nn.sigmoid(_block_diag_dot(xc, wx_ref) + bx)
    log_a = (-LRU_C * r) * _softplus(-lam)
    a = jnp.exp(log_a)
    mult = jnp.sqrt(-_expm1(2.0 * log_a))
    return a, mult, ig


HIST = CONV_W - 1
EXT_PAD = 8


def _rglru_prompt_body(u_ref, gt_ref, cw_ref, cb_ref, wa_ref, ba_ref, wx_ref, bx_ref, lam_ref,
                       y_ref, hl_ref, nh_ref, ext_s, a_s, x_s, h_s):
    c = pl.program_id(1)
    tc = u_ref.shape[1]

    @pl.when(c == 0)
    def _():
        ext_s[0:EXT_PAD, :] = jnp.zeros((EXT_PAD, ext_s.shape[1]), F32)
        h_s[...] = jnp.zeros(h_s.shape, F32)

    @pl.when(c > 0)
    def _():
        ext_s[0:EXT_PAD, :] = ext_s[tc:tc + EXT_PAD, :]

    ext_s[EXT_PAD:EXT_PAD + tc, :] = u_ref[0]
    xc = cb_ref[...]
    for j in range(CONV_W):
        off = EXT_PAD - HIST + j
        xc = xc + ext_s[off:off + tc, :] * cw_ref[j:j + 1, :]
    a, mult, ig = _lru_gates(xc, wa_ref, ba_ref[...], wx_ref, bx_ref[...], lam_ref[...])
    row = c * tc + lax.broadcasted_iota(I32, (tc, 1), 0)
    mult = jnp.where(row == 0, 1.0, mult)
    a_s[...] = a
    x_s[...] = mult * ig * xc

    def step(t, h):
        h = a_s[pl.ds(t, 1), :] * h + x_s[pl.ds(t, 1), :]
        x_s[pl.ds(t, 1), :] = h
        return h

    h = lax.fori_loop(0, tc, step, h_s[0:1, :], unroll=8)
    h_s[0:1, :] = h
    y_ref[0] = (x_s[...] * jax.nn.gelu(gt_ref[0])).astype(y_ref.dtype)

    @pl.when(c == pl.num_programs(1) - 1)
    def _():
        hl_ref[0] = h
        nh_ref[0] = ext_s[EXT_PAD + tc - HIST:EXT_PAD + tc, :]


def _rglru_prompt(zug, P, l, *, tc):
    B, T, C2 = zug.shape
    C = C2 // 2
    nb = C // RNN_BLOCK
    vec = lambda: pl.BlockSpec((None, 1, C), lambda b, c: (l, 0, 0))
    blk = lambda: pl.BlockSpec((None, nb, RNN_BLOCK, RNN_BLOCK), lambda b, c: (l, 0, 0, 0))
    return pl.pallas_call(
        _rglru_prompt_body,
        grid=(B, T // tc),
        in_specs=[pl.BlockSpec((1, tc, C), lambda b, c: (b, c, 0)),
                  pl.BlockSpec((1, tc, C), lambda b, c: (b, c, 1)),
                  pl.BlockSpec((None, CONV_W, C), lambda b, c: (l, 0, 0)),
                  vec(), blk(), vec(), blk(), vec(), vec()],
        out_specs=[pl.BlockSpec((1, tc, C), lambda b, c: (b, c, 0)),
                   pl.BlockSpec((1, 1, C), lambda b, c: (b, 0, 0)),
                   pl.BlockSpec((1, HIST, C), lambda b, c: (b, 0, 0))],
        out_shape=[jax.ShapeDtypeStruct((B, T, C), BF16),
                   jax.ShapeDtypeStruct((B, 1, C), F32),
                   jax.ShapeDtypeStruct((B, HIST, C), F32)],
        scratch_shapes=[pltpu.VMEM((tc + EXT_PAD, C), F32), pltpu.VMEM((tc, C), F32),
                        pltpu.VMEM((tc, C), F32), pltpu.VMEM((8, C), F32)],
        compiler_params=_params("arbitrary", "arbitrary"),
        name="rglru_prompt",
    )(zug, zug, P["conv_w"], P["conv_b3"], P["lru_wa"], P["lru_ba3"], P["lru_wx"], P["lru_bx3"],
      P["lru_lambda3"])


def _rglru_sample_body(u_ref, gt_ref, hist_ref, h0_ref, cw_ref, cb_ref, wa_ref, ba_ref, wx_ref, bx_ref,
                       lam_ref, y_ref, h_ref):
    u = u_ref[...]
    xc = cb_ref[...]
    for j in range(HIST):
        xc = xc + hist_ref[j] * cw_ref[j:j + 1, :]
    xc = xc + u * cw_ref[HIST:HIST + 1, :]
    a, mult, ig = _lru_gates(xc, wa_ref, ba_ref[...], wx_ref, bx_ref[...], lam_ref[...])
    h = a * h0_ref[...] + mult * ig * xc
    h_ref[...] = h
    y_ref[...] = (h * jax.nn.gelu(gt_ref[...])).astype(y_ref.dtype)


def _rglru_sample(zug, hist_t, h0, P, l):
    Bs, C2 = zug.shape
    C = C2 // 2
    nb = C // RNN_BLOCK
    vec = lambda: pl.BlockSpec((None, 1, C), lambda i: (l, 0, 0))
    blk = lambda: pl.BlockSpec((None, nb, RNN_BLOCK, RNN_BLOCK), lambda i: (l, 0, 0, 0))
    return pl.pallas_call(
        _rglru_sample_body,
        grid=(1,),
        in_specs=[pl.BlockSpec((Bs, C), lambda i: (0, 0)),
                  pl.BlockSpec((Bs, C), lambda i: (0, 1)),
                  pl.BlockSpec((HIST, Bs, C), lambda i: (0, 0, 0)),
                  pl.BlockSpec((Bs, C), lambda i: (0, 0)),
                  pl.BlockSpec((None, CONV_W, C), lambda i: (l, 0, 0)),
                  vec(), blk(), vec(), blk(), vec(), vec()],
        out_specs=[pl.BlockSpec((Bs, C), lambda i: (0, 0)),
                   pl.BlockSpec((Bs, C), lambda i: (0, 0))],
        out_shape=[jax.ShapeDtypeStruct((Bs, C), BF16), jax.ShapeDtypeStruct((Bs, C), F32)],
        compiler_params=_params("arbitrary"),
        name="rglru_sample",
    )(zug, zug, hist_t, h0, P["conv_w"], P["conv_b3"], P["lru_wa"], P["lru_ba3"], P["lru_wx"],
      P["lru_bx3"], P["lru_lambda3"])


def _rel_bucket(dist):
    n = jnp.maximum(dist, 0)
    max_exact = N_BUCKETS // 2
    nf = jnp.maximum(n, 1).astype(F32)
    large = max_exact + (jnp.log(nf / max_exact) / math.log(MAX_DISTANCE / max_exact)
                         * (N_BUCKETS - max_exact)).astype(I32)
    large = jnp.minimum(large, N_BUCKETS - 1)
    return jnp.where(n < max_exact, n, large)


def _sort_key(x):
    bits = pltpu.bitcast(x, I32)
    return jnp.where(bits < 0, bits ^ jnp.int32(0x7FFFFFFF), bits)


def _kth_largest_key(count_ge, shape, k):
    c = count_ge(jnp.zeros(shape, I32))
    t = jnp.where(c >= k, jnp.int32(0), jnp.int32(INT_MIN))
    for bit in range(30, -1, -1):
        cand = t + jnp.int32(1 << bit)
        c = count_ge(cand)
        t = jnp.where(c >= k, cand, t)
    return t


KB = 2 * Q_BLOCK


def _attn_prompt_body(rb_ref, q_ref, k_ref, v_ref, qi_ref, ki_ref, wit_ref, o_ref,
                      key_s, msk_s, bias_s, qis_s, m_s, l_s, al_s, acc_s, lg_s, p_s, *, n_sel, n_heads, n_idx_heads):
    b = pl.program_id(0)
    i = pl.program_id(1)
    QB = Q_BLOCK
    n_groups = n_heads // KV_GROUP
    GQ = KV_GROUP * QB
    npair = (i + 2) // 2
    s_io = lax.broadcasted_iota(I32, (QB, QB), 0)
    t_io = lax.broadcasted_iota(I32, (QB, QB), 1)

    @pl.when((b == 0) & (i == 0))
    def _():
        for d in range(2):
            bucket = _rel_bucket(t_io - s_io + d * QB)
            for h in range(n_heads):
                acc = jnp.zeros((QB, QB), F32)
                for bb in range(N_BUCKETS):
                    acc = jnp.where(bucket == bb, rb_ref[bb, h], acc)
                bias_s[d, h] = acc
        for h in range(n_heads):
            bias_s[2, h] = jnp.full((QB, QB), rb_ref[N_BUCKETS - 1, h], F32)

    def causal_masked(j):
        off = jnp.where(j < i, QB, jnp.where(j == i, 0, -QB))
        return s_io > t_io + off

    for h in range(n_idx_heads):
        qis_s[h * QB:(h + 1) * QB, :] = qi_ref[0, :, h * IDX_DIM:(h + 1) * IDX_DIM]
    wt = wit_ref[0] * (n_idx_heads * IDX_DIM) ** -0.5

    def score_pair(jj, carry):
        ks = pl.multiple_of(jj * KB, KB)
        d = _dot_nt(ki_ref[0, pl.ds(ks, KB), :], qis_s[...])
        st = jnp.zeros((KB, QB), F32)
        for h in range(n_idx_heads):
            st = st + jnp.maximum(d[:, h * QB:(h + 1) * QB], 0.0) * wt[h:h + 1, :]
        for r in range(2):
            blk = jnp.where(causal_masked(2 * jj + r), NEG_INF, st[r * QB:(r + 1) * QB])
            key_s[pl.ds(ks + r * QB, QB), :] = _sort_key(blk)
        return carry

    lax.fori_loop(0, npair, score_pair, 0)

    def count(pred):
        def body(jj, acc):
            blk = key_s[pl.ds(pl.multiple_of(jj * KB, KB), KB), :]
            return acc + jnp.sum(jnp.where(pred(blk), 1.0, 0.0).reshape(KB // 32, 4, 8, QB), axis=0)
        acc = lax.fori_loop(0, npair, body, jnp.zeros((4, 8, QB), F32))
        return jnp.sum(acc.reshape(32, QB), axis=0, keepdims=True)

    thr = _kth_largest_key(lambda cand: count(lambda blk: blk >= cand), (1, QB), n_sel)
    need = n_sel - count(lambda blk: blk > thr)
    r_io = lax.broadcasted_iota(I32, (KB, KB), 0)
    c_io = lax.broadcasted_iota(I32, (KB, KB), 1)
    ltri = jnp.where(c_io <= r_io, 1.0, 0.0).astype(BF16)

    def mask_pair(jj, carry):
        ks = pl.multiple_of(jj * KB, KB)
        blk = key_s[pl.ds(ks, KB), :]
        eq = blk == thr
        rank = carry + _dot(ltri, jnp.where(eq, 1.0, 0.0).astype(BF16))
        keep = jnp.where(blk > thr, 1.0, jnp.where(eq & (rank <= need), 1.0, 0.0))
        for r in range(2):
            kr = jnp.where(causal_masked(2 * jj + r), 0.0, keep[r * QB:(r + 1) * QB])
            msk_s[pl.ds(ks + r * QB, QB), :] = jnp.where(kr > 0.0, 0.0, NEG_INF)
        return rank[KB - 1:KB, :]

    lax.fori_loop(0, npair, mask_pair, jnp.zeros((1, QB), F32))

    scale = HEAD_DIM ** -0.5
    m_s[...] = jnp.full(m_s.shape, NEG_INF, F32)
    l_s[...] = jnp.zeros(l_s.shape, F32)
    acc_s[...] = jnp.zeros(acc_s.shape, F32)

    def kv_pair(jj, carry):
        ks = pl.multiple_of(jj * KB, KB)
        mk = msk_s[pl.ds(ks, KB), :]
        mk = jnp.concatenate([mk] * KV_GROUP, axis=1)
        dsel = [jnp.clip(i - 2 * jj - r, 0, 2) for r in range(2)]
        for g in range(n_groups):
            heads = [g * KV_GROUP + r for r in range(KV_GROUP)]
            qp = jnp.concatenate([q_ref[0, :, h * HEAD_DIM:(h + 1) * HEAD_DIM] for h in heads], axis=0)
            kb = k_ref[0, pl.ds(ks, KB), g * HEAD_DIM:(g + 1) * HEAD_DIM]
            lg_s[g] = _dot_nt(kb, qp)
        for g in range(n_groups):
            heads = [g * KV_GROUP + r for r in range(KV_GROUP)]
            bias = jnp.concatenate(
                [jnp.concatenate([bias_s[dsel[r], h] for h in heads], axis=1) for r in range(2)], axis=0)
            lt = lg_s[g] * scale + bias + mk
            m_old = m_s[g]
            m_new = jnp.maximum(m_old, jnp.max(jnp.max(lt.reshape(4, KB // 4, GQ), axis=0), axis=0, keepdims=True))
            m_fin = jnp.where(m_new == NEG_INF, 0.0, m_new)
            alpha = jnp.exp(m_old - m_fin)
            p = jnp.exp(lt - m_fin)
            l_s[g] = alpha * l_s[g] + jnp.sum(jnp.sum(p.reshape(4, KB // 4, GQ), axis=0), axis=0, keepdims=True)
            al_s[g] = alpha
            p_s[g] = p.astype(BF16)
            m_s[g] = m_new
        for g in range(n_groups):
            vb = v_ref[0, pl.ds(ks, KB), g * HEAD_DIM:(g + 1) * HEAD_DIM]
            acc_s[g] = al_s[g] * acc_s[g] + _dot_tn(vb, p_s[g])
        return carry

    lax.fori_loop(0, npair, kv_pair, 0)
    for g in range(n_groups):
        ot = acc_s[g] / l_s[g]
        for r in range(KV_GROUP):
            h = g * KV_GROUP + r
            o_ref[0, :, h * HEAD_DIM:(h + 1) * HEAD_DIM] = ot[:, r * QB:(r + 1) * QB].T.astype(o_ref.dtype)


def _attn_prompt(q, k, v, qi, ki, wit, rel_bias):
    B, S, HD = q.shape
    n_heads = HD // HEAD_DIM
    n_groups = n_heads // KV_GROUP
    KD = k.shape[2]
    n_idx_heads = wit.shape[1]
    n_sel = min(TOPK_MAX, S // 4)
    QB = Q_BLOCK
    assert S % KB == 0
    return pl.pallas_call(
        functools.partial(_attn_prompt_body, n_sel=n_sel, n_heads=n_heads, n_idx_heads=n_idx_heads),
        grid=(B, S // QB),
        in_specs=[pl.BlockSpec(memory_space=pltpu.SMEM),
                  pl.BlockSpec((1, QB, HD), lambda b, i: (b, i, 0)),
                  pl.BlockSpec((1, S, KD), lambda b, i: (b, 0, 0)),
                  pl.BlockSpec((1, S, KD), lambda b, i: (b, 0, 0)),
                  pl.BlockSpec((1, QB, qi.shape[2]), lambda b, i: (b, i, 0)),
                  pl.BlockSpec((1, S, IDX_DIM), lambda b, i: (b, 0, 0)),
                  pl.BlockSpec((1, n_idx_heads, QB), lambda b, i: (b, 0, i))],
        out_specs=pl.BlockSpec((1, QB, HD), lambda b, i: (b, i, 0)),
        out_shape=jax.ShapeDtypeStruct((B, S, HD), BF16),
        scratch_shapes=[pltpu.VMEM((S, QB), I32), pltpu.VMEM((S, QB), F32),
                        pltpu.VMEM((3, n_heads, QB, QB), F32),
                        pltpu.VMEM((n_idx_heads * QB, IDX_DIM), BF16),
                        pltpu.VMEM((n_groups, 1, KV_GROUP * QB), F32),
                        pltpu.VMEM((n_groups, 1, KV_GROUP * QB), F32),
                        pltpu.VMEM((n_groups, 1, KV_GROUP * QB), F32),
                        pltpu.VMEM((n_groups, HEAD_DIM, KV_GROUP * QB), F32),
                        pltpu.VMEM((n_groups, KB, KV_GROUP * QB), F32),
                        pltpu.VMEM((n_groups, KB, KV_GROUP * QB), BF16)],
        compiler_params=_params("arbitrary", "arbitrary"),
        name="attn_prompt",
    )(rel_bias, q, k, v, qi, ki, wit)


SCORE_ROWS = 24


def _smp_scores_body(pt_ref, qi_ref, wcol_ref, kin_ref, *refs, n_pages, n_idx_heads):
    pages = refs[:n_pages]
    out_ref = refs[n_pages]
    q16 = qi_ref[0].astype(BF16)
    wcol = wcol_ref[0] * (n_idx_heads * IDX_DIM) ** -0.5
    rows = []
    for p in range(n_pages):
        d = _dot(q16, pages[p][...].astype(BF16))
        rows.append(jnp.sum(jnp.maximum(d, 0.0) * wcol, axis=0, keepdims=True))
    kin = kin_ref[0].astype(BF16).astype(F32)
    dn = jnp.sum(q16.astype(F32) * kin, axis=1, keepdims=True)
    snew = jnp.sum(jnp.maximum(dn, 0.0) * wcol, axis=0, keepdims=True)
    lane = lax.broadcasted_iota(I32, (1, PAGE), 1)
    rows.append(jnp.where(lane == 0, snew, NEG_INF))
    rows.append(jnp.full((SCORE_ROWS - n_pages - 1, PAGE), NEG_INF, F32))
    out_ref[0] = jnp.concatenate(rows, axis=0)


def _smp_scores(page_table, qi3, wcol, kin3, cache_idx_kt, layer):
    Bs, n_pages = page_table.shape
    n_idx_heads = qi3.shape[1]
    page_spec = lambda p: pl.BlockSpec((None, None, IDX_DIM, PAGE), lambda b, pt: (layer, pt[b, p], 0, 0))
    grid_spec = pltpu.PrefetchScalarGridSpec(
        num_scalar_prefetch=1,
        grid=(Bs,),
        in_specs=[pl.BlockSpec((1, n_idx_heads, IDX_DIM), lambda b, pt: (b, 0, 0)),
                  pl.BlockSpec((1, n_idx_heads, 1), lambda b, pt: (b, 0, 0)),
                  pl.BlockSpec((1, 1, IDX_DIM), lambda b, pt: (b, 0, 0))]
        + [page_spec(p) for p in range(n_pages)],
        out_specs=pl.BlockSpec((1, SCORE_ROWS, PAGE), lambda b, pt: (b, 0, 0)),
    )
    return pl.pallas_call(
        functools.partial(_smp_scores_body, n_pages=n_pages, n_idx_heads=n_idx_heads),
        grid_spec=grid_spec,
        out_shape=jax.ShapeDtypeStruct((Bs, SCORE_ROWS, PAGE), F32),
        compiler_params=_params("arbitrary"),
        name="smp_scores",
    )(page_table, qi3, wcol, kin3, *([cache_idx_kt] * n_pages))


def _smp_select_body(sc_ref, o_ref, key_s, *, n_sel):
    Bs, W = sc_ref.shape
    key_s[...] = _sort_key(sc_ref[...])

    def count_ge(cand):
        return jnp.sum(jnp.where(key_s[...] >= cand, 1.0, 0.0), axis=1, keepdims=True)

    thr = _kth_largest_key(count_ge, (Bs, 1), n_sel)
    n_gt = jnp.sum(jnp.where(key_s[...] > thr, 1.0, 0.0), axis=1, keepdims=True)
    need = n_sel - n_gt
    r_io = lax.broadcasted_iota(I32, (LANES, LANES), 0)
    c_io = lax.broadcasted_iota(I32, (LANES, LANES), 1)
    utri = jnp.where(r_io <= c_io, 1.0, 0.0).astype(BF16)
    carry = jnp.zeros((Bs, 1), F32)
    for j in range(W // LANES):
        blk = key_s[:, j * LANES:(j + 1) * LANES]
        eq = blk == thr
        rank = carry + _dot(jnp.where(eq, 1.0, 0.0).astype(BF16), utri)
        keep = jnp.where(blk > thr, 1.0, jnp.where(eq & (rank <= need), 1.0, 0.0))
        o_ref[:, j * LANES:(j + 1) * LANES] = keep
        carry = rank[:, LANES - 1:LANES]


def _smp_select(scores, n_sel):
    Bs, W = scores.shape
    return pl.pallas_call(
        functools.partial(_smp_select_body, n_sel=n_sel),
        grid=(1,),
        in_specs=[pl.BlockSpec((Bs, W), lambda i: (0, 0))],
        out_specs=pl.BlockSpec((Bs, W), lambda i: (0, 0)),
        out_shape=jax.ShapeDtypeStruct((Bs, W), F32),
        scratch_shapes=[pltpu.VMEM((Bs, W), I32)],
        compiler_params=_params("arbitrary"),
        name="smp_select",
    )(scores)


def _smp_attn_body(pt_ref, rbt_ref, q_ref, kn_ref, vn_ref, sel_ref, *refs, n_pages, n_heads, n_kv):
    kpages = refs[:n_pages]
    vpages = refs[n_pages:2 * n_pages]
    o_ref = refs[2 * n_pages]
    lg_s, bias_s, expand_s = refs[2 * n_pages + 1:]
    R = PAGE * n_kv
    row = lax.broadcasted_iota(I32, (1, R), 1)
    head = lax.broadcasted_iota(I32, (n_heads, 1), 0)
    own_group = (row % n_kv) == (head // KV_GROUP)

    @pl.when(pl.program_id(0) == 0)
    def _():
        k_io = lax.broadcasted_iota(I32, (PAGE, R), 0)
        r_io = lax.broadcasted_iota(I32, (PAGE, R), 1)
        expand_s[...] = jnp.where(r_io // n_kv == k_io, 1.0, 0.0).astype(BF16)
        bucket = _rel_bucket(PAGE - row // n_kv)
        acc = jnp.zeros((n_heads, R), F32)
        for bb in range(N_BUCKETS):
            acc = jnp.where(bucket == bb, rbt_ref[:, bb:bb + 1], acc)
        bias_s[...] = acc

    q16 = q_ref[0]
    scale = HEAD_DIM ** -0.5
    far_bias = rbt_ref[:, N_BUCKETS - 1:N_BUCKETS]
    sel = sel_ref[0]
    selx = _dot(sel.astype(BF16), expand_s[...])

    m = jnp.full((n_heads, LANES), NEG_INF, F32)
    for p in range(n_pages):
        lt = _dot_nt(q16, kpages[p][...].astype(BF16)) * scale
        lt = lt + (bias_s[...] if p == n_pages - 1 else far_bias)
        lt = jnp.where(own_group, jnp.where(selx[p:p + 1, :] > 0.5, lt, NEG_INF), NEG_INF)
        lg_s[p] = lt
        for c in range(R // LANES):
            m = jnp.maximum(m, lt[:, c * LANES:(c + 1) * LANES])
    m = jnp.max(m, axis=1, keepdims=True)
    qf = q16.astype(F32)
    ln = jnp.sum(qf * kn_ref[0].astype(BF16).astype(F32), axis=1, keepdims=True) * scale + rbt_ref[:, 0:1]
    ln = jnp.where(sel[n_pages:n_pages + 1, 0:1] > 0.5, ln, NEG_INF)
    m = jnp.maximum(m, ln)

    pn = jnp.exp(ln - m)
    l = pn
    acc = pn.astype(BF16).astype(F32) * vn_ref[0].astype(BF16).astype(F32)
    for p in range(n_pages):
        pr = jnp.exp(lg_s[p] - m)
        l = l + jnp.sum(pr, axis=1, keepdims=True)
        acc = acc + _dot(pr.astype(BF16), vpages[p][...].astype(BF16))
    o_ref[0] = (acc / l).astype(o_ref.dtype)


def _smp_attn(page_table, rbt_pad, q3, kn_rep, vn_rep, sel3, cache_k4, cache_v4, layer, n_kv):
    Bs, n_pages = page_table.shape
    n_heads = q3.shape[1]
    R = PAGE * n_kv
    page_spec = lambda p: pl.BlockSpec((None, None, R, HEAD_DIM), lambda b, pt: (layer, pt[b, p], 0, 0))
    head_spec = lambda: pl.BlockSpec((1, n_heads, HEAD_DIM), lambda b, pt: (b, 0, 0))
    grid_spec = pltpu.PrefetchScalarGridSpec(
        num_scalar_prefetch=1,
        grid=(Bs,),
        in_specs=[pl.BlockSpec((n_heads, LANES), lambda b, pt: (0, 0)),
                  head_spec(), head_spec(), head_spec(),
                  pl.BlockSpec((1, SCORE_ROWS, PAGE), lambda b, pt: (b, 0, 0))]
        + [page_spec(p) for p in range(n_pages)] * 2,
        out_specs=head_spec(),
        scratch_shapes=[pltpu.VMEM((n_pages, n_heads, R), F32), pltpu.VMEM((n_heads, R), F32),
                        pltpu.VMEM((PAGE, R), BF16)],
    )
    return pl.pallas_call(
        functools.partial(_smp_attn_body, n_pages=n_pages, n_heads=n_heads, n_kv=n_kv),
        grid_spec=grid_spec,
        out_shape=jax.ShapeDtypeStruct((Bs, n_heads, HEAD_DIM), BF16),
        compiler_params=_params("arbitrary"),
        name="smp_attn",
    )(page_table, rbt_pad, q3, kn_rep, vn_rep, sel3, *([cache_k4] * n_pages), *([cache_v4] * n_pages))


def _merge_body(yr_ref, ya_ref, wa_ref, wb_ref, ga_ref, gb_ref, o_ref):
    ba = _dot(yr_ref[...], wa_ref[...].astype(BF16))
    bb = _dot(ya_ref[...], wb_ref[...].astype(BF16))
    o_ref[...] = (jax.nn.sigmoid(ga_ref[...]) * ba + jax.nn.sigmoid(gb_ref[...]) * bb).astype(o_ref.dtype)


def _merge(yr, ya, wa, wb, gab, l, *, tm, tn):
    M, K = yr.shape
    D = wa.shape[2]
    nj = D // tn
    return pl.pallas_call(
        _merge_body,
        grid=(nj, M // tm),
        in_specs=[pl.BlockSpec((tm, K), lambda j, i: (i, 0)),
                  pl.BlockSpec((tm, K), lambda j, i: (i, 0)),
                  pl.BlockSpec((None, K, tn), lambda j, i: (l, 0, j)),
                  pl.BlockSpec((None, K, tn), lambda j, i: (l, 0, j)),
                  pl.BlockSpec((tm, tn), lambda j, i: (i, j)),
                  pl.BlockSpec((tm, tn), lambda j, i: (i, nj + j))],
        out_specs=pl.BlockSpec((tm, tn), lambda j, i: (i, j)),
        out_shape=jax.ShapeDtypeStruct((M, D), BF16),
        compiler_params=_params("arbitrary", "arbitrary"),
        name="merge",
    )(yr, ya, wa, wb, gab, gab)


def _proj_res_body(a_ref, w_ref, x_ref, g_ref, o_ref):
    o_ref[0] = x_ref[0] + g_ref[0] * _dot(a_ref[...], w_ref[...].astype(BF16))


def _proj_res(a, w, l, x, mod, g_chunk, *, tm, tn):
    B, T, D = x.shape
    K = a.shape[1]
    tpb = T // tm
    nj = D // tn
    per_tok = mod.shape[1] == T
    tg = tm if per_tok else 1
    gmap = (lambda j, i: (i // tpb, i % tpb, g_chunk * nj + j)) if per_tok else \
        (lambda j, i: (i // tpb, 0, g_chunk * nj + j))
    return pl.pallas_call(
        _proj_res_body,
        grid=(nj, B * tpb),
        in_specs=[pl.BlockSpec((tm, K), lambda j, i: (i, 0)),
                  pl.BlockSpec((None, K, tn), lambda j, i: (l, 0, j)),
                  pl.BlockSpec((1, tm, tn), lambda j, i: (i // tpb, i % tpb, j)),
                  pl.BlockSpec((1, tg, tn), gmap)],
        out_specs=pl.BlockSpec((1, tm, tn), lambda j, i: (i // tpb, i % tpb, j)),
        out_shape=jax.ShapeDtypeStruct((B, T, D), F32),
        compiler_params=_params("arbitrary", "arbitrary"),
        name="proj_res",
    )(a, w, x, mod)


def _router_body(h_ref, r_ref, o_ref, *, n_experts):
    lg = _dot(h_ref[...].astype(BF16), r_ref[...].astype(BF16))
    lane = lax.broadcasted_iota(I32, lg.shape, 1)
    lg = jnp.where(lane < n_experts, lg, NEG_INF)
    m1 = jnp.max(lg, axis=1, keepdims=True)
    i1 = jnp.min(jnp.where(lg == m1, lane, LANES), axis=1, keepdims=True)
    rest = jnp.where(lane == i1, NEG_INF, lg)
    m2 = jnp.max(rest, axis=1, keepdims=True)
    i2 = jnp.min(jnp.where(rest == m2, lane, LANES), axis=1, keepdims=True)
    e2 = jnp.exp(m2 - m1)
    den = 1.0 + e2
    o_ref[...] = jnp.where(lane == 0, i1.astype(F32),
                           jnp.where(lane == 1, i2.astype(F32),
                                     jnp.where(lane == 2, 1.0 / den, jnp.where(lane == 3, e2 / den, 0.0))))


def _router(h, router_pad, *, tm, n_experts):
    M, D = h.shape
    return pl.pallas_call(
        functools.partial(_router_body, n_experts=n_experts),
        grid=(M // tm,),
        in_specs=[pl.BlockSpec((tm, D), lambda i: (i, 0)),
                  pl.BlockSpec((D, LANES), lambda i: (0, 0))],
        out_specs=pl.BlockSpec((tm, LANES), lambda i: (i, 0)),
        out_shape=jax.ShapeDtypeStruct((M, LANES), F32),
        compiler_params=_params("arbitrary"),
        name="router",
    )(h, router_pad)


def _ffn_body(h_ref, w1_ref, w3_ref, w2_ref, x_ref, g2_ref, o_ref):
    f = pl.program_id(1)

    @pl.when(f == 0)
    def _():
        o_ref[...] = jnp.zeros(o_ref.shape, F32)

    h = h_ref[...]
    a = _dot(h, w1_ref[...].astype(BF16))
    b = _dot(h, w3_ref[...].astype(BF16))
    act = a * jax.nn.sigmoid(a) * b
    o_ref[0] += _dot(act.astype(BF16), w2_ref[...].astype(BF16))

    @pl.when(f == pl.num_programs(1) - 1)
    def _():
        o_ref[0] = x_ref[0] + g2_ref[0] * o_ref[0]


def _ffn(h, w1, w3, w2, layer_idx, x, mod, g_chunk, *, tm, tf):
    B, T, D = x.shape
    tpb = T // tm
    F = w1.shape[2]
    per_tok = mod.shape[1] == T
    tg = tm if per_tok else 1
    gmap = (lambda i, f: (i // tpb, i % tpb, g_chunk)) if per_tok else (lambda i, f: (i // tpb, 0, g_chunk))
    return pl.pallas_call(
        _ffn_body,
        grid=(B * tpb, F // tf),
        in_specs=[pl.BlockSpec((tm, D), lambda i, f: (i, 0)),
                  pl.BlockSpec((None, D, tf), lambda i, f: (layer_idx, 0, f)),
                  pl.BlockSpec((None, D, tf), lambda i, f: (layer_idx, 0, f)),
                  pl.BlockSpec((None, tf, D), lambda i, f: (layer_idx, f, 0)),
                  pl.BlockSpec((1, tm, D), lambda i, f: (i // tpb, i % tpb, 0), pipeline_mode=pl.Buffered(1)),
                  pl.BlockSpec((1, tg, D), gmap)],
        out_specs=pl.BlockSpec((1, tm, D), lambda i, f: (i // tpb, i % tpb, 0)),
        out_shape=jax.ShapeDtypeStruct((B, T, D), F32),
        compiler_params=_params("arbitrary", "arbitrary"),
        name="ffn_dense",
    )(h, w1, w3, w2, x, mod)


def _moe_chunk(tm, n_f):
    return -(-(-(-tm // n_f)) // 8) * 8


def _moe_body(te_ref, dst_ref, nval_ref, nused_ref, h_hbm, w1_ref, w3_ref, w2_ref, rw_ref, out_hbm,
              xg_s, xb_s, acc_s, os_s, sem_in, sem_out, *, n_tok, n_f, n_tiles, sub):
    i = pl.program_id(0)
    f = pl.program_id(1)
    tm = xb_s.shape[0]
    chunk = _moe_chunk(tm, n_f)
    ts = n_f * chunk

    def gather(base, r):
        d = dst_ref[base + r]
        tok = jnp.maximum(d - jnp.where(d >= n_tok, n_tok, 0), 0)
        pltpu.make_async_copy(h_hbm.at[pl.ds(tok, 1), :], xg_s.at[pl.ds(r, 1), :], sem_in).start()

    def scatter(base, r):
        d = dst_ref[base + r]
        row = jnp.where(d >= 0, d, 2 * n_tok + r)
        pltpu.make_async_copy(os_s.at[pl.ds(r, 1), :], out_hbm.at[pl.ds(row, 1), :], sem_out).start()

    def wait_gather():
        pltpu.make_async_copy(h_hbm.at[pl.ds(0, ts), :], xg_s, sem_in).wait()

    def wait_scatter():
        pltpu.make_async_copy(os_s, out_hbm.at[pl.ds(0, ts), :], sem_out).wait()

    @pl.when((i == 0) & (f == 0))
    def _():
        os_s[...] = jnp.zeros(os_s.shape, F32)

        def start(r, c):
            gather(ts, r)
            return c

        lax.fori_loop(0, ts, start, 0)

    @pl.when(f == 0)
    def _():
        wait_gather()
        xb_s[...] = xg_s[0:tm, :].astype(BF16)
        acc_s[...] = jnp.zeros(acc_s.shape, F32)

    def move_rows():
        for k in range(chunk):
            r = f * chunk + k
            gather((i + 2) * ts, r)
            scatter(i * ts, r)

    def expert_rows(n):
        x = xb_s[0:n, :]
        a = _dot(x, w1_ref[...].astype(BF16))
        b = _dot(x, w3_ref[...].astype(BF16))
        act = a * jax.nn.sigmoid(a) * b
        acc_s[0:n, :] += _dot(act.astype(BF16), w2_ref[...].astype(BF16))

    used = i < nused_ref[0]
    short = nval_ref[i] <= sub

    @pl.when(used & jnp.logical_not(short))
    def _():
        move_rows()
        expert_rows(tm)

    @pl.when(used & short)
    def _():
        move_rows()
        expert_rows(sub)

    @pl.when(i >= nused_ref[0])
    def _():
        move_rows()

    @pl.when(f == n_f - 1)
    def _():
        wait_scatter()

        @pl.when(i < nused_ref[0])
        def _():
            os_s[0:tm, :] = acc_s[...] * rw_ref[...]

        @pl.when(i == n_tiles - 1)
        def _():
            def start(r, c):
                scatter((i + 1) * ts, r)
                return c

            lax.fori_loop(0, ts, start, 0)
            wait_scatter()
            wait_gather()


def _moe_sparse(h_all, top_i, top_w, w1, w3, w2, layer_idx, *, tm, sub, tf):
    M, D = h_all.shape
    nE, _, F = w1.shape[1:]
    n_asg = 2 * M
    n_tiles = -(-(n_asg + nE * (tm - 1)) // tm)
    m_pad = n_tiles * tm
    nf = F // tf
    ts = nf * _moe_chunk(tm, nf)
    e_flat = top_i.T.reshape(n_asg)
    onehot = (e_flat[:, None] == jnp.arange(nE, dtype=I32)[None, :]).astype(I32)
    cum = jnp.cumsum(onehot, axis=0)
    counts = cum[-1]
    rank = jnp.take_along_axis(cum, e_flat[:, None], axis=1)[:, 0] - 1
    padded = -(-counts // tm) * tm
    ends = jnp.cumsum(padded)
    starts = ends - padded
    pos = starts[e_flat] + rank
    dst = jnp.full((m_pad,), -1, I32).at[pos].set(jnp.arange(n_asg, dtype=I32))
    dst = jnp.pad(dst.reshape(n_tiles, tm), ((1, 1), (0, ts - tm)), constant_values=-1).reshape(-1)
    roww = jnp.zeros((m_pad,), F32).at[pos].set(top_w.T.reshape(n_asg))
    tile_start = jnp.arange(n_tiles, dtype=I32) * tm
    n_used = (ends[-1] // tm).astype(I32)
    tile_e = jnp.minimum(jnp.searchsorted(ends, tile_start, side="right").astype(I32), nE - 1)
    tile_e = jnp.where(tile_start < ends[-1], tile_e, tile_e[jnp.maximum(n_used - 1, 0)])
    nval = jnp.clip(counts[tile_e] - (tile_start - starts[tile_e]), 0, tm).astype(I32)

    def fidx(i, f, nused):
        return jnp.where(i < nused[0], f, nf - 1)

    grid_spec = pltpu.PrefetchScalarGridSpec(
        num_scalar_prefetch=4,
        grid=(n_tiles, nf),
        in_specs=[pl.BlockSpec(memory_space=pl.ANY),
                  pl.BlockSpec((None, None, D, tf), lambda i, f, te, ds_, nv, nu: (layer_idx, te[i], 0, fidx(i, f, nu))),
                  pl.BlockSpec((None, None, D, tf), lambda i, f, te, ds_, nv, nu: (layer_idx, te[i], 0, fidx(i, f, nu))),
                  pl.BlockSpec((None, None, tf, D), lambda i, f, te, ds_, nv, nu: (layer_idx, te[i], fidx(i, f, nu), 0)),
                  pl.BlockSpec((tm, 1), lambda i, f, te, ds_, nv, nu: (i, 0))],
        out_specs=pl.BlockSpec(memory_space=pl.ANY),
        scratch_shapes=[pltpu.VMEM((ts, D), F32), pltpu.VMEM((tm, D), BF16), pltpu.VMEM((tm, D), F32),
                        pltpu.VMEM((ts, D), F32), pltpu.SemaphoreType.DMA(()), pltpu.SemaphoreType.DMA(())],
    )
    return pl.pallas_call(
        functools.partial(_moe_body, n_tok=M, n_f=nf, n_tiles=n_tiles, sub=sub),
        grid_spec=grid_spec,
        out_shape=jax.ShapeDtypeStruct((n_asg + ts, D), F32),
        compiler_params=_params("arbitrary", "arbitrary"),
        name="moe_sparse",
    )(tile_e, dst, nval, n_used.reshape(1), h_all, w1, w3, w2, roww.reshape(m_pad, 1))


def _moe_combine_body(x_ref, g2_ref, a_ref, b_ref, o_ref):
    o_ref[0] = x_ref[0] + g2_ref[0] * (a_ref[...] + b_ref[...])


def _moe_combine(x, mod, g_chunk, y2, row0, n_tok, *, tt):
    B, T, D = x.shape
    per_tok = mod.shape[1] == T
    tg = tt if per_tok else 1
    tpb = T // tt
    assert row0 % tt == 0 and n_tok % tt == 0
    rb0, rb1 = row0 // tt, (n_tok + row0) // tt
    return pl.pallas_call(
        _moe_combine_body,
        grid=(B, tpb),
        in_specs=[pl.BlockSpec((1, tt, D), lambda b, t: (b, t, 0)),
                  pl.BlockSpec((1, tg, D), lambda b, t: (b, t if per_tok else 0, g_chunk)),
                  pl.BlockSpec((tt, D), lambda b, t: (rb0 + b * tpb + t, 0)),
                  pl.BlockSpec((tt, D), lambda b, t: (rb1 + b * tpb + t, 0))],
        out_specs=pl.BlockSpec((1, tt, D), lambda b, t: (b, t, 0)),
        out_shape=jax.ShapeDtypeStruct((B, T, D), F32),
        compiler_params=_params("arbitrary", "arbitrary"),
        name="moe_combine",
    )(x, mod, y2, y2)


def _tile(n, prefs):
    for p in prefs:
        if n % p == 0:
            return p
    return n


def _mixer_half(l, x, mod, P, attend, rglru, kv_prev, *, tt, tm):
    B, T, D = x.shape
    M = B * T
    h = _norm_mod(x, P["norm_g"][l, 0:1], mod, 1, 0, BF16, tt=tt).reshape(M, D)
    w_in = P["w_in_t"]
    (zug,) = _mm(h, w_in, l, 0, 2 * D, [F32], tm=tm, tn=1024, w_nk=True)
    (q,) = _mm(h, w_in, l, 2 * D, D, [BF16], tm=tm, tn=1024, w_nk=True)
    KD = P["n_kv_cols"] // 2
    depth = w_in.shape[0]
    tnk = _tile(KD, (1024,))
    k_all, k16 = _mm_stacked(h, w_in, l, depth, P["col_kv"], KD, kv_prev and kv_prev[0], tm=tm, tn=tnk, w_nk=True)
    v_all, v16 = _mm_stacked(h, w_in, l, depth, P["col_kv"] + KD, KD, kv_prev and kv_prev[1], tm=tm, tn=tnk,
                             w_nk=True)
    (qi,) = _mm(h, w_in, l, P["col_qi"], P["n_qi_cols"], [BF16], tm=tm, tn=1024, w_nk=True)
    (kiwi,) = _mm(h, w_in, l, P["col_ki"], LANES, [F32], tm=tm, tn=LANES, w_nk=True)
    (gab,) = _mm(h, w_in, l, P["col_ki"] + P["n_kiwi"], 2 * D, [F32], tm=tm, tn=1024, w_nk=True)
    y_rnn, h_last, new_hist = rglru(l, zug)
    y_att = attend(l, q, (k_all, v_all), (k16, v16), qi, kiwi)
    merged = _merge(y_rnn.reshape(M, D), y_att.reshape(M, D), P["w_branch_a"], P["w_branch_b"], gab, l,
                    tm=tm, tn=512)
    x = _proj_res(merged, P["w_out"], l, x, mod, 2, tm=tm, tn=1024)
    return x, h_last, new_hist, (k_all, v_all), kiwi


def kernel(x_prompt, x_sample, c_prompt, c_sample, cache_k, cache_v, cache_idx_k, state_h, state_conv,
           page_table, w_ada, b_ada, norm_g, w_in, conv_w, conv_b, lru_wa, lru_ba, lru_wx, lru_bx,
           lru_lambda, w_branch_a, w_branch_b, w_out, rel_bias, ffn_w1, ffn_w3, ffn_w2,
           moe_router, moe_w1, moe_w3, moe_w2, final_g):
    Bp, S, D = x_prompt.shape
    Bs = x_sample.shape[0]
    Mp = Bp * S
    depth = w_in.shape[0]
    n_heads = rel_bias.shape[1]
    n_phys, _, n_kv = cache_k.shape[1:4]
    KD = n_kv * HEAD_DIM
    n_idx_heads = (w_in.shape[2] - (5 * D + 2 * KD + IDX_DIM)) // (IDX_DIM + 1)
    n_pages = page_table.shape[1]
    past = n_pages * PAGE
    col_qi = 3 * D + 2 * KD
    col_ki = col_qi + n_idx_heads * IDX_DIM
    n_kiwi = IDX_DIM + n_idx_heads
    n_experts = moe_router.shape[2]

    P = dict(
        col_kv=3 * D, n_kv_cols=2 * KD, col_qi=col_qi, n_qi_cols=n_idx_heads * IDX_DIM,
        col_ki=col_ki, n_kiwi=n_kiwi, norm_g=norm_g, w_in_t=jnp.swapaxes(w_in, 1, 2), conv_w=conv_w, lru_wa=lru_wa, lru_wx=lru_wx,
        conv_b3=conv_b[:, None, :], lru_ba3=lru_ba[:, None, :], lru_bx3=lru_bx[:, None, :],
        lru_lambda3=lru_lambda[:, None, :],
        w_branch_a=w_branch_a, w_branch_b=w_branch_b, w_out=w_out,
    )
    router_pad = jnp.pad(moe_router, ((0, 0), (0, 0), (0, LANES - n_experts)))
    rbt_pad = jnp.pad(rel_bias.T, ((0, 0), (0, LANES - rel_bias.shape[0])))
    cache_idx_kt = jnp.swapaxes(cache_idx_k, 2, 3)
    cache_k4 = cache_k.reshape(depth, n_phys, PAGE * n_kv, HEAD_DIM)
    cache_v4 = cache_v.reshape(depth, n_phys, PAGE * n_kv, HEAD_DIM)

    n_c = Bp + Bs
    n_c_pad = -(-n_c // 8) * 8
    c_all = jnp.concatenate([c_prompt, c_sample, jnp.zeros((n_c_pad - n_c, D), F32)], axis=0)
    b_ada3 = b_ada[:, None, :]

    def attend_prompt(l, q, kv32, kv16, qi, kiwi):
        ki = kiwi[:, :IDX_DIM].astype(BF16).reshape(Bp, S, IDX_DIM)
        wit = kiwi[:, IDX_DIM:n_kiwi].reshape(Bp, S, n_idx_heads).transpose(0, 2, 1)
        return _attn_prompt(q.reshape(Bp, S, D), kv16[0].reshape(Bp, S, KD), kv16[1].reshape(Bp, S, KD),
                            qi.reshape(Bp, S, -1), ki, wit, rel_bias)

    def attend_sample(l, q, kv32, kv16, qi, kiwi):
        qi3 = qi.reshape(Bs, n_idx_heads, IDX_DIM)
        wcol = kiwi[:, IDX_DIM:n_kiwi].reshape(Bs, n_idx_heads, 1)
        kin3 = kiwi[:, :IDX_DIM].reshape(Bs, 1, IDX_DIM)
        scores = _smp_scores(page_table, qi3, wcol, kin3, cache_idx_kt, l)
        n_sel = min(TOPK_MAX, (past + 1) // 4)
        sel3 = _smp_select(scores.reshape(Bs, SCORE_ROWS * PAGE), n_sel).reshape(Bs, SCORE_ROWS, PAGE)
        q3 = q.reshape(Bs, n_heads, HEAD_DIM)
        kn_rep = jnp.repeat(kv32[0][l].reshape(Bs, n_kv, HEAD_DIM), KV_GROUP, axis=1)
        vn_rep = jnp.repeat(kv32[1][l].reshape(Bs, n_kv, HEAD_DIM), KV_GROUP, axis=1)
        return _smp_attn(page_table, rbt_pad, q3, kn_rep, vn_rep, sel3, cache_k4, cache_v4, l, n_kv)

    xp, xs = x_prompt, x_sample.reshape(1, Bs, D)
    outs_p, outs_s = [], []
    kvp = kvs = None
    for l in range(depth):
        (mod,) = _mm(c_all, w_ada, l, 0, 6 * D, [F32], tm=n_c_pad, tn=1024, bias=b_ada3, silu_in=True)
        mod_p = mod[:Bp].reshape(Bp, 1, 6 * D)
        mod_s = mod[Bp:n_c].reshape(1, Bs, 6 * D)
        hist_t = state_conv[l].transpose(1, 0, 2)

        xp, hp, cp, kvp, kiwip = _mixer_half(
            l, xp, mod_p, P, attend_prompt, lambda l_, zug: _rglru_prompt(zug.reshape(Bp, S, 2 * D), P, l_, tc=256),
            kvp, tt=512, tm=1024)

        def rglru_s(l_, zug, hist_t=hist_t):
            y, h = _rglru_sample(zug, hist_t, state_h[l_], P, l_)
            new_hist = jnp.concatenate([state_conv[l_][:, 1:], zug[:, None, :D]], axis=1)
            return y, h, new_hist

        xs, hs, cs, kvs, kiwis = _mixer_half(l, xs, mod_s, P, attend_sample, rglru_s, kvs, tt=Bs, tm=Bs)

        g2p, g2s = P["norm_g"][l, 1:2], P["norm_g"][l, 1:2]
        if l % 2 == 0:
            h2p = _norm_mod(xp, g2p, mod_p, 4, 3, BF16, tt=512).reshape(Mp, D)
            h2s = _norm_mod(xs, g2s, mod_s, 4, 3, BF16, tt=Bs).reshape(Bs, D)
            xp = _ffn(h2p, ffn_w1, ffn_w3, ffn_w2, l // 2, xp, mod_p, 5, tm=1024, tf=256)
            xs = _ffn(h2s, ffn_w1, ffn_w3, ffn_w2, l // 2, xs, mod_s, 5, tm=Bs, tf=512)
        else:
            n_tok = Mp + Bs
            h_all = _norm_mod_rows(xp, g2p, mod_p, 4, 3, n_tok, 0, None, tt=512)
            h_all = _norm_mod_rows(xs, g2s, mod_s, 4, 3, n_tok, Mp, h_all, tt=Bs)
            route = _router(h_all, router_pad[l // 2], tm=_tile(n_tok, (1024, 640, 512, 256, 128)),
                            n_experts=n_experts)
            y2 = _moe_sparse(h_all, route[:, 0:2].astype(I32), route[:, 2:4], moe_w1, moe_w3, moe_w2, l // 2,
                             tm=1024, sub=512, tf=256)
            xp = _moe_combine(xp, mod_p, 5, y2, 0, n_tok, tt=Bs)
            xs = _moe_combine(xs, mod_s, 5, y2, Mp, n_tok, tt=Bs)

        outs_p.append((kiwip[:, :IDX_DIM].reshape(Bp, S, IDX_DIM), hp.reshape(Bp, D), cp))
        outs_s.append((kiwis[:, :IDX_DIM].reshape(Bs, 1, IDX_DIM), hs, cs))
    y_prompt = _final_norm(xp, final_g[None, :], tt=512)
    y_sample = _final_norm(xs, final_g[None, :], tt=Bs).reshape(Bs, 1, D)
    stack = lambda outs, i: jnp.stack([o[i] for o in outs])
    kv_shape_p = (depth, Bp, S, n_kv, HEAD_DIM)
    kv_shape_s = (depth, Bs, 1, n_kv, HEAD_DIM)
    return (y_prompt, y_sample,
            kvp[0].reshape(kv_shape_p), kvp[1].reshape(kv_shape_p),
            stack(outs_p, 0), stack(outs_p, 1), stack(outs_p, 2),
            kvs[0].reshape(kv_shape_s), kvs[1].reshape(kv_shape_s),
            stack(outs_s, 0), stack(outs_s, 1), stack(outs_s, 2))
```
